```python
import math
import jax, jax.numpy as jnp
from jax import lax
import numpy as np

D_MODEL = 1024
BATCH = 8
SEQ = 2048
DEPTH = 4
DEC_BATCH = 128
DEC_SEQ = 8
PAST_LEN = 16384
PAGE_SIZE = 128

F32 = jnp.float32
N_META = 16
N_BRANCH = 4
MIX_W = D_MODEL // 4
N_HEADS = 4
HEAD_DIM = MIX_W // N_HEADS
CONV_W = 4
CHUNK = 64
LRU_C = 8.0
N_EXPERTS = 32
TOP_K = 4
D_FF = D_MODEL
SWIGLU_LIMIT = 7.0
SWIGLU_ALPHA = 1.702
MOE_BLOCK = 128
LN_EPS = 1e-5
NORM_EPS = 1e-6
DEEPNORM_ALPHA = (2 * DEPTH) ** 0.25
DEEPNORM_BETA = (8 * DEPTH) ** -0.25
IN_SIZES = (MIX_W, MIX_W,
            3 * MIX_W, MIX_W, N_HEADS, N_HEADS,
            MIX_W, MIX_W, MIX_W, MIX_W,
            MIX_W, MIX_W, MIX_W, N_HEADS, N_HEADS,
            N_BRANCH * D_MODEL)
IN_WIDTH = sum(IN_SIZES)

kernel_name = 'hybrid_lru_gdn_hgrn2_mlstm_moe_step'


def _layernorm(x, g, b):
    xf = x.astype(F32)
    mu = jnp.mean(xf, axis=-1, keepdims=True)
    var = jnp.mean(jnp.square(xf - mu), axis=-1, keepdims=True)
    return ((xf - mu) * lax.rsqrt(var + LN_EPS) * g + b).astype(x.dtype)


def _rmsnorm(x, g):
    return x * lax.rsqrt(jnp.mean(jnp.square(x), axis=-1, keepdims=True) + NORM_EPS) * g


def _headnorm(x, g):
    mu = jnp.mean(x, axis=-1, keepdims=True)
    var = jnp.mean(jnp.square(x - mu), axis=-1, keepdims=True)
    y = (x - mu) * lax.rsqrt(var + NORM_EPS)
    return y.reshape(x.shape[:2] + (-1,)) * g


def _l2norm(x):
    return x * lax.rsqrt(jnp.sum(jnp.square(x), axis=-1, keepdims=True) + NORM_EPS)


def _causal_conv(x, buf, w, b=None):
    L = x.shape[1]
    xp = jnp.concatenate([buf.astype(x.dtype), x], axis=1)
    y = sum(xp[:, j:j + L] * w[j] for j in range(CONV_W))
    if b is not None:
        y = y + b
    return y, xp[:, L:]


def _to_chunks(x, c):
    B, L, H = x.shape[:3]
    x = x.reshape((B, L // c, c, H) + x.shape[3:])
    return jnp.moveaxis(x, (1, 3), (0, 2))


def _from_chunks(y):
    y = jnp.moveaxis(y, (0, 2), (1, 3))
    n_b, n, c, h, d = y.shape
    return y.reshape(n_b, n * c, h, d)


def _segmented(body, carry, seqs, segs):
    outs = []
    start = 0
    for length, c in segs:
        part = tuple(_to_chunks(a[:, start:start + length], c) for a in seqs)
        carry, o = lax.scan(body, carry, part)
        outs.append(_from_chunks(o))
        start += length
    out = jnp.concatenate(outs, axis=1) if len(outs) > 1 else outs[0]
    return out, carry


def _masks(c):
    causal = jnp.tril(jnp.ones((c, c), dtype=bool))
    strict = jnp.tril(jnp.ones((c, c), dtype=bool), -1)
    return causal, strict


def _gdn_body(S, inp):
    qc, kc, vc, gc, bc = inp
    c = qc.shape[2]
    causal, strict = _masks(c)
    gcum = jnp.cumsum(gc, axis=-1)
    decay = jnp.exp(jnp.where(causal, gcum[..., :, None] - gcum[..., None, :], -jnp.inf))
    kk = jnp.einsum('bhtd,bhsd->bhts', kc, kc)
    lower = jnp.where(strict, bc[..., :, None] * kk * decay, 0.0)
    rhs = jnp.concatenate([bc[..., None] * vc, (bc * jnp.exp(gcum))[..., None] * kc], axis=-1)
    sol = lax.linalg.triangular_solve(jnp.eye(c, dtype=lower.dtype) + lower, rhs, left_side=True, lower=True)
    w_v, w_k = sol[..., :vc.shape[-1]], sol[..., vc.shape[-1]:]
    u = w_v - jnp.einsum('bhtk,bhkv->bhtv', w_k, S)
    qk = jnp.einsum('bhtd,bhsd->bhts', qc, kc) * decay
    o = jnp.einsum('bhtk,bhkv->bhtv', qc * jnp.exp(gcum)[..., None], S) + jnp.einsum('bhts,bhsv->bhtv', qk, u)
    g_last = gcum[..., -1]
    S_new = S * jnp.exp(g_last)[..., None, None] + jnp.einsum(
        'bhsk,bhsv->bhkv', kc * jnp.exp(g_last[..., None] - gcum)[..., None], u)
    return S_new, o


def _hgrn_body(S, inp):
    qc, kc, ic, lfc = inp
    causal, _ = _masks(qc.shape[2])
    b = jnp.cumsum(lfc, axis=2)
    diff = b[:, :, :, None, :] - b[:, :, None, :, :]
    decay = jnp.exp(jnp.where(causal[:, :, None], diff, -jnp.inf))
    att = jnp.einsum('bhtd,bhsd,bhtsd->bhts', qc, kc, decay)
    o = jnp.einsum('bhts,bhsv->bhtv', att, ic) + jnp.einsum('bhtk,bhkv->bhtv', qc * jnp.exp(b), S)
    b_last = b[:, :, -1]
    S_new = jnp.exp(b_last)[..., None] * S + jnp.einsum(
        'bhsk,bhsv->bhkv', kc * jnp.exp(b_last[:, :, None, :] - b), ic)
    return S_new, o


def _mlstm_body(carry, inp):
    C, n, m = carry
    qc, kc, vc, ic, lfc = inp
    causal, _ = _masks(qc.shape[2])
    b = jnp.cumsum(lfc, axis=-1)
    m_t = jnp.maximum(m[..., None] + b, b + lax.cummax(ic - b, axis=2))
    logw = ic[:, :, None, :] + b[:, :, :, None] - b[:, :, None, :] - m_t[:, :, :, None]
    w = jnp.exp(jnp.where(causal, logw, -jnp.inf))
    inter = jnp.exp(m[..., None] + b - m_t)
    qk = jnp.einsum('bhtd,bhsd->bhts', qc, kc) * w
    num = jnp.einsum('bhts,bhsv->bhtv', qk, vc) + inter[..., None] * jnp.einsum('bhtk,bhkv->bhtv', qc, C)
    den = jnp.sum(qk, axis=-1) + inter * jnp.einsum('bhtk,bhk->bht', qc, n)
    h = num / jnp.maximum(jnp.abs(den), jnp.exp(-m_t))[..., None]
    m_last = m_t[..., -1]
    wl = jnp.exp(ic + b[..., -1:] - b - m_last[..., None])
    dec = jnp.exp(m + b[..., -1] - m_last)
    C_new = dec[..., None, None] * C + jnp.einsum('bhs,bhsk,bhsv->bhkv', wl, kc, vc)
    n_new = dec[..., None] * n + jnp.einsum('bhs,bhsk->bhk', wl, kc)
    return (C_new, n_new, m_last), h


def _rglru(xa, h0, w_ra, b_ra, w_ix, b_ix, lam):
    B, L, W = xa.shape
    xh = xa.reshape(B, L, N_HEADS, HEAD_DIM)
    r = jax.nn.sigmoid(jnp.einsum('blhi,hij->blhj', xh, w_ra).reshape(B, L, W) + b_ra)
    i = jax.nn.sigmoid(jnp.einsum('blhi,hij->blhj', xh, w_ix).reshape(B, L, W) + b_ix)
    log_a = -LRU_C * r * jax.nn.softplus(-lam)
    a = jnp.exp(log_a)
    u = jnp.sqrt(-jnp.expm1(2.0 * log_a)) * (i * xa)

    def combine(lhs, rhs):
        a1, b1 = lhs
        a2, b2 = rhs
        return a1 * a2, a2 * b1 + b2

    a_cum, b_cum = lax.associative_scan(combine, (a, u), axis=1)
    h = a_cum * h0[:, None, :] + b_cum
    return h, h[:, -1]


def _moe(x2, w_router, b_router, w_up, b_up, w_down, b_down):
    T, D = x2.shape
    logits = jnp.dot(x2, w_router).astype(F32) + b_router.astype(F32)
    top_v, top_e = lax.top_k(logits, TOP_K)
    gates = jax.nn.softmax(top_v, axis=-1)
    M = T * TOP_K
    flat_e = top_e.reshape(M)
    order = jnp.argsort(flat_e)
    e_sorted = flat_e[order]
    tok_sorted = order // TOP_K
    gate_sorted = gates.reshape(M)[order]
    counts = jnp.bincount(flat_e, length=N_EXPERTS)
    starts = jnp.cumsum(counts) - counts
    padded = (counts + MOE_BLOCK - 1) // MOE_BLOCK * MOE_BLOCK
    pad_ends = jnp.cumsum(padded)
    row = pad_ends[e_sorted] - padded[e_sorted] + (jnp.arange(M) - starts[e_sorted])
    n_blocks = -(-M // MOE_BLOCK) + N_EXPERTS
    xs = jnp.zeros((n_blocks * MOE_BLOCK, D), x2.dtype).at[row].set(x2[tok_sorted])
    block_e = jnp.minimum(jnp.searchsorted(pad_ends, jnp.arange(n_blocks) * MOE_BLOCK, side='right'), N_EXPERTS - 1)

    def expert_block(args):
        xb, e = args
        h = xb @ w_up[e] + b_up[e]
        g, u = jnp.split(h, 2, axis=-1)
        g = jnp.minimum(g, SWIGLU_LIMIT)
        u = jnp.clip(u, -SWIGLU_LIMIT, SWIGLU_LIMIT)
        act = (u + 1.0) * g * jax.nn.sigmoid(SWIGLU_ALPHA * g)
        return act @ w_down[e] + b_down[e]

    ys = lax.map(expert_block, (xs.reshape(n_blocks, MOE_BLOCK, D), block_e)).reshape(-1, D)
    y = (ys[row] * gate_sorted[:, None]).astype(x2.dtype)
    return jnp.zeros_like(x2).at[tok_sorted].add(y)


def _mixer(x, lp, st, lb, segs):
    B, L, _ = x.shape
    lru_conv0, lru_h0, gdn_conv0, gdn_s0, hg_s0, ml_conv0, ml_c0, ml_n0, ml_m0 = [s.astype(F32) for s in st]
    proj = jnp.einsum('bld,de->ble', x, lp['w_in']).astype(F32)
    (a_x, a_gate, b_qkv, b_z, b_a, b_b, c_q, c_f, c_i, c_g,
     d_x, d_v, d_o, d_i, d_f, gate_logits) = jnp.split(proj, np.cumsum(IN_SIZES)[:-1].tolist(), axis=-1)

    def heads(t):
        return t.reshape(B, L, N_HEADS, HEAD_DIM)

    xa, lru_conv1 = _causal_conv(a_x, lru_conv0, lp['w_lru_conv'], lp['b_lru_conv'])
    h, lru_h1 = _rglru(xa, lru_h0, lp['w_lru_ra'], lp['b_lru_ra'], lp['w_lru_ix'], lp['b_lru_ix'], lp['lru_lambda'])
    y_a = h * jax.nn.gelu(a_gate)

    qkv, gdn_conv1 = _causal_conv(b_qkv, gdn_conv0, lp['w_gdn_conv'])
    qkv = jax.nn.silu(qkv)
    q, k, v = jnp.split(qkv, 3, axis=-1)
    q = _l2norm(heads(q)) * HEAD_DIM ** -0.5
    k = _l2norm(heads(k))
    g = -jnp.exp(lp['gdn_a_log']) * jax.nn.softplus(b_a + lp['gdn_dt_bias'])
    beta = jax.nn.sigmoid(b_b)
    o_b, gdn_s1 = _segmented(_gdn_body, gdn_s0, (q, k, heads(v), g, beta), segs)
    y_b = _rmsnorm(o_b, lp['gdn_norm_g']).reshape(B, L, MIX_W) * jax.nn.silu(b_z)

    lbh = lb.reshape(N_HEADS, HEAD_DIM)
    z = heads(c_f)
    log_f = jnp.logaddexp(jnp.log(lbh), jnp.log1p(-lbh) + jax.nn.log_sigmoid(z))
    k_c = (1.0 - lbh) * jax.nn.sigmoid(-z)
    o_c, hg_s1 = _segmented(_hgrn_body, hg_s0, (heads(c_q) * HEAD_DIM ** -0.5, k_c, heads(c_i), log_f), segs)
    y_c = _rmsnorm(o_c, lp['hg_norm_g']).reshape(B, L, MIX_W) * jax.nn.silu(c_g)

    xm, ml_conv1 = _causal_conv(d_x, ml_conv0, lp['w_ml_conv'], lp['b_ml_conv'])
    xm = heads(jax.nn.silu(xm))
    q_d = jnp.einsum('blhi,hij->blhj', xm, lp['w_ml_q'])
    k_d = jnp.einsum('blhi,hij->blhj', xm, lp['w_ml_k']) * HEAD_DIM ** -0.5
    log_i = d_i + lp['b_ml_i']
    log_fd = jax.nn.log_sigmoid(d_f + lp['b_ml_f'])
    h_d, (ml_c1, ml_n1, ml_m1) = _segmented(_mlstm_body, (ml_c0, ml_n0, ml_m0), (q_d, k_d, heads(d_v), log_i, log_fd), segs)
    y_d = _headnorm(h_d, lp['ml_norm_g']) * jax.nn.sigmoid(d_o)

    branches = jnp.stack([y_a, y_b, y_c, y_d], axis=2)
    up = jnp.einsum('blnm,nmd->blnd', branches, lp['w_branch'])
    gates = jax.nn.sigmoid(gate_logits.reshape(B, L, N_BRANCH, D_MODEL))
    merged = jnp.sum(up * gates, axis=2)
    out = jnp.einsum('bld,de->ble', merged, lp['w_out']).astype(x.dtype)
    new_state = (lru_conv1, lru_h1, gdn_conv1, gdn_s1, hg_s1, ml_conv1, ml_c1, ml_n1, ml_m1)
    return out, tuple(s.astype(x.dtype) for s in new_state)


def _trunk(x, states, segs, lb, P):
    collected = [[] for _ in states]
    for l in range(DEPTH):
        lp = {name: arr[l] for name, arr in P.items()}
        mix, new = _mixer(x, lp, tuple(s[l] for s in states), lb[l], segs)
        x = _layernorm(DEEPNORM_ALPHA * x + mix, lp['ln1_g'], lp['ln1_b'])
        ff = _moe(x.reshape(-1, D_MODEL), lp['w_router'], lp['b_router'], lp['w_up'], lp['b_up'],
                  lp['w_down'], lp['b_down']).reshape(x.shape)
        x = _layernorm(DEEPNORM_ALPHA * x + ff, lp['ln2_g'], lp['ln2_b'])
        for acc, s in zip(collected, new):
            acc.append(s)
    return x, tuple(jnp.stack(acc) for acc in collected)


def _zero_states(B, dtype):
    return (jnp.zeros((DEPTH, B, CONV_W - 1, MIX_W), dtype),
            jnp.zeros((DEPTH, B, MIX_W), dtype),
            jnp.zeros((DEPTH, B, CONV_W - 1, 3 * MIX_W), dtype),
            jnp.zeros((DEPTH, B, N_HEADS, HEAD_DIM, HEAD_DIM), dtype),
            jnp.zeros((DEPTH, B, N_HEADS, HEAD_DIM, HEAD_DIM), dtype),
            jnp.zeros((DEPTH, B, CONV_W - 1, MIX_W), dtype),
            jnp.zeros((DEPTH, B, N_HEADS, HEAD_DIM, HEAD_DIM), dtype),
            jnp.zeros((DEPTH, B, N_HEADS, HEAD_DIM), dtype),
            jnp.zeros((DEPTH, B, N_HEADS), dtype))


def setup_inputs(seed: int = 0) -> dict:
    key = jax.random.key(seed)
    ks = iter(jax.random.split(key, 64))

    def nrm(shape, scale):
        return scale * jax.random.normal(next(ks), shape, F32)

    def unif(shape, lo, hi):
        return jax.random.uniform(next(ks), shape, F32, lo, hi)

    H, Dh, W, D, E = N_HEADS, HEAD_DIM, MIX_W, D_MODEL, N_EXPERTS
    lru_r = unif((DEPTH, W), 0.9, 0.999) ** (1.0 / LRU_C)
    dt = jnp.exp(unif((DEPTH, H), math.log(1e-3), math.log(1e-1)))
    return {
        'x_prompt': nrm((BATCH, SEQ, D), 1.0),
        'x_sample': nrm((DEC_BATCH, DEC_SEQ, D), 1.0),
        'state_lru_conv': nrm((DEPTH, DEC_BATCH, CONV_W - 1, W), 1.0),
        'state_lru_h': nrm((DEPTH, DEC_BATCH, W), 0.5),
        'state_gdn_conv': nrm((DEPTH, DEC_BATCH, CONV_W - 1, 3 * W), 1.0),
        'state_gdn': nrm((DEPTH, DEC_BATCH, H, Dh, Dh), 0.3),
        'state_hgrn': nrm((DEPTH, DEC_BATCH, H, Dh, Dh), 0.3),
        'state_mlstm_conv': nrm((DEPTH, DEC_BATCH, CONV_W - 1, W), 1.0),
        'state_mlstm_c': nrm((DEPTH, DEC_BATCH, H, Dh, Dh), 0.3),
        'state_mlstm_n': nrm((DEPTH, DEC_BATCH, H, Dh), 0.3),
        'state_mlstm_m': nrm((DEPTH, DEC_BATCH, H), 1.0),
        'meta_tokens': nrm((N_META, D), 1.0),
        'ln_emb_g': 1.0 + nrm((D,), 0.02),
        'ln_emb_b': nrm((D,), 0.02),
        'hg_lb_logits': nrm((DEPTH, W), 1.0),
        'w_in': nrm((DEPTH, D, IN_WIDTH), D ** -0.5),
        'w_lru_conv': nrm((DEPTH, CONV_W, W), CONV_W ** -0.5),
        'b_lru_conv': nrm((DEPTH, W), 0.01),
        'w_lru_ra': nrm((DEPTH, H, Dh, Dh), Dh ** -0.5),
        'b_lru_ra': nrm((DEPTH, W), 0.01),
        'w_lru_ix': nrm((DEPTH, H, Dh, Dh), Dh ** -0.5),
        'b_lru_ix': nrm((DEPTH, W), 0.01),
        'lru_lambda': jnp.log(lru_r) - jnp.log1p(-lru_r),
        'w_gdn_conv': nrm((DEPTH, CONV_W, 3 * W), CONV_W ** -0.5),
        'gdn_a_log': jnp.log(unif((DEPTH, H), 1.0, 16.0)),
        'gdn_dt_bias': dt + jnp.log(-jnp.expm1(-dt)),
        'gdn_norm_g': 1.0 + nrm((DEPTH, Dh), 0.02),
        'hg_norm_g': 1.0 + nrm((DEPTH, Dh), 0.02),
        'w_ml_conv': nrm((DEPTH, CONV_W, W), CONV_W ** -0.5),
        'b_ml_conv': nrm((DEPTH, W), 0.01),
        'w_ml_q': nrm((DEPTH, H, Dh, Dh), Dh ** -0.5),
        'w_ml_k': nrm((DEPTH, H, Dh, Dh), Dh ** -0.5),
        'b_ml_i': nrm((DEPTH, H), 0.1),
        'b_ml_f': jnp.linspace(3.0, 6.0, H, dtype=F32)[None, :] + nrm((DEPTH, H), 0.01),
        'ml_norm_g': 1.0 + nrm((DEPTH, W), 0.02),
        'w_branch': nrm((DEPTH, N_BRANCH, W, D), W ** -0.5),
        'w_out': nrm((DEPTH, D, D), DEEPNORM_BETA * D ** -0.5),
        'ln1_g': 1.0 + nrm((DEPTH, D), 0.02),
        'ln1_b': nrm((DEPTH, D), 0.02),
        'ln2_g': 1.0 + nrm((DEPTH, D), 0.02),
        'ln2_b': nrm((DEPTH, D), 0.02),
        'w_router': nrm((DEPTH, D, E), D ** -0.5),
        'b_router': nrm((DEPTH, E), 0.01),
        'w_up': nrm((DEPTH, E, D, 2 * D_FF), D ** -0.5),
        'b_up': nrm((DEPTH, E, 2 * D_FF), 0.01),
        'w_down': nrm((DEPTH, E, D_FF, D), DEEPNORM_BETA * D_FF ** -0.5),
        'b_down': nrm((DEPTH, E, D), 0.01),
    }


def reference(x_prompt, x_sample, state_lru_conv, state_lru_h, state_gdn_conv, state_gdn, state_hgrn,
              state_mlstm_conv, state_mlstm_c, state_mlstm_n, state_mlstm_m, meta_tokens, ln_emb_g, ln_emb_b,
              hg_lb_logits, w_in, w_lru_conv, b_lru_conv, w_lru_ra, b_lru_ra, w_lru_ix, b_lru_ix, lru_lambda,
              w_gdn_conv, gdn_a_log, gdn_dt_bias, gdn_norm_g, hg_norm_g, w_ml_conv, b_ml_conv, w_ml_q, w_ml_k,
              b_ml_i, b_ml_f, ml_norm_g, w_branch, w_out, ln1_g, ln1_b, ln2_g, ln2_b, w_router, b_router,
              w_up, b_up, w_down, b_down):
    P = dict(w_in=w_in, w_lru_conv=w_lru_conv, b_lru_conv=b_lru_conv, w_lru_ra=w_lru_ra, b_lru_ra=b_lru_ra,
             w_lru_ix=w_lru_ix, b_lru_ix=b_lru_ix, lru_lambda=lru_lambda, w_gdn_conv=w_gdn_conv,
             gdn_a_log=gdn_a_log, gdn_dt_bias=gdn_dt_bias, gdn_norm_g=gdn_norm_g, hg_norm_g=hg_norm_g,
             w_ml_conv=w_ml_conv, b_ml_conv=b_ml_conv, w_ml_q=w_ml_q, w_ml_k=w_ml_k, b_ml_i=b_ml_i,
             b_ml_f=b_ml_f, ml_norm_g=ml_norm_g, w_branch=w_branch, w_out=w_out, ln1_g=ln1_g, ln1_b=ln1_b,
             ln2_g=ln2_g, ln2_b=ln2_b, w_router=w_router, b_router=b_router, w_up=w_up, b_up=b_up,
             w_down=w_down, b_down=b_down)
    lb_cum = jnp.cumsum(jax.nn.softmax(hg_lb_logits.astype(F32), axis=0), axis=0)
    lb = lb_cum - lb_cum[0:1]

    B, Lp, _ = x_prompt.shape
    meta = jnp.broadcast_to(meta_tokens.astype(x_prompt.dtype)[None], (B, N_META, D_MODEL))
    xp = _layernorm(jnp.concatenate([meta, x_prompt], axis=1), ln_emb_g, ln_emb_b)
    hp, new_p = _trunk(xp, _zero_states(B, xp.dtype), ((N_META, N_META), (Lp, CHUNK)), lb, P)

    Ls = x_sample.shape[1]
    xs = _layernorm(x_sample, ln_emb_g, ln_emb_b)
    states = (state_lru_conv, state_lru_h, state_gdn_conv, state_gdn, state_hgrn,
              state_mlstm_conv, state_mlstm_c, state_mlstm_n, state_mlstm_m)
    y_sample, new_s = _trunk(xs, states, ((Ls, CHUNK if Ls % CHUNK == 0 else Ls),), lb, P)

    y_prompt = hp[:, N_META:]
    (p_lru_conv, p_lru_h, p_gdn_conv, p_gdn, p_hgrn, p_ml_conv, p_ml_c, p_ml_n, p_ml_m) = new_p
    (s_lru_conv, s_lru_h, s_gdn_conv, s_gdn, s_hgrn, s_ml_conv, s_ml_c, s_ml_n, s_ml_m) = new_s
    return (y_prompt, y_sample, p_lru_conv, s_lru_conv, p_lru_h, s_lru_h, p_gdn_conv, s_gdn_conv,
            p_gdn, s_gdn, p_hgrn, s_hgrn, p_ml_conv, s_ml_conv, p_ml_c, s_ml_c, p_ml_n, s_ml_n, p_ml_m, s_ml_m)
```

```python
import functools

import jax
import jax.numpy as jnp
from jax import lax
from jax.experimental import pallas as pl
from jax.experimental.pallas import tpu as pltpu

F32 = jnp.float32
BF16 = jnp.bfloat16
HIGHEST = lax.Precision.HIGHEST

D_MODEL = 1024
DEPTH = 4
N_META = 16
N_BRANCH = 4
MIX_W = 256
N_HEADS = 4
HEAD_DIM = 64
CONV_W = 4
LRU_C = 8.0
N_EXPERTS = 32
TOP_K = 4
D_FF = 1024
SWIGLU_LIMIT = 7.0
SWIGLU_ALPHA = 1.702
LN_EPS = 1e-5
NORM_EPS = 1e-6
DEEPNORM_ALPHA = (2 * DEPTH) ** 0.25

LANES = 128
SUBLANES = 8
BD = 256
VMEM_LIMIT_BYTES = 56 * 1024 * 1024
PROMPT_CHUNK = 48
MOE_BLOCK_ROWS = 256

_A0, _B0, _C0, _D0, _G0 = 0, 512, 1544, 2568, 3344

_NN = (((1,), (0,)), ((), ()))
_NT = (((1,), (1,)), ((), ()))
_TN = (((0,), (0,)), ((), ()))


def _mm(a, b, dims=_NN, precision=None):
    return lax.dot_general(a, b, dims, precision=precision, preferred_element_type=F32)


def _bmm(a, b, dims=_NN):
    return _mm(a.astype(BF16), b.astype(BF16), dims)


def _mm_sel(x, sel, dims=_NN, terms=2):
    out = None
    r = x
    for _ in range(terms):
        p = r.astype(BF16)
        t = _mm(p, sel, dims)
        out = t if out is None else out + t
        r = r - p.astype(F32)
    return out


def _sigmoid(x):
    return jax.nn.sigmoid(x)


def _softplus(x):
    return jnp.maximum(x, 0.0) + jnp.log1p(jnp.exp(-jnp.abs(x)))


def _log_sigmoid(x):
    return jnp.minimum(x, 0.0) - jnp.log1p(jnp.exp(-jnp.abs(x)))


def _expm1(x):
    return jnp.tanh(0.5 * x) * (jnp.exp(x) + 1.0)


def _silu(x):
    return x * _sigmoid(x)


def _gelu_tanh(x):
    return 0.5 * x * (1.0 + jnp.tanh(0.7978845608028654 * (x + 0.044715 * (x * x * x))))


def _layernorm(x, g, b):
    mu = jnp.mean(x, axis=-1, keepdims=True)
    xc = x - mu
    var = jnp.mean(xc * xc, axis=-1, keepdims=True)
    return xc * lax.rsqrt(var + LN_EPS) * g + b


def _cparams(*sem):
    return pltpu.CompilerParams(dimension_semantics=sem, vmem_limit_bytes=VMEM_LIMIT_BYTES)


def _full(a):
    nd = a.ndim
    return pl.BlockSpec(a.shape, lambda *_: (0,) * nd)


def _ln_kernel(x_ref, g_ref, b_ref, o_ref):
    o_ref[...] = _layernorm(x_ref[...], g_ref[...], b_ref[...])


def _ln_call(x2, g, b, tm):
    t, d = x2.shape
    return pl.pallas_call(
        _ln_kernel,
        out_shape=jax.ShapeDtypeStruct((t, d), F32),
        grid=(t // tm,),
        in_specs=[pl.BlockSpec((tm, d), lambda i: (i, 0)), _full(g), _full(b)],
        out_specs=pl.BlockSpec((tm, d), lambda i: (i, 0)),
        compiler_params=_cparams("parallel"),
        name="ln_rows",
    )(x2, g, b)


def _ln2_kernel(x_ref, f_ref, g_ref, b_ref, o_ref):
    o_ref[...] = _layernorm(DEEPNORM_ALPHA * x_ref[...] + f_ref[...], g_ref[...], b_ref[...])


def _ln2_call(x2, ff, g, b, tm):
    t, d = x2.shape
    row = pl.BlockSpec((tm, d), lambda i: (i, 0))
    return pl.pallas_call(
        _ln2_kernel,
        out_shape=jax.ShapeDtypeStruct((t, d), F32),
        grid=(t // tm,),
        in_specs=[row, row, _full(g), _full(b)],
        out_specs=row,
        compiler_params=_cparams("parallel"),
        name="ln2_rows",
    )(x2, ff, g, b)


def _merge_kernel(x_ref, ya_ref, yb_ref, yc_ref, yd_ref, wg_ref, wbr_ref, wout_ref, g_ref, b_ref, wr_ref, br_ref,
                  x1_ref, x1b_ref, te_ref, tg_ref):
    x = x_ref[...]
    xb = x.astype(BF16)
    merged = None
    for n, y_ref in enumerate((ya_ref, yb_ref, yc_ref, yd_ref)):
        up = _mm(y_ref[...].astype(BF16), wbr_ref[n])
        gate = _sigmoid(_mm(xb, wg_ref[:, n * D_MODEL:(n + 1) * D_MODEL]))
        merged = up * gate if merged is None else merged + up * gate
    out = _mm(merged.astype(BF16), wout_ref[...])
    x1 = _layernorm(DEEPNORM_ALPHA * x + out, g_ref[...], b_ref[...])
    x1_ref[...] = x1
    x1b = x1.astype(BF16)
    x1b_ref[...] = x1b
    logits = _mm(x1b, wr_ref[...]) + br_ref[...]
    lane = lax.broadcasted_iota(jnp.int32, logits.shape, 1)
    vals = []
    idxs = []
    v = logits
    for _ in range(TOP_K):
        m = jnp.max(v, axis=-1, keepdims=True)
        idx = jnp.min(jnp.where(v == m, lane, LANES), axis=-1, keepdims=True)
        vals.append(m)
        idxs.append(idx)
        v = jnp.where(lane == idx, -jnp.inf, v)
    es = [jnp.exp(m - vals[0]) for m in vals]
    den = es[0] + es[1] + es[2] + es[3]
    te = jnp.zeros(logits.shape, jnp.int32)
    tg = jnp.zeros(logits.shape, F32)
    for j in range(TOP_K):
        te = jnp.where(lane == j, idxs[j], te)
        tg = jnp.where(lane == j, es[j] / den, tg)
    te_ref[...] = te
    tg_ref[...] = tg


def _merge_call(x2, ys, wg, wbr, wout, g, b, wr, br, tm):
    t, d = x2.shape
    row = pl.BlockSpec((tm, d), lambda i: (i, 0))
    yrow = pl.BlockSpec((tm, MIX_W), lambda i: (i, 0))
    lrow = pl.BlockSpec((tm, LANES), lambda i: (i, 0))
    return pl.pallas_call(
        _merge_kernel,
        out_shape=(jax.ShapeDtypeStruct((t, d), F32), jax.ShapeDtypeStruct((t, d), BF16),
                   jax.ShapeDtypeStruct((t, LANES), jnp.int32), jax.ShapeDtypeStruct((t, LANES), F32)),
        grid=(t // tm,),
        in_specs=[row, yrow, yrow, yrow, yrow, _full(wg), _full(wbr), _full(wout), _full(g), _full(b),
                  _full(wr), _full(br)],
        out_specs=(row, row, lrow, lrow),
        compiler_params=_cparams("parallel"),
        name="merge_ln_router",
    )(x2, *ys, wg, wbr, wout, g, b, wr, br)


def _moe_kernel(be_ref, nused_ref, xs_ref, gate_ref, wu_ref, bu_ref, wd_ref, bd_ref, o_ref):
    i = pl.program_id(0)

    @pl.when(i < nused_ref[0])
    def _():
        h = _mm(xs_ref[...], wu_ref[0]) + bu_ref[0]
        g = jnp.minimum(h[:, :D_FF], SWIGLU_LIMIT)
        u = jnp.clip(h[:, D_FF:], -SWIGLU_LIMIT, SWIGLU_LIMIT)
        act = (u + 1.0) * g * _sigmoid(SWIGLU_ALPHA * g)
        y = _mm(act.astype(BF16), wd_ref[0]) + bd_ref[0]
        o_ref[...] = y * gate_ref[...]

    @pl.when(i >= nused_ref[0])
    def _():
        o_ref[...] = jnp.zeros(o_ref.shape, F32)


def _moe_call(block_e, n_used, xs, gate_rows, wu, bu, wd, bd):
    mp, d = xs.shape
    nb = mp // MOE_BLOCK_ROWS
    grid_spec = pltpu.PrefetchScalarGridSpec(
        num_scalar_prefetch=2,
        grid=(nb,),
        in_specs=[
            pl.BlockSpec((MOE_BLOCK_ROWS, d), lambda i, be, nu: (i, 0)),
            pl.BlockSpec((MOE_BLOCK_ROWS, 1), lambda i, be, nu: (i, 0)),
            pl.BlockSpec((1, d, 2 * D_FF), lambda i, be, nu: (be[i], 0, 0)),
            pl.BlockSpec((1, 1, 2 * D_FF), lambda i, be, nu: (be[i], 0, 0)),
            pl.BlockSpec((1, D_FF, d), lambda i, be, nu: (be[i], 0, 0)),
            pl.BlockSpec((1, 1, d), lambda i, be, nu: (be[i], 0, 0)),
        ],
        out_specs=pl.BlockSpec((MOE_BLOCK_ROWS, d), lambda i, be, nu: (i, 0)),
    )
    return pl.pallas_call(
        _moe_kernel,
        out_shape=jax.ShapeDtypeStruct((mp, d), F32),
        grid_spec=grid_spec,
        compiler_params=_cparams("arbitrary"),
        name="moe_experts",
    )(block_e, n_used, xs, gate_rows, wu, bu, wd, bd)


def _chunk_pos(bb, c, width):
    return lax.broadcasted_iota(jnp.int32, (bb, c, width), 1).reshape(bb * c, width)


def _seg_scan(x, pos, c, op, ident):
    s = 1
    while s < c:
        x = op(x, jnp.where(pos >= s, pltpu.roll(x, s, 0), ident))
        s *= 2
    return x


def _chunk_last(x, pos, bb, c):
    w = x.shape[1]
    x3 = jnp.where(pos == c - 1, x, 0.0).reshape(bb, c, w)
    return jnp.broadcast_to(jnp.sum(x3, axis=1, keepdims=True), (bb, c, w)).reshape(bb * c, w)


def _causal_conv(x3, ext_ref, prev_ref, cw_ref, c):
    ext_ref[:, 0:SUBLANES, :] = prev_ref[...]
    ext_ref[:, SUBLANES:SUBLANES + c, :] = x3
    y = None
    for j in range(CONV_W):
        lo = SUBLANES - (CONV_W - 1) + j
        tap = ext_ref[:, lo:lo + c, :] * cw_ref[j:j + 1, :]
        y = tap if y is None else y + tap
    prev_ref[...] = ext_ref[:, c:c + SUBLANES, :]
    return y


def _head_masks(width=MIX_W):
    lane = lax.broadcasted_iota(jnp.int32, (1, width), 1)
    return [(lane >= h * HEAD_DIM) & (lane < (h + 1) * HEAD_DIM) for h in range(N_HEADS)]


def _lane_masks():
    lane = lax.broadcasted_iota(jnp.int32, (1, LANES), 1)
    return [lane == h for h in range(N_HEADS)]


def _stack(xg, masks, g, c, cs):
    pieces = []
    for b in range(g):
        xb = xg[b * c:(b + 1) * c]
        for m in masks:
            pieces.append(jnp.where(m, xb, 0.0))
            if cs > c:
                pieces.append(jnp.zeros((cs - c, xg.shape[1]), xg.dtype))
    return jnp.concatenate(pieces, axis=0)


def _unstack(y, g, c, cs):
    outs = []
    for b in range(g):
        acc = None
        for h in range(N_HEADS):
            r0 = (b * N_HEADS + h) * cs
            piece = y[r0:r0 + c]
            acc = piece if acc is None else acc + piece
        outs.append(acc)
    return outs[0] if g == 1 else jnp.concatenate(outs, axis=0)


def _bd_masks(cs):
    r = lax.broadcasted_iota(jnp.int32, (BD, BD), 0)
    q = lax.broadcasted_iota(jnp.int32, (BD, BD), 1)
    shift = cs.bit_length() - 1
    same = (r >> shift) == (q >> shift)
    tr = r & (cs - 1)
    tq = q & (cs - 1)
    return same & (tq <= tr), same & (tq < tr)


def _head_block_mask():
    r = lax.broadcasted_iota(jnp.int32, (MIX_W, MIX_W), 0)
    q = lax.broadcasted_iota(jnp.int32, (MIX_W, MIX_W), 1)
    return (r // HEAD_DIM) == (q // HEAD_DIM)


def _head_sum(x, ones_bd):
    return _mm_sel(x, ones_bd)


def _to_hb(cols, expand):
    return _mm_sel(cols, expand, terms=3)


def _stack_cols(cols_g, g, c, cs):
    st = _stack(cols_g, _lane_masks(), g, c, cs)
    col = jnp.sum(st, axis=1, keepdims=True)
    ones = jnp.ones((SUBLANES, LANES), F32)
    row = _mm(ones, st, _NT, precision=HIGHEST)[0:1, :]
    return col, row


def _group_rows(ref, gi, g, c):
    w = ref.shape[2]
    if g == 1:
        return ref[gi]
    return ref[pl.ds(gi * g, g)].reshape(g * c, w)


def _seq_specs(bb, c):
    x_spec = pl.BlockSpec((bb, c, D_MODEL), lambda i, k: (i, k, 0))
    y_spec = pl.BlockSpec((bb, c, MIX_W), lambda i, k: (i, k, 0))
    return x_spec, y_spec


def _state_spec(bb, *tail):
    nt = len(tail)
    return pl.BlockSpec((bb,) + tuple(tail), lambda i, k: (i,) + (0,) * nt)


def _lru_kernel(x_ref, w_ref, cw_ref, cb_ref, wra_ref, bra_ref, wix_ref, bix_ref, lam_ref, conv0_ref, h0_ref,
                y_ref, conv1_ref, h1_ref, ext_ref, prev_ref, hst_ref):
    k = pl.program_id(1)
    bb, c, _ = x_ref.shape
    n = bb * c

    @pl.when(k == 0)
    def _():
        prev_ref[...] = conv0_ref[...]
        hst_ref[...] = h0_ref[...]

    xb = x_ref[...].reshape(n, D_MODEL).astype(BF16)
    proj = _mm(xb, w_ref[...])
    a_gate = proj[:, MIX_W:]
    xa = _causal_conv(proj[:, :MIX_W].reshape(bb, c, MIX_W), ext_ref, prev_ref, cw_ref, c) + cb_ref[...]
    xa = xa.reshape(n, MIX_W)
    xab = xa.astype(BF16)
    r = _sigmoid(_mm(xab, wra_ref[...]) + bra_ref[...])
    i = _sigmoid(_mm(xab, wix_ref[...]) + bix_ref[...])
    log_a = -LRU_C * r * _softplus(-lam_ref[...])
    a_cum = jnp.exp(log_a)
    b_cum = jnp.sqrt(-_expm1(2.0 * log_a)) * (i * xa)
    pos = _chunk_pos(bb, c, MIX_W)
    s = 1
    while s < c:
        keep = pos >= s
        a_sh = pltpu.roll(a_cum, s, 0)
        b_sh = pltpu.roll(b_cum, s, 0)
        b_cum = jnp.where(keep, a_cum * b_sh + b_cum, b_cum)
        a_cum = jnp.where(keep, a_cum * a_sh, a_cum)
        s *= 2
    h0 = jnp.broadcast_to(hst_ref[:, SUBLANES - 1:SUBLANES, :], (bb, c, MIX_W)).reshape(n, MIX_W)
    h = a_cum * h0 + b_cum
    y_ref[...] = (h * _gelu_tanh(a_gate)).reshape(bb, c, MIX_W)
    hst_ref[...] = h.reshape(bb, c, MIX_W)[:, c - SUBLANES:c, :]

    @pl.when(k == pl.num_programs(1) - 1)
    def _():
        conv1_ref[...] = prev_ref[...]
        h1_ref[...] = hst_ref[...]


def _lru_call(x3, w, cw, cb, wra, bra, wix, bix, lam, conv0, h0, bb, c):
    b, l, _ = x3.shape
    x_spec, y_spec = _seq_specs(bb, c)
    st = _state_spec(bb, SUBLANES, MIX_W)
    consts = (w, cw, cb, wra, bra, wix, bix, lam)
    return pl.pallas_call(
        _lru_kernel,
        out_shape=(jax.ShapeDtypeStruct((b, l, MIX_W), F32), jax.ShapeDtypeStruct((b, SUBLANES, MIX_W), F32),
                   jax.ShapeDtypeStruct((b, SUBLANES, MIX_W), F32)),
        grid=(b // bb, l // c),
        in_specs=[x_spec] + [_full(a) for a in consts] + [st, st],
        out_specs=(y_spec, st, st),
        scratch_shapes=[pltpu.VMEM((bb, c + SUBLANES, MIX_W), F32), pltpu.VMEM((bb, SUBLANES, MIX_W), F32),
                        pltpu.VMEM((bb, SUBLANES, MIX_W), F32)],
        compiler_params=_cparams("parallel", "arbitrary"),
        name="mixer_rglru",
    )(x3, *consts, conv0, h0)


def _hgrn_kernel(x_ref, w_ref, loglb_ref, log1mlb_ref, onemlb_ref, ng_ref, ones_ref, st0_ref,
                 y_ref, st1_ref, st_ref, q_s, k_s, i_s, b_s, qe_s, kd_s, el_s, o_s):
    kk = pl.program_id(1)
    bb, c, _ = x_ref.shape
    n = bb * c
    nblk = c // SUBLANES

    @pl.when(kk == 0)
    def _():
        st_ref[...] = st0_ref[...]

    xb = x_ref[...].reshape(n, D_MODEL).astype(BF16)
    proj = _mm(xb, w_ref[...])
    q = proj[:, 0:MIX_W] * HEAD_DIM ** -0.5
    z = proj[:, MIX_W:2 * MIX_W]
    iv = proj[:, 2 * MIX_W:3 * MIX_W]
    cg = proj[:, 3 * MIX_W:4 * MIX_W]
    lo = loglb_ref[...]
    hi = log1mlb_ref[...] + _log_sigmoid(z)
    log_f = jnp.maximum(lo, hi) + jnp.log1p(jnp.exp(-jnp.abs(lo - hi)))
    kc = onemlb_ref[...] * _sigmoid(-z)
    pos = _chunk_pos(bb, c, MIX_W)
    bc = _seg_scan(log_f, pos, c, jnp.add, 0.0)
    b_last = _chunk_last(bc, pos, bb, c)
    ones_bd = ones_ref[...]

    sub = pos & (SUBLANES - 1)
    o_band = None
    for d in range(SUBLANES):
        kr = pltpu.roll(kc, d, 0) if d else kc
        br = pltpu.roll(bc, d, 0) if d else bc
        ir = pltpu.roll(iv, d, 0) if d else iv
        wd = jnp.where(sub >= d, q * kr * jnp.exp(jnp.minimum(bc - br, 0.0)), 0.0)
        term = _head_sum(wd, ones_bd) * ir
        o_band = term if o_band is None else o_band + term

    q_s[...] = q.reshape(bb, c, MIX_W)
    k_s[...] = kc.reshape(bb, c, MIX_W)
    i_s[...] = iv.reshape(bb, c, MIX_W)
    b_s[...] = bc.reshape(bb, c, MIX_W)
    qe_s[...] = (q * jnp.exp(bc)).reshape(bb, c, MIX_W)
    kd_s[...] = (kc * jnp.exp(b_last - bc)).reshape(bb, c, MIX_W)
    el_s[...] = jnp.exp(b_last).reshape(bb, c, MIX_W)

    hmasks = _head_masks()
    blockmask = _head_block_mask()
    pad_rows = LANES - c
    rowid = lax.broadcasted_iota(jnp.int32, (LANES, MIX_W), 0)
    zpad = jnp.zeros((pad_rows, MIX_W), F32)

    def per_b(b, carry):
        st = st_ref[b]
        o = _bmm(qe_s[b], st, _NT)
        if nblk > 1:
            kp = jnp.concatenate([k_s[b], zpad], axis=0)
            bp = jnp.concatenate([b_s[b], zpad], axis=0)
            ip = jnp.concatenate([i_s[b], zpad], axis=0).astype(BF16)
            pieces = [jnp.zeros((SUBLANES, MIX_W), F32)]
            for blk in range(1, nblk):
                r0 = blk * SUBLANES
                ref_b = b_s[b, r0 - 1:r0, :]
                qi = q_s[b, r0:r0 + SUBLANES, :] * jnp.exp(jnp.minimum(b_s[b, r0:r0 + SUBLANES, :] - ref_b, 0.0))
                ki = jnp.where(rowid < r0, kp * jnp.exp(jnp.minimum(ref_b - bp, 0.0)), 0.0)
                qst = jnp.concatenate([jnp.where(m, qi, 0.0) for m in hmasks], axis=0)
                att = _bmm(qst, ki, _NT)
                res = _mm(att.astype(BF16), ip)
                acc = None
                for h in range(N_HEADS):
                    part = jnp.where(hmasks[h], res[h * SUBLANES:(h + 1) * SUBLANES], 0.0)
                    acc = part if acc is None else acc + part
                pieces.append(acc)
            o = o + jnp.concatenate(pieces, axis=0)
        o_s[b] = o
        upd = _bmm(jnp.concatenate([i_s[b], zpad], axis=0), jnp.concatenate([kd_s[b], zpad], axis=0), _TN)
        st_ref[b] = st * el_s[b, 0:1, :] + jnp.where(blockmask, upd, 0.0)
        return carry

    lax.fori_loop(0, bb, per_b, 0)

    o = o_s[...].reshape(n, MIX_W) + o_band
    ms = _head_sum(o * o, ones_bd) * (1.0 / HEAD_DIM)
    y = o * lax.rsqrt(ms + NORM_EPS) * ng_ref[...] * _silu(cg)
    y_ref[...] = y.reshape(bb, c, MIX_W)

    @pl.when(kk == pl.num_programs(1) - 1)
    def _():
        st1_ref[...] = st_ref[...]


def _hgrn_call(x3, w, loglb, log1mlb, onemlb, ng, ones_bd, st0, bb, c):
    b, l, _ = x3.shape
    x_spec, y_spec = _seq_specs(bb, c)
    st = _state_spec(bb, MIX_W, MIX_W)
    consts = (w, loglb, log1mlb, onemlb, ng, ones_bd)
    rows = pltpu.VMEM((bb, c, MIX_W), F32)
    return pl.pallas_call(
        _hgrn_kernel,
        out_shape=(jax.ShapeDtypeStruct((b, l, MIX_W), F32), jax.ShapeDtypeStruct((b, MIX_W, MIX_W), F32)),
        grid=(b // bb, l // c),
        in_specs=[x_spec] + [_full(a) for a in consts] + [st],
        out_specs=(y_spec, st),
        scratch_shapes=[pltpu.VMEM((bb, MIX_W, MIX_W), F32)] + [rows] * 8,
        compiler_params=_cparams("parallel", "arbitrary"),
        name="mixer_hgrn2",
    )(x3, *consts, st0)


def _mlstm_kernel(g, cs, x_ref, w_ref, cw_ref, cb_ref, wq_ref, wk_ref, bi_ref, bf_ref, ng_ref, ones_ref, exp_ref,
                  conv0_ref, c0_ref, n0_ref, m0_ref,
                  y_ref, conv1_ref, c1_ref, n1_ref, m1_ref,
                  ext_ref, prev_ref, cst_ref, nst_ref, mst_ref, q_s, k_s, v_s, kw_s, dec_s, a_s, g_s, num_s, den_s,
                  qc_s):
    kk = pl.program_id(1)
    bb, c, _ = x_ref.shape
    n = bb * c
    ngroups = bb // g

    @pl.when(kk == 0)
    def _():
        prev_ref[...] = conv0_ref[...]
        cst_ref[...] = c0_ref[...]
        nst_ref[...] = n0_ref[...]
        mst_ref[...] = m0_ref[...]

    xb = x_ref[...].reshape(n, D_MODEL).astype(BF16)
    proj = _mm(xb, w_ref[...])
    xm = _causal_conv(proj[:, 0:MIX_W].reshape(bb, c, MIX_W), ext_ref, prev_ref, cw_ref, c) + cb_ref[...]
    xm = _silu(xm).reshape(n, MIX_W).astype(BF16)
    q = _mm(xm, wq_ref[...])
    k = _mm(xm, wk_ref[...]) * HEAD_DIM ** -0.5
    v = proj[:, MIX_W:2 * MIX_W]
    d_o = proj[:, 2 * MIX_W:3 * MIX_W]
    log_i = proj[:, 3 * MIX_W:3 * MIX_W + LANES] + bi_ref[...]
    log_f = _log_sigmoid(proj[:, 3 * MIX_W + LANES:3 * MIX_W + 2 * LANES] + bf_ref[...])
    pos = _chunk_pos(bb, c, LANES)
    bc = _seg_scan(log_f, pos, c, jnp.add, 0.0)
    gg = log_i - bc
    cm = _seg_scan(gg, pos, c, jnp.maximum, -jnp.inf)
    m0 = jnp.broadcast_to(mst_ref[:, SUBLANES - 1:SUBLANES, :], (bb, c, LANES)).reshape(n, LANES)
    m_t = jnp.maximum(m0 + bc, bc + cm)
    b_last = _chunk_last(bc, pos, bb, c)
    m_last = _chunk_last(m_t, pos, bb, c)
    expand = exp_ref[...]
    ones_bd = ones_ref[...]
    inter = _to_hb(jnp.exp(m0 + bc - m_t), expand)
    e_negm = _to_hb(jnp.exp(-m_t), expand)
    wl = _to_hb(jnp.exp(gg + b_last - m_last), expand)
    nrows = jnp.broadcast_to(nst_ref[:, 0:1, :], (bb, c, MIX_W)).reshape(n, MIX_W)
    qn = _head_sum(q * nrows, ones_bd)

    q_s[...] = q.reshape(bb, c, MIX_W)
    k_s[...] = k.reshape(bb, c, MIX_W)
    v_s[...] = v.reshape(bb, c, MIX_W)
    kw_s[...] = (k * wl).reshape(bb, c, MIX_W)
    dec_s[...] = _to_hb(jnp.exp(m0 + b_last - m_last), expand).reshape(bb, c, MIX_W)
    a_s[...] = (bc - m_t).reshape(bb, c, LANES)
    g_s[...] = gg.reshape(bb, c, LANES)

    hmasks = _head_masks()
    causal, _ = _bd_masks(cs)
    rowb = lax.broadcasted_iota(jnp.int32, (BD, MIX_W), 0) // (N_HEADS * cs)
    sel = _stack(jnp.ones((g * c, MIX_W), F32), hmasks, g, c, cs)

    def per_group(gi, carry):
        qg = _group_rows(q_s, gi, g, c)
        qst = _stack(qg, hmasks, g, c, cs).astype(BF16)
        kst = _stack(_group_rows(k_s, gi, g, c), hmasks, g, c, cs).astype(BF16)
        vst = _stack(_group_rows(v_s, gi, g, c), hmasks, g, c, cs).astype(BF16)
        kwst = _stack(_group_rows(kw_s, gi, g, c), hmasks, g, c, cs)
        acol, _ = _stack_cols(_group_rows(a_s, gi, g, c), g, c, cs)
        _, grow = _stack_cols(_group_rows(g_s, gi, g, c), g, c, cs)
        wmat = jnp.where(causal, jnp.exp(jnp.minimum(acol + grow, 0.0)), 0.0)
        qkw = _mm(qst, kst, _NT) * wmat
        num = _unstack(_mm(qkw.astype(BF16), vst), g, c, cs)
        den = _unstack(jnp.sum(qkw, axis=1, keepdims=True) * sel, g, c, cs)
        for bl in range(g):
            b = gi * g + bl
            cmat = cst_ref[b]
            qc_s[b] = _bmm(qg[bl * c:(bl + 1) * c], cmat)
            num_s[b] = num[bl * c:(bl + 1) * c]
            den_s[b] = den[bl * c:(bl + 1) * c]
            kwb = kwst if g == 1 else jnp.where(rowb == bl, kwst, 0.0)
            dec_row = dec_s[b, 0:1, :]
            cst_ref[b] = cmat * dec_row + _mm(kwb.astype(BF16), vst, _TN)
            ksum = jnp.sum(kw_s[b], axis=0, keepdims=True)
            nst_ref[b] = nst_ref[b] * dec_row + jnp.broadcast_to(ksum, (SUBLANES, MIX_W))
        return carry

    lax.fori_loop(0, ngroups, per_group, 0)

    num = num_s[...].reshape(n, MIX_W) + inter * qc_s[...].reshape(n, MIX_W)
    den = den_s[...].reshape(n, MIX_W) + inter * qn
    h = num / jnp.maximum(jnp.abs(den), e_negm)
    mu = _head_sum(h, ones_bd) * (1.0 / HEAD_DIM)
    hc = h - mu
    var = _head_sum(hc * hc, ones_bd) * (1.0 / HEAD_DIM)
    y = hc * lax.rsqrt(var + NORM_EPS) * ng_ref[...] * _sigmoid(d_o)
    y_ref[...] = y.reshape(bb, c, MIX_W)
    mst_ref[...] = m_t.reshape(bb, c, LANES)[:, c - SUBLANES:c, :]

    @pl.when(kk == pl.num_programs(1) - 1)
    def _():
        conv1_ref[...] = prev_ref[...]
        c1_ref[...] = cst_ref[...]
        n1_ref[...] = nst_ref[...]
        m1_ref[...] = mst_ref[...]


def _mlstm_call(x3, w, cw, cb, wq, wk, bi, bf, ng, ones_bd, expand, conv0, c0, n0, m0, bb, c, g, cs):
    b, l, _ = x3.shape
    x_spec, y_spec = _seq_specs(bb, c)
    st_conv = _state_spec(bb, SUBLANES, MIX_W)
    st_c = _state_spec(bb, MIX_W, MIX_W)
    st_n = _state_spec(bb, SUBLANES, MIX_W)
    st_m = _state_spec(bb, SUBLANES, LANES)
    consts = (w, cw, cb, wq, wk, bi, bf, ng, ones_bd, expand)
    rows = pltpu.VMEM((bb, c, MIX_W), F32)
    cols = pltpu.VMEM((bb, c, LANES), F32)
    return pl.pallas_call(
        functools.partial(_mlstm_kernel, g, cs),
        out_shape=(jax.ShapeDtypeStruct((b, l, MIX_W), F32), jax.ShapeDtypeStruct((b, SUBLANES, MIX_W), F32),
                   jax.ShapeDtypeStruct((b, MIX_W, MIX_W), F32), jax.ShapeDtypeStruct((b, SUBLANES, MIX_W), F32),
                   jax.ShapeDtypeStruct((b, SUBLANES, LANES), F32)),
        grid=(b // bb, l // c),
        in_specs=[x_spec] + [_full(a) for a in consts] + [st_conv, st_c, st_n, st_m],
        out_specs=(y_spec, st_conv, st_c, st_n, st_m),
        scratch_shapes=[pltpu.VMEM((bb, c + SUBLANES, MIX_W), F32), pltpu.VMEM((bb, SUBLANES, MIX_W), F32),
                        pltpu.VMEM((bb, MIX_W, MIX_W), F32), pltpu.VMEM((bb, SUBLANES, MIX_W), F32),
                        pltpu.VMEM((bb, SUBLANES, LANES), F32),
                        rows, rows, rows, rows, rows, cols, cols, rows, rows, rows],
        compiler_params=_cparams("parallel", "arbitrary"),
        name="mixer_mlstm",
    )(x3, *consts, conv0, c0, n0, m0)


def _gdn_kernel(g, cs, x_ref, w_ref, cw_ref, alog_ref, dtb_ref, ng_ref, ones_ref, exp_ref, conv0_ref, s0_ref,
                y_ref, conv1_ref, s1_ref,
                ext_ref, prev_ref, sst_ref, q_s, k_s, vb_s, kb_s, qe_s, kd_s, el_s, gc_s, be_s, o_s):
    kk = pl.program_id(1)
    bb, c, _ = x_ref.shape
    n = bb * c
    ngroups = bb // g
    qkv_w = 3 * MIX_W

    @pl.when(kk == 0)
    def _():
        prev_ref[...] = conv0_ref[...]
        sst_ref[...] = s0_ref[...]

    xb = x_ref[...].reshape(n, D_MODEL).astype(BF16)
    proj = _mm(xb, w_ref[...])
    qkv = _causal_conv(proj[:, 0:qkv_w].reshape(bb, c, qkv_w), ext_ref, prev_ref, cw_ref, c)
    qkv = _silu(qkv).reshape(n, qkv_w)
    ones_bd = ones_ref[...]
    expand = exp_ref[...]
    q = qkv[:, 0:MIX_W]
    k = qkv[:, MIX_W:2 * MIX_W]
    v = qkv[:, 2 * MIX_W:3 * MIX_W]
    q = q * lax.rsqrt(_head_sum(q * q, ones_bd) + NORM_EPS) * HEAD_DIM ** -0.5
    k = k * lax.rsqrt(_head_sum(k * k, ones_bd) + NORM_EPS)
    z = proj[:, qkv_w:qkv_w + MIX_W]
    a_in = proj[:, qkv_w + MIX_W:qkv_w + MIX_W + LANES]
    b_in = proj[:, qkv_w + MIX_W + LANES:qkv_w + MIX_W + 2 * LANES]
    gdec = -jnp.exp(alog_ref[...]) * _softplus(a_in + dtb_ref[...])
    beta = _sigmoid(b_in)
    pos = _chunk_pos(bb, c, LANES)
    gcum = _seg_scan(gdec, pos, c, jnp.add, 0.0)
    g_last = _chunk_last(gcum, pos, bb, c)
    beta_hb = _to_hb(beta, expand)
    eg_hb = _to_hb(jnp.exp(gcum), expand)

    q_s[...] = q.reshape(bb, c, MIX_W)
    k_s[...] = k.reshape(bb, c, MIX_W)
    vb_s[...] = (v * beta_hb).reshape(bb, c, MIX_W)
    kb_s[...] = (k * beta_hb * eg_hb).reshape(bb, c, MIX_W)
    qe_s[...] = (q * eg_hb).reshape(bb, c, MIX_W)
    kd_s[...] = (k * _to_hb(jnp.exp(g_last - gcum), expand)).reshape(bb, c, MIX_W)
    el_s[...] = _to_hb(jnp.exp(g_last), expand).reshape(bb, c, MIX_W)
    gc_s[...] = gcum.reshape(bb, c, LANES)
    be_s[...] = beta.reshape(bb, c, LANES)

    hmasks = _head_masks()
    causal, strict = _bd_masks(cs)
    rowb = lax.broadcasted_iota(jnp.int32, (BD, MIX_W), 0) // (N_HEADS * cs)
    rows_b = N_HEADS * cs

    def per_group(gi, carry):
        qg = _group_rows(qe_s, gi, g, c)
        qst = _stack(_group_rows(q_s, gi, g, c), hmasks, g, c, cs).astype(BF16)
        kst = _stack(_group_rows(k_s, gi, g, c), hmasks, g, c, cs).astype(BF16)
        kdst = _stack(_group_rows(kd_s, gi, g, c), hmasks, g, c, cs)
        rhs = jnp.concatenate([_stack(_group_rows(vb_s, gi, g, c), hmasks, g, c, cs),
                               _stack(_group_rows(kb_s, gi, g, c), hmasks, g, c, cs)], axis=1)
        gcol, grow = _stack_cols(_group_rows(gc_s, gi, g, c), g, c, cs)
        bcol, _ = _stack_cols(_group_rows(be_s, gi, g, c), g, c, cs)
        decay = jnp.exp(jnp.minimum(gcol - grow, 0.0))
        amat = jnp.where(strict, bcol * _mm(kst, kst, _NT) * decay, 0.0)
        x = rhs - _mm(amat, rhs, precision=HIGHEST)
        p = amat
        span = 2
        while span < cs:
            p = _mm(p, p, precision=HIGHEST)
            x = x + _mm(p, x, precision=HIGHEST)
            span *= 2
        w_v = x[:, 0:MIX_W]
        w_k = x[:, MIX_W:2 * MIX_W]
        us = []
        for bl in range(g):
            b = gi * g + bl
            r0 = bl * rows_b
            us.append(w_v[r0:r0 + rows_b] - _bmm(w_k[r0:r0 + rows_b], sst_ref[b]))
        u = us[0] if g == 1 else jnp.concatenate(us, axis=0)
        ub = u.astype(BF16)
        qk = jnp.where(causal, _mm(qst, kst, _NT) * decay, 0.0)
        o_intra = _unstack(_mm(qk.astype(BF16), ub), g, c, cs)
        for bl in range(g):
            b = gi * g + bl
            smat = sst_ref[b]
            o_s[b] = o_intra[bl * c:(bl + 1) * c] + _bmm(qg[bl * c:(bl + 1) * c], smat)
            kdb = kdst if g == 1 else jnp.where(rowb == bl, kdst, 0.0)
            sst_ref[b] = smat * el_s[b, 0:1, :] + _mm(kdb.astype(BF16), ub, _TN)
        return carry

    lax.fori_loop(0, ngroups, per_group, 0)

    o = o_s[...].reshape(n, MIX_W)
    ms = _head_sum(o * o, ones_bd) * (1.0 / HEAD_DIM)
    y = o * lax.rsqrt(ms + NORM_EPS) * ng_ref[...] * _silu(z)
    y_ref[...] = y.reshape(bb, c, MIX_W)

    @pl.when(kk == pl.num_programs(1) - 1)
    def _():
        conv1_ref[...] = prev_ref[...]
        s1_ref[...] = sst_ref[...]


def _gdn_call(x3, w, cw, alog, dtb, ng, ones_bd, expand, conv0, s0, bb, c, g, cs):
    b, l, _ = x3.shape
    x_spec, y_spec = _seq_specs(bb, c)
    st_conv = _state_spec(bb, SUBLANES, 3 * MIX_W)
    st_s = _state_spec(bb, MIX_W, MIX_W)
    consts = (w, cw, alog, dtb, ng, ones_bd, expand)
    rows = pltpu.VMEM((bb, c, MIX_W), F32)
    cols = pltpu.VMEM((bb, c, LANES), F32)
    return pl.pallas_call(
        functools.partial(_gdn_kernel, g, cs),
        out_shape=(jax.ShapeDtypeStruct((b, l, MIX_W), F32), jax.ShapeDtypeStruct((b, SUBLANES, 3 * MIX_W), F32),
                   jax.ShapeDtypeStruct((b, MIX_W, MIX_W), F32)),
        grid=(b // bb, l // c),
        in_specs=[x_spec] + [_full(a) for a in consts] + [st_conv, st_s],
        out_specs=(y_spec, st_conv, st_s),
        scratch_shapes=[pltpu.VMEM((bb, c + SUBLANES, 3 * MIX_W), F32), pltpu.VMEM((bb, SUBLANES, 3 * MIX_W), F32),
                        pltpu.VMEM((bb, MIX_W, MIX_W), F32),
                        rows, rows, rows, rows, rows, rows, rows, cols, cols, rows],
        compiler_params=_cparams("parallel", "arbitrary"),
        name="mixer_gdn",
    )(x3, *consts, conv0, s0)


def _pad_cols(a, width):
    return jnp.pad(a, ((0, 0), (0, width - a.shape[1])))


def _row(a, width=None):
    a = a.reshape(1, -1).astype(F32)
    return a if width is None else _pad_cols(a, width)


def _block_diag(w4):
    h, d, _ = w4.shape
    eye = jnp.eye(h, dtype=w4.dtype)
    return jnp.einsum("hij,hg->higj", w4, eye).reshape(h * d, h * d)


def _bd_state(s):
    b, h, d, _ = s.shape
    eye = jnp.eye(h, dtype=s.dtype)
    return jnp.einsum("bhij,hg->bhigj", s, eye).reshape(b, h * d, h * d)


def _bd_blocks(s):
    b = s.shape[0]
    s5 = s.reshape(b, N_HEADS, HEAD_DIM, N_HEADS, HEAD_DIM)
    return jnp.stack([s5[:, h, :, h, :] for h in range(N_HEADS)], axis=1)


def _tail8(buf):
    return jnp.pad(buf, ((0, 0), (SUBLANES - (CONV_W - 1), 0), (0, 0)))


def _bcast8(a):
    return jnp.broadcast_to(a[:, None, :], (a.shape[0], SUBLANES, a.shape[1]))


def _layer_params(P, lb, l):
    w_in = P["w_in"][l].astype(BF16)
    p = {}
    p["w_a"] = w_in[:, _A0:_B0]
    p["w_b"] = jnp.concatenate([w_in[:, _B0:_B0 + 4 * MIX_W],
                                _pad_cols(w_in[:, _B0 + 4 * MIX_W:_B0 + 4 * MIX_W + N_HEADS], LANES),
                                _pad_cols(w_in[:, _B0 + 4 * MIX_W + N_HEADS:_C0], LANES)], axis=1)
    p["w_c"] = w_in[:, _C0:_D0]
    p["w_d"] = jnp.concatenate([w_in[:, _D0:_D0 + 3 * MIX_W],
                                _pad_cols(w_in[:, _D0 + 3 * MIX_W:_D0 + 3 * MIX_W + N_HEADS], LANES),
                                _pad_cols(w_in[:, _D0 + 3 * MIX_W + N_HEADS:_G0], LANES)], axis=1)
    p["w_g"] = w_in[:, _G0:]
    p["lru_cw"] = P["w_lru_conv"][l]
    p["lru_cb"] = _row(P["b_lru_conv"][l])
    p["lru_wra"] = _block_diag(P["w_lru_ra"][l]).astype(BF16)
    p["lru_bra"] = _row(P["b_lru_ra"][l])
    p["lru_wix"] = _block_diag(P["w_lru_ix"][l]).astype(BF16)
    p["lru_bix"] = _row(P["b_lru_ix"][l])
    p["lru_lam"] = _row(P["lru_lambda"][l])
    p["gdn_cw"] = P["w_gdn_conv"][l]
    p["gdn_alog"] = _row(P["gdn_a_log"][l], LANES)
    p["gdn_dtb"] = _row(P["gdn_dt_bias"][l], LANES)
    p["gdn_ng"] = _row(jnp.tile(P["gdn_norm_g"][l], N_HEADS))
    p["hg_loglb"] = _row(jnp.log(lb[l]))
    p["hg_log1mlb"] = _row(jnp.log1p(-lb[l]))
    p["hg_onemlb"] = _row(1.0 - lb[l])
    p["hg_ng"] = _row(jnp.tile(P["hg_norm_g"][l], N_HEADS))
    p["ml_cw"] = P["w_ml_conv"][l]
    p["ml_cb"] = _row(P["b_ml_conv"][l])
    p["ml_wq"] = _block_diag(P["w_ml_q"][l]).astype(BF16)
    p["ml_wk"] = _block_diag(P["w_ml_k"][l]).astype(BF16)
    p["ml_bi"] = _row(P["b_ml_i"][l], LANES)
    p["ml_bf"] = _row(P["b_ml_f"][l], LANES)
    p["ml_ng"] = _row(P["ml_norm_g"][l])
    p["w_br"] = P["w_branch"][l].astype(BF16)
    p["w_out"] = P["w_out"][l].astype(BF16)
    p["ln1_g"] = _row(P["ln1_g"][l])
    p["ln1_b"] = _row(P["ln1_b"][l])
    p["ln2_g"] = _row(P["ln2_g"][l])
    p["ln2_b"] = _row(P["ln2_b"][l])
    p["w_r"] = _pad_cols(P["w_router"][l], LANES).astype(BF16)
    p["b_r"] = jnp.concatenate([P["b_router"][l].astype(F32), jnp.full((LANES - N_EXPERTS,), -1e30, F32)]).reshape(1, LANES)
    p["w_up"] = P["w_up"][l].astype(BF16)
    p["b_up"] = P["b_up"][l].reshape(N_EXPERTS, 1, 2 * D_FF)
    p["w_down"] = P["w_down"][l].astype(BF16)
    p["b_down"] = P["b_down"][l].reshape(N_EXPERTS, 1, D_MODEL)
    return p


def _mixers(x3, p, st, consts, bb, c, g, cs):
    ones_bd, expand = consts
    lru_conv, lru_h, gdn_conv, gdn_s, hg_st, ml_conv, ml_c, ml_n, ml_m = st
    y_a, lru_conv1, lru_h1 = _lru_call(x3, p["w_a"], p["lru_cw"], p["lru_cb"], p["lru_wra"], p["lru_bra"],
                                       p["lru_wix"], p["lru_bix"], p["lru_lam"], lru_conv, lru_h, bb, c)
    y_b, gdn_conv1, gdn_s1 = _gdn_call(x3, p["w_b"], p["gdn_cw"], p["gdn_alog"], p["gdn_dtb"], p["gdn_ng"],
                                       ones_bd, expand, gdn_conv, gdn_s, bb, c, g, cs)
    y_c, hg_st1 = _hgrn_call(x3, p["w_c"], p["hg_loglb"], p["hg_log1mlb"], p["hg_onemlb"], p["hg_ng"], ones_bd,
                             hg_st, bb, c)
    y_d, ml_conv1, ml_c1, ml_n1, ml_m1 = _mlstm_call(x3, p["w_d"], p["ml_cw"], p["ml_cb"], p["ml_wq"], p["ml_wk"],
                                                     p["ml_bi"], p["ml_bf"], p["ml_ng"], ones_bd, expand,
                                                     ml_conv, ml_c, ml_n, ml_m, bb, c, g, cs)
    return (y_a, y_b, y_c, y_d), (lru_conv1, lru_h1, gdn_conv1, gdn_s1, hg_st1, ml_conv1, ml_c1, ml_n1, ml_m1)


def _states_to_kernel(st):
    lru_conv, lru_h, gdn_conv, gdn_s, hg_s, ml_conv, ml_c, ml_n, ml_m = [s.astype(F32) for s in st]
    b = lru_h.shape[0]
    return (_tail8(lru_conv), _bcast8(lru_h), _tail8(gdn_conv), _bd_state(gdn_s),
            jnp.swapaxes(_bd_state(hg_s), 1, 2), _tail8(ml_conv), _bd_state(ml_c),
            _bcast8(ml_n.reshape(b, MIX_W)), _bcast8(_pad_cols(ml_m, LANES)))


def _states_from_kernel(st):
    lru_conv, lru_h, gdn_conv, gdn_s, hg_st, ml_conv, ml_c, ml_n, ml_m = st
    b = lru_h.shape[0]
    tail = SUBLANES - (CONV_W - 1)
    return (lru_conv[:, tail:], lru_h[:, SUBLANES - 1], gdn_conv[:, tail:], _bd_blocks(gdn_s),
            _bd_blocks(jnp.swapaxes(hg_st, 1, 2)), ml_conv[:, tail:], _bd_blocks(ml_c),
            ml_n[:, 0].reshape(b, N_HEADS, HEAD_DIM), ml_m[:, SUBLANES - 1, :N_HEADS])


def _route(top_e, top_g):
    t = top_e.shape[0]
    m = t * TOP_K
    flat_e = top_e.reshape(m)
    onehot = (flat_e[:, None] == jnp.arange(N_EXPERTS, dtype=jnp.int32)[None, :]).astype(jnp.int32)
    csum = jnp.cumsum(onehot, axis=0)
    rank = jnp.sum(onehot * csum, axis=1) - 1
    counts = csum[-1]
    padded = (counts + MOE_BLOCK_ROWS - 1) // MOE_BLOCK_ROWS * MOE_BLOCK_ROWS
    pad_ends = jnp.cumsum(padded)
    dest = (pad_ends - padded)[flat_e] + rank
    n_blocks = m // MOE_BLOCK_ROWS + N_EXPERTS
    mp = n_blocks * MOE_BLOCK_ROWS
    block_start = jnp.arange(n_blocks, dtype=jnp.int32) * MOE_BLOCK_ROWS
    block_e = jnp.minimum(jnp.sum((pad_ends[None, :] <= block_start[:, None]).astype(jnp.int32), axis=1),
                          N_EXPERTS - 1)
    n_used = (pad_ends[-1] // MOE_BLOCK_ROWS).astype(jnp.int32).reshape(1)
    src_tok = jnp.zeros((mp,), jnp.int32).at[dest].set(jnp.arange(m, dtype=jnp.int32) // TOP_K)
    gate_rows = jnp.zeros((mp,), F32).at[dest].set(top_g.reshape(m))
    return dest, src_tok, gate_rows.reshape(mp, 1), block_e, n_used


def _trunk_layer(xs, states, p, consts, cfgs):
    x1s, x1bs, tes, tgs, new_states = [], [], [], [], []
    for x3, st, (bb, c, g, cs, tm) in zip(xs, states, cfgs):
        b, l, _ = x3.shape
        ys, st1 = _mixers(x3, p, st, consts, bb, c, g, cs)
        x1, x1b, te, tg = _merge_call(x3.reshape(b * l, D_MODEL), [y.reshape(b * l, MIX_W) for y in ys],
                                      p["w_g"], p["w_br"], p["w_out"], p["ln1_g"], p["ln1_b"], p["w_r"], p["b_r"], tm)
        x1s.append(x1)
        x1bs.append(x1b)
        tes.append(te[:, :TOP_K])
        tgs.append(tg[:, :TOP_K])
        new_states.append(st1)
    x1b_all = jnp.concatenate(x1bs, axis=0)
    dest, src_tok, gate_rows, block_e, n_used = _route(jnp.concatenate(tes, axis=0), jnp.concatenate(tgs, axis=0))
    rows = jnp.take(x1b_all, src_tok, axis=0)
    ys = _moe_call(block_e, n_used, rows, gate_rows, p["w_up"], p["b_up"], p["w_down"], p["b_down"])
    ff_all = jnp.sum(jnp.take(ys, dest, axis=0).reshape(-1, TOP_K, D_MODEL), axis=1)
    outs = []
    off = 0
    for x3, x1, (bb, c, g, cs, tm) in zip(xs, x1s, cfgs):
        b, l, _ = x3.shape
        ff = lax.slice_in_dim(ff_all, off, off + b * l, axis=0)
        off += b * l
        outs.append(_ln2_call(x1, ff, p["ln2_g"], p["ln2_b"], tm).reshape(b, l, D_MODEL))
    return outs, new_states


def _row_tile(t):
    for tm in (512, 384, 256, 128, 64, 32, 16):
        if t % tm == 0:
            return tm
    raise ValueError(f"no row tile for {t} rows")


def _group_cfg(b, l):
    if l % PROMPT_CHUNK == 0:
        return (min(b, 8), PROMPT_CHUNK, 1, BD // N_HEADS, _row_tile(b * l))
    assert l == SUBLANES, "sequence length must be a multiple of the prompt chunk or one sublane tile"
    g = BD // (N_HEADS * SUBLANES)
    bb = min(b, 16)
    assert bb % g == 0 and b % bb == 0
    return (bb, SUBLANES, g, SUBLANES, _row_tile(b * l))


def _zero_states(b):
    return (jnp.zeros((b, CONV_W - 1, MIX_W), F32), jnp.zeros((b, MIX_W), F32),
            jnp.zeros((b, CONV_W - 1, 3 * MIX_W), F32), jnp.zeros((b, N_HEADS, HEAD_DIM, HEAD_DIM), F32),
            jnp.zeros((b, N_HEADS, HEAD_DIM, HEAD_DIM), F32), jnp.zeros((b, CONV_W - 1, MIX_W), F32),
            jnp.zeros((b, N_HEADS, HEAD_DIM, HEAD_DIM), F32), jnp.zeros((b, N_HEADS, HEAD_DIM), F32),
            jnp.zeros((b, N_HEADS), F32))


def kernel(x_prompt, x_sample, state_lru_conv, state_lru_h, state_gdn_conv, state_gdn, state_hgrn, state_mlstm_conv, state_mlstm_c, state_mlstm_n, state_mlstm_m, meta_tokens, ln_emb_g, ln_emb_b, hg_lb_logits, w_in, w_lru_conv, b_lru_conv, w_lru_ra, b_lru_ra, w_lru_ix, b_lru_ix, lru_lambda, w_gdn_conv, gdn_a_log, gdn_dt_bias, gdn_norm_g, hg_norm_g, w_ml_conv, b_ml_conv, w_ml_q, w_ml_k, b_ml_i, b_ml_f, ml_norm_g, w_branch, w_out, ln1_g, ln1_b, ln2_g, ln2_b, w_router, b_router, w_up, b_up, w_down, b_down):
    P = dict(w_in=w_in, w_lru_conv=w_lru_conv, b_lru_conv=b_lru_conv, w_lru_ra=w_lru_ra, b_lru_ra=b_lru_ra,
             w_lru_ix=w_lru_ix, b_lru_ix=b_lru_ix, lru_lambda=lru_lambda, w_gdn_conv=w_gdn_conv,
             gdn_a_log=gdn_a_log, gdn_dt_bias=gdn_dt_bias, gdn_norm_g=gdn_norm_g, hg_norm_g=hg_norm_g,
             w_ml_conv=w_ml_conv, b_ml_conv=b_ml_conv, w_ml_q=w_ml_q, w_ml_k=w_ml_k, b_ml_i=b_ml_i,
             b_ml_f=b_ml_f, ml_norm_g=ml_norm_g, w_branch=w_branch, w_out=w_out, ln1_g=ln1_g, ln1_b=ln1_b,
             ln2_g=ln2_g, ln2_b=ln2_b, w_router=w_router, b_router=b_router, w_up=w_up, b_up=b_up,
             w_down=w_down, b_down=b_down)
    depth = w_in.shape[0]
    lb_cum = jnp.cumsum(jax.nn.softmax(hg_lb_logits.astype(F32), axis=0), axis=0)
    lb = lb_cum - lb_cum[0:1]

    lane = jnp.arange(MIX_W) // HEAD_DIM
    ones_bd = (lane[:, None] == lane[None, :]).astype(BF16)
    expand = (jnp.arange(LANES)[:, None] == lane[None, :]).astype(BF16)
    consts = (ones_bd, expand)

    bp, lp0, _ = x_prompt.shape
    bs, ls, _ = x_sample.shape
    meta = jnp.broadcast_to(meta_tokens.astype(F32)[None], (bp, N_META, D_MODEL))
    xp_in = jnp.concatenate([meta, x_prompt], axis=1)
    lp = lp0 + N_META
    cfgs = [_group_cfg(bp, lp), _group_cfg(bs, ls)]
    g_emb, b_emb = _row(ln_emb_g), _row(ln_emb_b)
    xp = _ln_call(xp_in.reshape(bp * lp, D_MODEL), g_emb, b_emb, cfgs[0][4]).reshape(bp, lp, D_MODEL)
    xs_ = _ln_call(x_sample.reshape(bs * ls, D_MODEL), g_emb, b_emb, cfgs[1][4]).reshape(bs, ls, D_MODEL)

    sample_states = (state_lru_conv, state_lru_h, state_gdn_conv, state_gdn, state_hgrn,
                     state_mlstm_conv, state_mlstm_c, state_mlstm_n, state_mlstm_m)
    zero_p = _zero_states(bp)
    xs = [xp, xs_]
    collected = ([], [])
    for l in range(depth):
        p = _layer_params(P, lb, l)
        states = [_states_to_kernel(zero_p), _states_to_kernel(tuple(s[l] for s in sample_states))]
        xs, new_states = _trunk_layer(xs, states, p, consts, cfgs)
        for grp in range(2):
            collected[grp].append(_states_from_kernel(new_states[grp]))
    new_p = tuple(jnp.stack([layer[i] for layer in collected[0]]) for i in range(9))
    new_s = tuple(jnp.stack([layer[i] for layer in collected[1]]) for i in range(9))
    y_prompt = xs[0][:, N_META:]
    y_sample = xs[1]
    out = [y_prompt, y_sample]
    for i in range(9):
        out.append(new_p[i])
        out.append(new_s[i])
    return tuple(out)
```

```python
import functools

import jax
import jax.numpy as jnp
from jax import lax
from jax.experimental import pallas as pl
from jax.experimental.pallas import tpu as pltpu

F32 = jnp.float32
BF16 = jnp.bfloat16
HIGHEST = lax.Precision.HIGHEST

D_MODEL = 1024
DEPTH = 4
N_META = 16
N_BRANCH = 4
MIX_W = 256
N_HEADS = 4
HEAD_DIM = 64
CONV_W = 4
LRU_C = 8.0
N_EXPERTS = 32
TOP_K = 4
D_FF = 1024
SWIGLU_LIMIT = 7.0
SWIGLU_ALPHA = 1.702
LN_EPS = 1e-5
NORM_EPS = 1e-6
DEEPNORM_ALPHA = (2 * DEPTH) ** 0.25

LANES = 128
SUBLANES = 8
BD = 256
VMEM_LIMIT_BYTES = 56 * 1024 * 1024
PROMPT_CHUNK = 48
MOE_BLOCK_ROWS = 256
COMBINE_ROWS = 128

_A0, _B0, _C0, _D0, _G0 = 0, 512, 1544, 2568, 3344

_NN = (((1,), (0,)), ((), ()))
_NT = (((1,), (1,)), ((), ()))
_TN = (((0,), (0,)), ((), ()))


def _mm(a, b, dims=_NN, precision=None):
    return lax.dot_general(a, b, dims, precision=precision, preferred_element_type=F32)


def _bmm(a, b, dims=_NN):
    return _mm(a.astype(BF16), b.astype(BF16), dims)


def _split_bf16(x):
    hi = x.astype(BF16)
    return hi, (x - hi.astype(F32)).astype(BF16)


def _mm3(a, b, dims=_NN):
    ah, al = _split_bf16(a)
    bh, bl = _split_bf16(b)
    return _mm(ah, bh, dims) + (_mm(al, bh, dims) + _mm(ah, bl, dims))


def _mm_sel(x, sel, dims=_NN, terms=2):
    out = None
    r = x
    for _ in range(terms):
        p = r.astype(BF16)
        t = _mm(p, sel, dims)
        out = t if out is None else out + t
        r = r - p.astype(F32)
    return out


def _sigmoid(x):
    return jax.nn.sigmoid(x)


def _softplus(x):
    return jnp.maximum(x, 0.0) + jnp.log1p(jnp.exp(-jnp.abs(x)))


def _log_sigmoid(x):
    return jnp.minimum(x, 0.0) - jnp.log1p(jnp.exp(-jnp.abs(x)))


def _expm1(x):
    return jnp.tanh(0.5 * x) * (jnp.exp(x) + 1.0)


def _silu(x):
    return x * _sigmoid(x)


def _gelu_tanh(x):
    return 0.5 * x * (1.0 + jnp.tanh(0.7978845608028654 * (x + 0.044715 * (x * x * x))))


def _layernorm(x, g, b):
    mu = jnp.mean(x, axis=-1, keepdims=True)
    xc = x - mu
    var = jnp.mean(xc * xc, axis=-1, keepdims=True)
    return xc * lax.rsqrt(var + LN_EPS) * g + b


def _cparams(*sem):
    return pltpu.CompilerParams(dimension_semantics=sem, vmem_limit_bytes=VMEM_LIMIT_BYTES)


def _full(a):
    nd = a.ndim
    return pl.BlockSpec(a.shape, lambda *_: (0,) * nd)


def _ln_kernel(x_ref, g_ref, b_ref, o_ref):
    o_ref[...] = _layernorm(x_ref[...], g_ref[...], b_ref[...])


def _ln_call(x2, g, b, tm):
    t, d = x2.shape
    return pl.pallas_call(
        _ln_kernel,
        out_shape=jax.ShapeDtypeStruct((t, d), F32),
        grid=(t // tm,),
        in_specs=[pl.BlockSpec((tm, d), lambda i: (i, 0)), _full(g), _full(b)],
        out_specs=pl.BlockSpec((tm, d), lambda i: (i, 0)),
        compiler_params=_cparams("parallel"),
        name="ln_rows",
    )(x2, g, b)


def _combine_ln2_kernel(n, dcur_ref, dnxt_ref, x_ref, gate_ref, g_ref, b_ref, ys_hbm, o_ref, buf_ref, sem_ref):
    i = pl.program_id(0)
    tm = x_ref.shape[0]
    nrow = TOP_K * tm

    def row_copy(d_ref, r, slot):
        return pltpu.make_async_copy(ys_hbm.at[pl.ds(d_ref[0, 0, r], 1)], buf_ref.at[slot, pl.ds(r, 1)],
                                     sem_ref.at[slot])

    def issue(d_ref, slot):
        def body(r, carry):
            row_copy(d_ref, r, slot).start()
            return carry

        lax.fori_loop(0, nrow, body, 0, unroll=8)

    @pl.when(i == 0)
    def _():
        issue(dcur_ref, 0)

    @pl.when(i + 1 < n)
    def _():
        issue(dnxt_ref, (i + 1) % 2)

    slot = i % 2

    def wait_body(r, carry):
        row_copy(dcur_ref, r, slot).wait()
        return carry

    lax.fori_loop(0, nrow, wait_body, 0, unroll=8)
    gate = gate_ref[...]
    ff = None
    for k in range(TOP_K):
        term = buf_ref[slot, pl.ds(k * tm, tm), :] * gate[:, k:k + 1]
        ff = term if ff is None else ff + term
    o_ref[...] = _layernorm(DEEPNORM_ALPHA * x_ref[...] + ff, g_ref[...], b_ref[...])


def _combine_ln2_call(x2, gates, dest_tiles, ys, g, b):
    t, d = x2.shape
    tm = COMBINE_ROWS
    n = t // tm
    row = pl.BlockSpec((tm, d), lambda i: (i, 0))
    dspec = lambda f: pl.BlockSpec((1, 1, TOP_K * tm), f, memory_space=pltpu.SMEM)
    return pl.pallas_call(
        functools.partial(_combine_ln2_kernel, n),
        out_shape=jax.ShapeDtypeStruct((t, d), F32),
        grid=(n,),
        in_specs=[dspec(lambda i: (i, 0, 0)), dspec(lambda i: (jnp.minimum(i + 1, n - 1), 0, 0)), row,
                  pl.BlockSpec((tm, LANES), lambda i: (i, 0)), _full(g), _full(b),
                  pl.BlockSpec(memory_space=pl.ANY)],
        out_specs=row,
        scratch_shapes=[pltpu.VMEM((2, TOP_K * tm, d), F32), pltpu.SemaphoreType.DMA((2,))],
        compiler_params=_cparams("arbitrary"),
        name="combine_ln2",
    )(dest_tiles, dest_tiles, x2, gates, g, b, ys)


def _merge_kernel(x_ref, ya_ref, yb_ref, yc_ref, yd_ref, wg_ref, wbr_ref, wout_ref, g_ref, b_ref, wr_ref, br_ref,
                  x1_ref, x1b_ref, te_ref, tg_ref):
    x = x_ref[...]
    xb = x.astype(BF16)
    merged = None
    for n, y_ref in enumerate((ya_ref, yb_ref, yc_ref, yd_ref)):
        up = _mm(y_ref[...].astype(BF16), wbr_ref[n])
        gate = _sigmoid(_mm(xb, wg_ref[:, n * D_MODEL:(n + 1) * D_MODEL]))
        merged = up * gate if merged is None else merged + up * gate
    out = _mm(merged.astype(BF16), wout_ref[...])
    x1 = _layernorm(DEEPNORM_ALPHA * x + out, g_ref[...], b_ref[...])
    x1_ref[...] = x1
    x1b = x1.astype(BF16)
    x1b_ref[...] = x1b
    logits = _mm(x1b, wr_ref[...]) + br_ref[...]
    lane = lax.broadcasted_iota(jnp.int32, logits.shape, 1)
    vals = []
    idxs = []
    v = logits
    for _ in range(TOP_K):
        m = jnp.max(v, axis=-1, keepdims=True)
        idx = jnp.min(jnp.where(v == m, lane, LANES), axis=-1, keepdims=True)
        vals.append(m)
        idxs.append(idx)
        v = jnp.where(lane == idx, -jnp.inf, v)
    es = [jnp.exp(m - vals[0]) for m in vals]
    den = es[0] + es[1] + es[2] + es[3]
    te = jnp.zeros(logits.shape, jnp.int32)
    tg = jnp.zeros(logits.shape, F32)
    for j in range(TOP_K):
        te = jnp.where(lane == j, idxs[j], te)
        tg = jnp.where(lane == j, es[j] / den, tg)
    te_ref[...] = te
    tg_ref[...] = tg


def _merge_call(x2, ys, wg, wbr, wout, g, b, wr, br, tm):
    t, d = x2.shape
    row = pl.BlockSpec((tm, d), lambda i: (i, 0))
    yrow = pl.BlockSpec((tm, MIX_W), lambda i: (i, 0))
    lrow = pl.BlockSpec((tm, LANES), lambda i: (i, 0))
    return pl.pallas_call(
        _merge_kernel,
        out_shape=(jax.ShapeDtypeStruct((t, d), F32), jax.ShapeDtypeStruct((t, d), BF16),
                   jax.ShapeDtypeStruct((t, LANES), jnp.int32), jax.ShapeDtypeStruct((t, LANES), F32)),
        grid=(t // tm,),
        in_specs=[row, yrow, yrow, yrow, yrow, _full(wg), _full(wbr), _full(wout), _full(g), _full(b),
                  _full(wr), _full(br)],
        out_specs=(row, row, lrow, lrow),
        compiler_params=_cparams("parallel"),
        name="merge_ln_router",
    )(x2, *ys, wg, wbr, wout, g, b, wr, br)


def _moe_kernel(be_ref, nused_ref, xs_ref, wu_ref, bu_ref, wd_ref, bd_ref, o_ref):
    i = pl.program_id(0)

    @pl.when(i < nused_ref[0])
    def _():
        h = _mm(xs_ref[...], wu_ref[0, 0]) + bu_ref[0, 0]
        g = jnp.minimum(h[:, :D_FF], SWIGLU_LIMIT)
        u = jnp.clip(h[:, D_FF:], -SWIGLU_LIMIT, SWIGLU_LIMIT)
        act = (u + 1.0) * g * _sigmoid(SWIGLU_ALPHA * g)
        o_ref[...] = _mm(act.astype(BF16), wd_ref[0, 0]) + bd_ref[0, 0]

    @pl.when(i >= nused_ref[0])
    def _():
        o_ref[...] = jnp.zeros(o_ref.shape, F32)


def _moe_call(layer, block_e, n_used, xs, wu, bu, wd, bd):
    mp, d = xs.shape
    nb = mp // MOE_BLOCK_ROWS
    grid_spec = pltpu.PrefetchScalarGridSpec(
        num_scalar_prefetch=2,
        grid=(nb,),
        in_specs=[
            pl.BlockSpec((MOE_BLOCK_ROWS, d), lambda i, be, nu: (i, 0)),
            pl.BlockSpec((1, 1, d, 2 * D_FF), lambda i, be, nu: (layer, be[i], 0, 0)),
            pl.BlockSpec((1, 1, 1, 2 * D_FF), lambda i, be, nu: (layer, be[i], 0, 0)),
            pl.BlockSpec((1, 1, D_FF, d), lambda i, be, nu: (layer, be[i], 0, 0)),
            pl.BlockSpec((1, 1, 1, d), lambda i, be, nu: (layer, be[i], 0, 0)),
        ],
        out_specs=pl.BlockSpec((MOE_BLOCK_ROWS, d), lambda i, be, nu: (i, 0)),
    )
    return pl.pallas_call(
        _moe_kernel,
        out_shape=jax.ShapeDtypeStruct((mp, d), F32),
        grid_spec=grid_spec,
        compiler_params=_cparams("arbitrary"),
        name="moe_experts",
    )(block_e, n_used, xs, wu, bu, wd, bd)


def _chunk_pos(bb, c, width):
    return lax.broadcasted_iota(jnp.int32, (bb, c, width), 1).reshape(bb * c, width)


def _seg_scan(x, pos, c, op, ident):
    s = 1
    while s < c:
        x = op(x, jnp.where(pos >= s, pltpu.roll(x, s, 0), ident))
        s *= 2
    return x


def _chunk_last(x, pos, bb, c):
    w = x.shape[1]
    x3 = jnp.where(pos == c - 1, x, 0.0).reshape(bb, c, w)
    return jnp.broadcast_to(jnp.sum(x3, axis=1, keepdims=True), (bb, c, w)).reshape(bb * c, w)


def _causal_conv(x3, ext_ref, prev_ref, cw_ref, c):
    ext_ref[:, 0:SUBLANES, :] = prev_ref[...]
    ext_ref[:, SUBLANES:SUBLANES + c, :] = x3
    y = None
    for j in range(CONV_W):
        lo = SUBLANES - (CONV_W - 1) + j
        tap = ext_ref[:, lo:lo + c, :] * cw_ref[j:j + 1, :]
        y = tap if y is None else y + tap
    prev_ref[...] = ext_ref[:, c:c + SUBLANES, :]
    return y


def _head_masks(width=MIX_W):
    lane = lax.broadcasted_iota(jnp.int32, (1, width), 1)
    return [(lane >= h * HEAD_DIM) & (lane < (h + 1) * HEAD_DIM) for h in range(N_HEADS)]


def _lane_masks():
    lane = lax.broadcasted_iota(jnp.int32, (1, LANES), 1)
    return [lane == h for h in range(N_HEADS)]


def _stack(xg, masks, g, c, cs):
    pieces = []
    for b in range(g):
        xb = xg[b * c:(b + 1) * c]
        for m in masks:
            pieces.append(jnp.where(m, xb, 0.0))
            if cs > c:
                pieces.append(jnp.zeros((cs - c, xg.shape[1]), xg.dtype))
    return jnp.concatenate(pieces, axis=0)


def _unstack(y, g, c, cs):
    outs = []
    for b in range(g):
        acc = None
        for h in range(N_HEADS):
            r0 = (b * N_HEADS + h) * cs
            piece = y[r0:r0 + c]
            acc = piece if acc is None else acc + piece
        outs.append(acc)
    return outs[0] if g == 1 else jnp.concatenate(outs, axis=0)


def _bd_masks(cs):
    r = lax.broadcasted_iota(jnp.int32, (BD, BD), 0)
    q = lax.broadcasted_iota(jnp.int32, (BD, BD), 1)
    shift = cs.bit_length() - 1
    same = (r >> shift) == (q >> shift)
    tr = r & (cs - 1)
    tq = q & (cs - 1)
    return same & (tq <= tr), same & (tq < tr)


def _head_block_mask():
    r = lax.broadcasted_iota(jnp.int32, (MIX_W, MIX_W), 0)
    q = lax.broadcasted_iota(jnp.int32, (MIX_W, MIX_W), 1)
    return (r // HEAD_DIM) == (q // HEAD_DIM)


def _head_sum(x, ones_bd):
    return _mm_sel(x, ones_bd)


def _to_hb(cols, expand):
    return _mm_sel(cols, expand, terms=3)


def _stack_cols(cols_g, g, c, cs):
    st = _stack(cols_g, _lane_masks(), g, c, cs)
    col = jnp.sum(st, axis=1, keepdims=True)
    ones = jnp.ones((SUBLANES, LANES), F32)
    row = _mm(ones, st, _NT, precision=HIGHEST)[0:1, :]
    return col, row


def _group_rows(ref, gi, g, c):
    w = ref.shape[2]
    if g == 1:
        return ref[gi]
    return ref[pl.ds(gi * g, g)].reshape(g * c, w)


def _seq_specs(bb, c):
    x_spec = pl.BlockSpec((bb, c, D_MODEL), lambda i, k: (i, k, 0))
    y_spec = pl.BlockSpec((bb, c, MIX_W), lambda i, k: (i, k, 0))
    return x_spec, y_spec


def _state_spec(bb, *tail):
    nt = len(tail)
    return pl.BlockSpec((bb,) + tuple(tail), lambda i, k: (i,) + (0,) * nt)


def _lru_kernel(x_ref, w_ref, cw_ref, cb_ref, wra_ref, bra_ref, wix_ref, bix_ref, lam_ref, conv0_ref, h0_ref,
                y_ref, conv1_ref, h1_ref, ext_ref, prev_ref, hst_ref):
    k = pl.program_id(1)
    bb, c, _ = x_ref.shape
    n = bb * c

    @pl.when(k == 0)
    def _():
        prev_ref[...] = conv0_ref[...]
        hst_ref[...] = h0_ref[...]

    xb = x_ref[...].reshape(n, D_MODEL).astype(BF16)
    proj = _mm(xb, w_ref[...])
    a_gate = proj[:, MIX_W:]
    xa = _causal_conv(proj[:, :MIX_W].reshape(bb, c, MIX_W), ext_ref, prev_ref, cw_ref, c) + cb_ref[...]
    xa = xa.reshape(n, MIX_W)
    xab = xa.astype(BF16)
    r = _sigmoid(_mm(xab, wra_ref[...]) + bra_ref[...])
    i = _sigmoid(_mm(xab, wix_ref[...]) + bix_ref[...])
    log_a = -LRU_C * r * _softplus(-lam_ref[...])
    a_cum = jnp.exp(log_a)
    b_cum = jnp.sqrt(-_expm1(2.0 * log_a)) * (i * xa)
    pos = _chunk_pos(bb, c, MIX_W)
    s = 1
    while s < c:
        keep = pos >= s
        a_sh = pltpu.roll(a_cum, s, 0)
        b_sh = pltpu.roll(b_cum, s, 0)
        b_cum = jnp.where(keep, a_cum * b_sh + b_cum, b_cum)
        a_cum = jnp.where(keep, a_cum * a_sh, a_cum)
        s *= 2
    h0 = jnp.broadcast_to(hst_ref[:, SUBLANES - 1:SUBLANES, :], (bb, c, MIX_W)).reshape(n, MIX_W)
    h = a_cum * h0 + b_cum
    y_ref[...] = (h * _gelu_tanh(a_gate)).reshape(bb, c, MIX_W)
    hst_ref[...] = h.reshape(bb, c, MIX_W)[:, c - SUBLANES:c, :]

    @pl.when(k == pl.num_programs(1) - 1)
    def _():
        conv1_ref[...] = prev_ref[...]
        h1_ref[...] = hst_ref[...]


def _lru_call(x3, w, cw, cb, wra, bra, wix, bix, lam, conv0, h0, bb, c):
    b, l, _ = x3.shape
    x_spec, y_spec = _seq_specs(bb, c)
    st = _state_spec(bb, SUBLANES, MIX_W)
    consts = (w, cw, cb, wra, bra, wix, bix, lam)
    return pl.pallas_call(
        _lru_kernel,
        out_shape=(jax.ShapeDtypeStruct((b, l, MIX_W), F32), jax.ShapeDtypeStruct((b, SUBLANES, MIX_W), F32),
                   jax.ShapeDtypeStruct((b, SUBLANES, MIX_W), F32)),
        grid=(b // bb, l // c),
        in_specs=[x_spec] + [_full(a) for a in consts] + [st, st],
        out_specs=(y_spec, st, st),
        scratch_shapes=[pltpu.VMEM((bb, c + SUBLANES, MIX_W), F32), pltpu.VMEM((bb, SUBLANES, MIX_W), F32),
                        pltpu.VMEM((bb, SUBLANES, MIX_W), F32)],
        compiler_params=_cparams("parallel", "arbitrary"),
        name="mixer_rglru",
    )(x3, *consts, conv0, h0)


def _hgrn_kernel(x_ref, w_ref, wlo_ref, loglb_ref, log1mlb_ref, onemlb_ref, ng_ref, ones_ref, st0_ref,
                 y_ref, st1_ref, st_ref, q_s, k_s, i_s, b_s, qe_s, kd_s, el_s, o_s):
    kk = pl.program_id(1)
    bb, c, _ = x_ref.shape
    n = bb * c
    nblk = c // SUBLANES

    @pl.when(kk == 0)
    def _():
        st_ref[...] = st0_ref[...]

    xb, xlo = _split_bf16(x_ref[...].reshape(n, D_MODEL))
    proj = _mm(xb, w_ref[...])
    q = proj[:, 0:MIX_W] * HEAD_DIM ** -0.5
    z = proj[:, MIX_W:2 * MIX_W] + _mm(xlo, w_ref[:, MIX_W:2 * MIX_W]) + _mm(xb, wlo_ref[...])
    iv = proj[:, 2 * MIX_W:3 * MIX_W]
    cg = proj[:, 3 * MIX_W:4 * MIX_W]
    lo = loglb_ref[...]
    hi = log1mlb_ref[...] + _log_sigmoid(z)
    log_f = jnp.maximum(lo, hi) + jnp.log1p(jnp.exp(-jnp.abs(lo - hi)))
    kc = onemlb_ref[...] * _sigmoid(-z)
    pos = _chunk_pos(bb, c, MIX_W)
    bc = _seg_scan(log_f, pos, c, jnp.add, 0.0)
    b_last = _chunk_last(bc, pos, bb, c)
    ones_bd = ones_ref[...]

    sub = pos & (SUBLANES - 1)
    o_band = None
    for d in range(SUBLANES):
        kr = pltpu.roll(kc, d, 0) if d else kc
        br = pltpu.roll(bc, d, 0) if d else bc
        ir = pltpu.roll(iv, d, 0) if d else iv
        wd = jnp.where(sub >= d, q * kr * jnp.exp(jnp.minimum(bc - br, 0.0)), 0.0)
        term = _head_sum(wd, ones_bd) * ir
        o_band = term if o_band is None else o_band + term

    q_s[...] = q.reshape(bb, c, MIX_W)
    k_s[...] = kc.reshape(bb, c, MIX_W)
    i_s[...] = iv.reshape(bb, c, MIX_W)
    b_s[...] = bc.reshape(bb, c, MIX_W)
    qe_s[...] = (q * jnp.exp(bc)).reshape(bb, c, MIX_W)
    kd_s[...] = (kc * jnp.exp(b_last - bc)).reshape(bb, c, MIX_W)
    el_s[...] = jnp.exp(b_last).reshape(bb, c, MIX_W)

    hmasks = _head_masks()
    blockmask = _head_block_mask()
    pad_rows = LANES - c
    rowid = lax.broadcasted_iota(jnp.int32, (LANES, MIX_W), 0)
    zpad = jnp.zeros((pad_rows, MIX_W), F32)

    def per_b(b, carry):
        st = st_ref[b]
        o = _bmm(qe_s[b], st, _NT)
        if nblk > 1:
            kp = jnp.concatenate([k_s[b], zpad], axis=0)
            bp = jnp.concatenate([b_s[b], zpad], axis=0)
            ip = jnp.concatenate([i_s[b], zpad], axis=0).astype(BF16)
            pieces = [jnp.zeros((SUBLANES, MIX_W), F32)]
            for blk in range(1, nblk):
                r0 = blk * SUBLANES
                ref_b = b_s[b, r0 - 1:r0, :]
                qi = q_s[b, r0:r0 + SUBLANES, :] * jnp.exp(jnp.minimum(b_s[b, r0:r0 + SUBLANES, :] - ref_b, 0.0))
                ki = jnp.where(rowid < r0, kp * jnp.exp(jnp.minimum(ref_b - bp, 0.0)), 0.0)
                qst = jnp.concatenate([jnp.where(m, qi, 0.0) for m in hmasks], axis=0)
                att = _bmm(qst, ki, _NT)
                res = _mm(att.astype(BF16), ip)
                acc = None
                for h in range(N_HEADS):
                    part = jnp.where(hmasks[h], res[h * SUBLANES:(h + 1) * SUBLANES], 0.0)
                    acc = part if acc is None else acc + part
                pieces.append(acc)
            o = o + jnp.concatenate(pieces, axis=0)
        o_s[b] = o
        upd = _mm3(jnp.concatenate([i_s[b], zpad], axis=0), jnp.concatenate([kd_s[b], zpad], axis=0), _TN)
        st_ref[b] = st * el_s[b, 0:1, :] + jnp.where(blockmask, upd, 0.0)
        return carry

    lax.fori_loop(0, bb, per_b, 0)

    o = o_s[...].reshape(n, MIX_W) + o_band
    ms = _head_sum(o * o, ones_bd) * (1.0 / HEAD_DIM)
    y = o * lax.rsqrt(ms + NORM_EPS) * ng_ref[...] * _silu(cg)
    y_ref[...] = y.reshape(bb, c, MIX_W)

    @pl.when(kk == pl.num_programs(1) - 1)
    def _():
        st1_ref[...] = st_ref[...]


def _hgrn_call(x3, w, wlo, loglb, log1mlb, onemlb, ng, ones_bd, st0, bb, c):
    b, l, _ = x3.shape
    x_spec, y_spec = _seq_specs(bb, c)
    st = _state_spec(bb, MIX_W, MIX_W)
    consts = (w, wlo, loglb, log1mlb, onemlb, ng, ones_bd)
    rows = pltpu.VMEM((bb, c, MIX_W), F32)
    return pl.pallas_call(
        _hgrn_kernel,
        out_shape=(jax.ShapeDtypeStruct((b, l, MIX_W), F32), jax.ShapeDtypeStruct((b, MIX_W, MIX_W), F32)),
        grid=(b // bb, l // c),
        in_specs=[x_spec] + [_full(a) for a in consts] + [st],
        out_specs=(y_spec, st),
        scratch_shapes=[pltpu.VMEM((bb, MIX_W, MIX_W), F32)] + [rows] * 8,
        compiler_params=_cparams("parallel", "arbitrary"),
        name="mixer_hgrn2",
    )(x3, *consts, st0)


def _mlstm_kernel(g, cs, x_ref, w_ref, wlo_ref, cw_ref, cb_ref, wq_ref, wk_ref, bi_ref, bf_ref, ng_ref, ones_ref, exp_ref,
                  conv0_ref, c0_ref, n0_ref, m0_ref,
                  y_ref, conv1_ref, c1_ref, n1_ref, m1_ref,
                  ext_ref, prev_ref, cst_ref, nst_ref, mst_ref, q_s, k_s, v_s, kw_s, dec_s, a_s, g_s, num_s, den_s,
                  qc_s):
    kk = pl.program_id(1)
    bb, c, _ = x_ref.shape
    n = bb * c
    ngroups = bb // g

    @pl.when(kk == 0)
    def _():
        prev_ref[...] = conv0_ref[...]
        cst_ref[...] = c0_ref[...]
        nst_ref[...] = n0_ref[...]
        mst_ref[...] = m0_ref[...]

    xb, xlo = _split_bf16(x_ref[...].reshape(n, D_MODEL))
    proj = _mm(xb, w_ref[...])
    gates = proj[:, 3 * MIX_W:] + _mm(xlo, w_ref[:, 3 * MIX_W:]) + _mm(xb, wlo_ref[...])
    xm = _causal_conv(proj[:, 0:MIX_W].reshape(bb, c, MIX_W), ext_ref, prev_ref, cw_ref, c) + cb_ref[...]
    xm = _silu(xm).reshape(n, MIX_W).astype(BF16)
    q = _mm(xm, wq_ref[...])
    k = _mm(xm, wk_ref[...]) * HEAD_DIM ** -0.5
    v = proj[:, MIX_W:2 * MIX_W]
    d_o = proj[:, 2 * MIX_W:3 * MIX_W]
    log_i = gates[:, 0:LANES] + bi_ref[...]
    log_f = _log_sigmoid(gates[:, LANES:2 * LANES] + bf_ref[...])
    pos = _chunk_pos(bb, c, LANES)
    bc = _seg_scan(log_f, pos, c, jnp.add, 0.0)
    gg = log_i - bc
    cm = _seg_scan(gg, pos, c, jnp.maximum, -jnp.inf)
    m0 = jnp.broadcast_to(mst_ref[:, SUBLANES - 1:SUBLANES, :], (bb, c, LANES)).reshape(n, LANES)
    m_t = jnp.maximum(m0 + bc, bc + cm)
    b_last = _chunk_last(bc, pos, bb, c)
    m_last = _chunk_last(m_t, pos, bb, c)
    expand = exp_ref[...]
    ones_bd = ones_ref[...]
    inter = _to_hb(jnp.exp(m0 + bc - m_t), expand)
    e_negm = _to_hb(jnp.exp(-m_t), expand)
    wl = _to_hb(jnp.exp(gg + b_last - m_last), expand)
    nrows = jnp.broadcast_to(nst_ref[:, 0:1, :], (bb, c, MIX_W)).reshape(n, MIX_W)
    qn = _head_sum(q * nrows, ones_bd)

    q_s[...] = q.reshape(bb, c, MIX_W)
    k_s[...] = k.reshape(bb, c, MIX_W)
    v_s[...] = v.reshape(bb, c, MIX_W)
    kw_s[...] = (k * wl).reshape(bb, c, MIX_W)
    dec_s[...] = _to_hb(jnp.exp(m0 + b_last - m_last), expand).reshape(bb, c, MIX_W)
    a_s[...] = (bc - m_t).reshape(bb, c, LANES)
    g_s[...] = gg.reshape(bb, c, LANES)

    hmasks = _head_masks()
    causal, _ = _bd_masks(cs)
    rowb = lax.broadcasted_iota(jnp.int32, (BD, MIX_W), 0) // (N_HEADS * cs)
    sel = _stack(jnp.ones((g * c, MIX_W), F32), hmasks, g, c, cs)

    def per_group(gi, carry):
        qg = _group_rows(q_s, gi, g, c)
        qst = _stack(qg, hmasks, g, c, cs).astype(BF16)
        kst = _stack(_group_rows(k_s, gi, g, c), hmasks, g, c, cs).astype(BF16)
        vst = _stack(_group_rows(v_s, gi, g, c), hmasks, g, c, cs).astype(BF16)
        kwst = _stack(_group_rows(kw_s, gi, g, c), hmasks, g, c, cs)
        acol, _ = _stack_cols(_group_rows(a_s, gi, g, c), g, c, cs)
        _, grow = _stack_cols(_group_rows(g_s, gi, g, c), g, c, cs)
        wmat = jnp.where(causal, jnp.exp(jnp.minimum(acol + grow, 0.0)), 0.0)
        qkw = _mm(qst, kst, _NT) * wmat
        num = _unstack(_mm(qkw.astype(BF16), vst), g, c, cs)
        den = _unstack(jnp.sum(qkw, axis=1, keepdims=True) * sel, g, c, cs)
        for bl in range(g):
            b = gi * g + bl
            cmat = cst_ref[b]
            qc_s[b] = _bmm(qg[bl * c:(bl + 1) * c], cmat)
            num_s[b] = num[bl * c:(bl + 1) * c]
            den_s[b] = den[bl * c:(bl + 1) * c]
            kwb = kwst if g == 1 else jnp.where(rowb == bl, kwst, 0.0)
            dec_row = dec_s[b, 0:1, :]
            cst_ref[b] = cmat * dec_row + _mm(kwb.astype(BF16), vst, _TN)
            ksum = jnp.sum(kw_s[b], axis=0, keepdims=True)
            nst_ref[b] = nst_ref[b] * dec_row + jnp.broadcast_to(ksum, (SUBLANES, MIX_W))
        return carry

    lax.fori_loop(0, ngroups, per_group, 0)

    num = num_s[...].reshape(n, MIX_W) + inter * qc_s[...].reshape(n, MIX_W)
    den = den_s[...].reshape(n, MIX_W) + inter * qn
    h = num / jnp.maximum(jnp.abs(den), e_negm)
    mu = _head_sum(h, ones_bd) * (1.0 / HEAD_DIM)
    hc = h - mu
    var = _head_sum(hc * hc, ones_bd) * (1.0 / HEAD_DIM)
    y = hc * lax.rsqrt(var + NORM_EPS) * ng_ref[...] * _sigmoid(d_o)
    y_ref[...] = y.reshape(bb, c, MIX_W)
    mst_ref[...] = m_t.reshape(bb, c, LANES)[:, c - SUBLANES:c, :]

    @pl.when(kk == pl.num_programs(1) - 1)
    def _():
        conv1_ref[...] = prev_ref[...]
        c1_ref[...] = cst_ref[...]
        n1_ref[...] = nst_ref[...]
        m1_ref[...] = mst_ref[...]


def _mlstm_call(x3, w, wlo, cw, cb, wq, wk, bi, bf, ng, ones_bd, expand, conv0, c0, n0, m0, bb, c, g, cs):
    b, l, _ = x3.shape
    x_spec, y_spec = _seq_specs(bb, c)
    st_conv = _state_spec(bb, SUBLANES, MIX_W)
    st_c = _state_spec(bb, MIX_W, MIX_W)
    st_n = _state_spec(bb, SUBLANES, MIX_W)
    st_m = _state_spec(bb, SUBLANES, LANES)
    consts = (w, wlo, cw, cb, wq, wk, bi, bf, ng, ones_bd, expand)
    rows = pltpu.VMEM((bb, c, MIX_W), F32)
    cols = pltpu.VMEM((bb, c, LANES), F32)
    return pl.pallas_call(
        functools.partial(_mlstm_kernel, g, cs),
        out_shape=(jax.ShapeDtypeStruct((b, l, MIX_W), F32), jax.ShapeDtypeStruct((b, SUBLANES, MIX_W), F32),
                   jax.ShapeDtypeStruct((b, MIX_W, MIX_W), F32), jax.ShapeDtypeStruct((b, SUBLANES, MIX_W), F32),
                   jax.ShapeDtypeStruct((b, SUBLANES, LANES), F32)),
        grid=(b // bb, l // c),
        in_specs=[x_spec] + [_full(a) for a in consts] + [st_conv, st_c, st_n, st_m],
        out_specs=(y_spec, st_conv, st_c, st_n, st_m),
        scratch_shapes=[pltpu.VMEM((bb, c + SUBLANES, MIX_W), F32), pltpu.VMEM((bb, SUBLANES, MIX_W), F32),
                        pltpu.VMEM((bb, MIX_W, MIX_W), F32), pltpu.VMEM((bb, SUBLANES, MIX_W), F32),
                        pltpu.VMEM((bb, SUBLANES, LANES), F32),
                        rows, rows, rows, rows, rows, cols, cols, rows, rows, rows],
        compiler_params=_cparams("parallel", "arbitrary"),
        name="mixer_mlstm",
    )(x3, *consts, conv0, c0, n0, m0)


def _gdn_kernel(g, cs, x_ref, w_ref, wlo_ref, cw_ref, alog_ref, dtb_ref, ng_ref, ones_ref, exp_ref, conv0_ref, s0_ref,
                y_ref, conv1_ref, s1_ref,
                ext_ref, prev_ref, sst_ref, q_s, k_s, vb_s, kb_s, qe_s, kd_s, el_s, gc_s, be_s, o_s):
    kk = pl.program_id(1)
    bb, c, _ = x_ref.shape
    n = bb * c
    ngroups = bb // g
    qkv_w = 3 * MIX_W

    @pl.when(kk == 0)
    def _():
        prev_ref[...] = conv0_ref[...]
        sst_ref[...] = s0_ref[...]

    xb, xlo = _split_bf16(x_ref[...].reshape(n, D_MODEL))
    proj = _mm(xb, w_ref[...])
    gates = proj[:, 4 * MIX_W:] + _mm(xlo, w_ref[:, 4 * MIX_W:]) + _mm(xb, wlo_ref[...])
    qkv = _causal_conv(proj[:, 0:qkv_w].reshape(bb, c, qkv_w), ext_ref, prev_ref, cw_ref, c)
    qkv = _silu(qkv).reshape(n, qkv_w)
    ones_bd = ones_ref[...]
    expand = exp_ref[...]
    q = qkv[:, 0:MIX_W]
    k = qkv[:, MIX_W:2 * MIX_W]
    v = qkv[:, 2 * MIX_W:3 * MIX_W]
    q = q * lax.rsqrt(_head_sum(q * q, ones_bd) + NORM_EPS) * HEAD_DIM ** -0.5
    k = k * lax.rsqrt(_head_sum(k * k, ones_bd) + NORM_EPS)
    z = proj[:, qkv_w:qkv_w + MIX_W]
    a_in = gates[:, 0:LANES]
    b_in = gates[:, LANES:2 * LANES]
    gdec = -jnp.exp(alog_ref[...]) * _softplus(a_in + dtb_ref[...])
    beta = _sigmoid(b_in)
    pos = _chunk_pos(bb, c, LANES)
    gcum = _seg_scan(gdec, pos, c, jnp.add, 0.0)
    g_last = _chunk_last(gcum, pos, bb, c)
    beta_hb = _to_hb(beta, expand)
    eg_hb = _to_hb(jnp.exp(gcum), expand)

    q_s[...] = q.reshape(bb, c, MIX_W)
    k_s[...] = k.reshape(bb, c, MIX_W)
    vb_s[...] = (v * beta_hb).reshape(bb, c, MIX_W)
    kb_s[...] = (k * beta_hb * eg_hb).reshape(bb, c, MIX_W)
    qe_s[...] = (q * eg_hb).reshape(bb, c, MIX_W)
    kd_s[...] = (k * _to_hb(jnp.exp(g_last - gcum), expand)).reshape(bb, c, MIX_W)
    el_s[...] = _to_hb(jnp.exp(g_last), expand).reshape(bb, c, MIX_W)
    gc_s[...] = gcum.reshape(bb, c, LANES)
    be_s[...] = beta.reshape(bb, c, LANES)

    hmasks = _head_masks()
    causal, strict = _bd_masks(cs)
    rowb = lax.broadcasted_iota(jnp.int32, (BD, MIX_W), 0) // (N_HEADS * cs)
    rows_b = N_HEADS * cs

    def per_group(gi, carry):
        qg = _group_rows(qe_s, gi, g, c)
        qst = _stack(_group_rows(q_s, gi, g, c), hmasks, g, c, cs).astype(BF16)
        kst = _stack(_group_rows(k_s, gi, g, c), hmasks, g, c, cs).astype(BF16)
        kdst = _stack(_group_rows(kd_s, gi, g, c), hmasks, g, c, cs)
        rhs = jnp.concatenate([_stack(_group_rows(vb_s, gi, g, c), hmasks, g, c, cs),
                               _stack(_group_rows(kb_s, gi, g, c), hmasks, g, c, cs)], axis=1)
        gcol, grow = _stack_cols(_group_rows(gc_s, gi, g, c), g, c, cs)
        bcol, _ = _stack_cols(_group_rows(be_s, gi, g, c), g, c, cs)
        decay = jnp.exp(jnp.minimum(gcol - grow, 0.0))
        amat = jnp.where(strict, bcol * _mm(kst, kst, _NT) * decay, 0.0)
        x = rhs - _bmm(amat, rhs)
        p = amat
        span = 2
        while span < cs:
            p = _bmm(p, p)
            x = x + _bmm(p, x)
            span *= 2
        w_v = x[:, 0:MIX_W]
        w_k = x[:, MIX_W:2 * MIX_W]
        us = []
        for bl in range(g):
            b = gi * g + bl
            r0 = bl * rows_b
            us.append(w_v[r0:r0 + rows_b] - _bmm(w_k[r0:r0 + rows_b], sst_ref[b]))
        u = us[0] if g == 1 else jnp.concatenate(us, axis=0)
        ub = u.astype(BF16)
        qk = jnp.where(causal, _mm(qst, kst, _NT) * decay, 0.0)
        o_intra = _unstack(_mm(qk.astype(BF16), ub), g, c, cs)
        for bl in range(g):
            b = gi * g + bl
            smat = sst_ref[b]
            o_s[b] = o_intra[bl * c:(bl + 1) * c] + _bmm(qg[bl * c:(bl + 1) * c], smat)
            kdb = kdst if g == 1 else jnp.where(rowb == bl, kdst, 0.0)
            sst_ref[b] = smat * el_s[b, 0:1, :] + _mm(kdb.astype(BF16), ub, _TN)
        return carry

    lax.fori_loop(0, ngroups, per_group, 0)

    o = o_s[...].reshape(n, MIX_W)
    ms = _head_sum(o * o, ones_bd) * (1.0 / HEAD_DIM)
    y = o * lax.rsqrt(ms + NORM_EPS) * ng_ref[...] * _silu(z)
    y_ref[...] = y.reshape(bb, c, MIX_W)

    @pl.when(kk == pl.num_programs(1) - 1)
    def _():
        conv1_ref[...] = prev_ref[...]
        s1_ref[...] = sst_ref[...]


def _gdn_call(x3, w, wlo, cw, alog, dtb, ng, ones_bd, expand, conv0, s0, bb, c, g, cs):
    b, l, _ = x3.shape
    x_spec, y_spec = _seq_specs(bb, c)
    st_conv = _state_spec(bb, SUBLANES, 3 * MIX_W)
    st_s = _state_spec(bb, MIX_W, MIX_W)
    consts = (w, wlo, cw, alog, dtb, ng, ones_bd, expand)
    rows = pltpu.VMEM((bb, c, MIX_W), F32)
    cols = pltpu.VMEM((bb, c, LANES), F32)
    return pl.pallas_call(
        functools.partial(_gdn_kernel, g, cs),
        out_shape=(jax.ShapeDtypeStruct((b, l, MIX_W), F32), jax.ShapeDtypeStruct((b, SUBLANES, 3 * MIX_W), F32),
                   jax.ShapeDtypeStruct((b, MIX_W, MIX_W), F32)),
        grid=(b // bb, l // c),
        in_specs=[x_spec] + [_full(a) for a in consts] + [st_conv, st_s],
        out_specs=(y_spec, st_conv, st_s),
        scratch_shapes=[pltpu.VMEM((bb, c + SUBLANES, 3 * MIX_W), F32), pltpu.VMEM((bb, SUBLANES, 3 * MIX_W), F32),
                        pltpu.VMEM((bb, MIX_W, MIX_W), F32),
                        rows, rows, rows, rows, rows, rows, rows, cols, cols, rows],
        compiler_params=_cparams("parallel", "arbitrary"),
        name="mixer_gdn",
    )(x3, *consts, conv0, s0)


def _pad_cols(a, width):
    return jnp.pad(a, ((0, 0), (0, width - a.shape[1])))


def _row(a, width=None):
    a = a.reshape(1, -1).astype(F32)
    return a if width is None else _pad_cols(a, width)


def _block_diag(w4):
    h, d, _ = w4.shape
    eye = jnp.eye(h, dtype=w4.dtype)
    return jnp.einsum("hij,hg->higj", w4, eye).reshape(h * d, h * d)


def _bd_state(s):
    b, h, d, _ = s.shape
    eye = jnp.eye(h, dtype=s.dtype)
    return jnp.einsum("bhij,hg->bhigj", s, eye).reshape(b, h * d, h * d)


def _bd_blocks(s):
    b = s.shape[0]
    s5 = s.reshape(b, N_HEADS, HEAD_DIM, N_HEADS, HEAD_DIM)
    return jnp.stack([s5[:, h, :, h, :] for h in range(N_HEADS)], axis=1)


def _tail8(buf):
    return jnp.pad(buf, ((0, 0), (SUBLANES - (CONV_W - 1), 0), (0, 0)))


def _bcast8(a):
    return jnp.broadcast_to(a[:, None, :], (a.shape[0], SUBLANES, a.shape[1]))


def _layer_params(P, lb, l):
    w_f32 = P["w_in"][l].astype(F32)
    w_in = w_f32.astype(BF16)
    w_lo = (w_f32 - w_in.astype(F32)).astype(BF16)

    def gate_tiles(w, lo, hi):
        return [_pad_cols(w[:, lo:lo + N_HEADS], LANES), _pad_cols(w[:, lo + N_HEADS:hi], LANES)]

    p = {}
    p["w_a"] = w_in[:, _A0:_B0]
    p["w_b"] = jnp.concatenate([w_in[:, _B0:_B0 + 4 * MIX_W]] + gate_tiles(w_in, _B0 + 4 * MIX_W, _C0), axis=1)
    p["w_b_lo"] = jnp.concatenate(gate_tiles(w_lo, _B0 + 4 * MIX_W, _C0), axis=1)
    p["w_c"] = w_in[:, _C0:_D0]
    p["w_c_lo"] = w_lo[:, _C0 + MIX_W:_C0 + 2 * MIX_W]
    p["w_d"] = jnp.concatenate([w_in[:, _D0:_D0 + 3 * MIX_W]] + gate_tiles(w_in, _D0 + 3 * MIX_W, _G0), axis=1)
    p["w_d_lo"] = jnp.concatenate(gate_tiles(w_lo, _D0 + 3 * MIX_W, _G0), axis=1)
    p["w_g"] = w_in[:, _G0:]
    p["lru_cw"] = P["w_lru_conv"][l]
    p["lru_cb"] = _row(P["b_lru_conv"][l])
    p["lru_wra"] = _block_diag(P["w_lru_ra"][l]).astype(BF16)
    p["lru_bra"] = _row(P["b_lru_ra"][l])
    p["lru_wix"] = _block_diag(P["w_lru_ix"][l]).astype(BF16)
    p["lru_bix"] = _row(P["b_lru_ix"][l])
    p["lru_lam"] = _row(P["lru_lambda"][l])
    p["gdn_cw"] = P["w_gdn_conv"][l]
    p["gdn_alog"] = _row(P["gdn_a_log"][l], LANES)
    p["gdn_dtb"] = _row(P["gdn_dt_bias"][l], LANES)
    p["gdn_ng"] = _row(jnp.tile(P["gdn_norm_g"][l], N_HEADS))
    p["hg_loglb"] = _row(jnp.log(lb[l]))
    p["hg_log1mlb"] = _row(jnp.log1p(-lb[l]))
    p["hg_onemlb"] = _row(1.0 - lb[l])
    p["hg_ng"] = _row(jnp.tile(P["hg_norm_g"][l], N_HEADS))
    p["ml_cw"] = P["w_ml_conv"][l]
    p["ml_cb"] = _row(P["b_ml_conv"][l])
    p["ml_wq"] = _block_diag(P["w_ml_q"][l]).astype(BF16)
    p["ml_wk"] = _block_diag(P["w_ml_k"][l]).astype(BF16)
    p["ml_bi"] = _row(P["b_ml_i"][l], LANES)
    p["ml_bf"] = _row(P["b_ml_f"][l], LANES)
    p["ml_ng"] = _row(P["ml_norm_g"][l])
    p["w_br"] = P["w_branch"][l].astype(BF16)
    p["w_out"] = P["w_out"][l].astype(BF16)
    p["ln1_g"] = _row(P["ln1_g"][l])
    p["ln1_b"] = _row(P["ln1_b"][l])
    p["ln2_g"] = _row(P["ln2_g"][l])
    p["ln2_b"] = _row(P["ln2_b"][l])
    p["w_r"] = _pad_cols(P["w_router"][l], LANES).astype(BF16)
    p["b_r"] = jnp.concatenate([P["b_router"][l].astype(F32), jnp.full((LANES - N_EXPERTS,), -1e30, F32)]).reshape(1, LANES)
    return p


def _mixers(x3, p, st, consts, bb, c, g, cs):
    ones_bd, expand = consts
    lru_conv, lru_h, gdn_conv, gdn_s, hg_st, ml_conv, ml_c, ml_n, ml_m = st
    y_a, lru_conv1, lru_h1 = _lru_call(x3, p["w_a"], p["lru_cw"], p["lru_cb"], p["lru_wra"], p["lru_bra"],
                                       p["lru_wix"], p["lru_bix"], p["lru_lam"], lru_conv, lru_h, bb, c)
    y_b, gdn_conv1, gdn_s1 = _gdn_call(x3, p["w_b"], p["w_b_lo"], p["gdn_cw"], p["gdn_alog"], p["gdn_dtb"], p["gdn_ng"],
                                       ones_bd, expand, gdn_conv, gdn_s, bb, c, g, cs)
    y_c, hg_st1 = _hgrn_call(x3, p["w_c"], p["w_c_lo"], p["hg_loglb"], p["hg_log1mlb"], p["hg_onemlb"], p["hg_ng"], ones_bd,
                             hg_st, bb, c)
    y_d, ml_conv1, ml_c1, ml_n1, ml_m1 = _mlstm_call(x3, p["w_d"], p["w_d_lo"], p["ml_cw"], p["ml_cb"], p["ml_wq"], p["ml_wk"],
                                                     p["ml_bi"], p["ml_bf"], p["ml_ng"], ones_bd, expand,
                                                     ml_conv, ml_c, ml_n, ml_m, bb, c, g, cs)
    return (y_a, y_b, y_c, y_d), (lru_conv1, lru_h1, gdn_conv1, gdn_s1, hg_st1, ml_conv1, ml_c1, ml_n1, ml_m1)


def _states_to_kernel(st):
    lru_conv, lru_h, gdn_conv, gdn_s, hg_s, ml_conv, ml_c, ml_n, ml_m = [s.astype(F32) for s in st]
    b = lru_h.shape[0]
    return (_tail8(lru_conv), _bcast8(lru_h), _tail8(gdn_conv), _bd_state(gdn_s),
            jnp.swapaxes(_bd_state(hg_s), 1, 2), _tail8(ml_conv), _bd_state(ml_c),
            _bcast8(ml_n.reshape(b, MIX_W)), _bcast8(_pad_cols(ml_m, LANES)))


def _states_from_kernel(st):
    lru_conv, lru_h, gdn_conv, gdn_s, hg_st, ml_conv, ml_c, ml_n, ml_m = st
    b = lru_h.shape[0]
    tail = SUBLANES - (CONV_W - 1)
    return (lru_conv[:, tail:], lru_h[:, SUBLANES - 1], gdn_conv[:, tail:], _bd_blocks(gdn_s),
            _bd_blocks(jnp.swapaxes(hg_st, 1, 2)), ml_conv[:, tail:], _bd_blocks(ml_c),
            ml_n[:, 0].reshape(b, N_HEADS, HEAD_DIM), ml_m[:, SUBLANES - 1, :N_HEADS])


def _route(top_e):
    t = top_e.shape[0]
    m = t * TOP_K
    flat_e = top_e.reshape(m)
    onehot = (flat_e[:, None] == jnp.arange(N_EXPERTS, dtype=jnp.int32)[None, :]).astype(jnp.int32)
    csum = jnp.cumsum(onehot, axis=0)
    rank = jnp.sum(onehot * csum, axis=1) - 1
    counts = csum[-1]
    padded = (counts + MOE_BLOCK_ROWS - 1) // MOE_BLOCK_ROWS * MOE_BLOCK_ROWS
    pad_ends = jnp.cumsum(padded)
    dest = (pad_ends - padded)[flat_e] + rank
    n_blocks = m // MOE_BLOCK_ROWS + N_EXPERTS
    mp = n_blocks * MOE_BLOCK_ROWS
    block_start = jnp.arange(n_blocks, dtype=jnp.int32) * MOE_BLOCK_ROWS
    block_e = jnp.minimum(jnp.sum((pad_ends[None, :] <= block_start[:, None]).astype(jnp.int32), axis=1),
                          N_EXPERTS - 1)
    n_used = (pad_ends[-1] // MOE_BLOCK_ROWS).astype(jnp.int32).reshape(1)
    src_tok = jnp.zeros((mp,), jnp.int32).at[dest].set(jnp.arange(m, dtype=jnp.int32) // TOP_K,
                                                       unique_indices=True)
    return dest.reshape(t, TOP_K), src_tok, block_e, n_used


def _dest_tiles(dest):
    t = dest.shape[0]
    n = t // COMBINE_ROWS
    return jnp.swapaxes(dest.reshape(n, COMBINE_ROWS, TOP_K), 1, 2).reshape(n, 1, TOP_K * COMBINE_ROWS)


def _trunk_layer(layer, xs, states, p, moe_w, consts, cfgs):
    x1s, x1bs, tes, tgs, new_states = [], [], [], [], []
    for x3, st, (bb, c, g, cs, tm) in zip(xs, states, cfgs):
        b, l, _ = x3.shape
        ys, st1 = _mixers(x3, p, st, consts, bb, c, g, cs)
        x1, x1b, te, tg = _merge_call(x3.reshape(b * l, D_MODEL), [y.reshape(b * l, MIX_W) for y in ys],
                                      p["w_g"], p["w_br"], p["w_out"], p["ln1_g"], p["ln1_b"], p["w_r"], p["b_r"], tm)
        x1s.append(x1)
        x1bs.append(x1b)
        tes.append(te[:, :TOP_K])
        tgs.append(tg)
        new_states.append(st1)
    x1b_all = jnp.concatenate(x1bs, axis=0)
    dest, src_tok, block_e, n_used = _route(jnp.concatenate(tes, axis=0))
    rows = jnp.take(x1b_all, src_tok, axis=0)
    ys = _moe_call(layer, block_e, n_used, rows, *moe_w)
    outs = []
    off = 0
    for x3, x1, tg in zip(xs, x1s, tgs):
        b, l, _ = x3.shape
        d_tiles = _dest_tiles(lax.slice_in_dim(dest, off, off + b * l, axis=0))
        off += b * l
        outs.append(_combine_ln2_call(x1, tg, d_tiles, ys, p["ln2_g"], p["ln2_b"]).reshape(b, l, D_MODEL))
    return outs, new_states


def _row_tile(t):
    for tm in (512, 384, 256, 128, 64, 32, 16):
        if t % tm == 0:
            return tm
    raise ValueError(f"no row tile for {t} rows")


def _group_cfg(b, l):
    if l % PROMPT_CHUNK == 0:
        return (min(b, 8), PROMPT_CHUNK, 1, BD // N_HEADS, _row_tile(b * l))
    assert l == SUBLANES, "sequence length must be a multiple of the prompt chunk or one sublane tile"
    g = BD // (N_HEADS * SUBLANES)
    bb = min(b, 16)
    assert bb % g == 0 and b % bb == 0
    return (bb, SUBLANES, g, SUBLANES, _row_tile(b * l))


def _zero_states(b):
    return (jnp.zeros((b, CONV_W - 1, MIX_W), F32), jnp.zeros((b, MIX_W), F32),
            jnp.zeros((b, CONV_W - 1, 3 * MIX_W), F32), jnp.zeros((b, N_HEADS, HEAD_DIM, HEAD_DIM), F32),
            jnp.zeros((b, N_HEADS, HEAD_DIM, HEAD_DIM), F32), jnp.zeros((b, CONV_W - 1, MIX_W), F32),
            jnp.zeros((b, N_HEADS, HEAD_DIM, HEAD_DIM), F32), jnp.zeros((b, N_HEADS, HEAD_DIM), F32),
            jnp.zeros((b, N_HEADS), F32))


def kernel(x_prompt, x_sample, state_lru_conv, state_lru_h, state_gdn_conv, state_gdn, state_hgrn, state_mlstm_conv, state_mlstm_c, state_mlstm_n, state_mlstm_m, meta_tokens, ln_emb_g, ln_emb_b, hg_lb_logits, w_in, w_lru_conv, b_lru_conv, w_lru_ra, b_lru_ra, w_lru_ix, b_lru_ix, lru_lambda, w_gdn_conv, gdn_a_log, gdn_dt_bias, gdn_norm_g, hg_norm_g, w_ml_conv, b_ml_conv, w_ml_q, w_ml_k, b_ml_i, b_ml_f, ml_norm_g, w_branch, w_out, ln1_g, ln1_b, ln2_g, ln2_b, w_router, b_router, w_up, b_up, w_down, b_down):
    P = dict(w_in=w_in, w_lru_conv=w_lru_conv, b_lru_conv=b_lru_conv, w_lru_ra=w_lru_ra, b_lru_ra=b_lru_ra,
             w_lru_ix=w_lru_ix, b_lru_ix=b_lru_ix, lru_lambda=lru_lambda, w_gdn_conv=w_gdn_conv,
             gdn_a_log=gdn_a_log, gdn_dt_bias=gdn_dt_bias, gdn_norm_g=gdn_norm_g, hg_norm_g=hg_norm_g,
             w_ml_conv=w_ml_conv, b_ml_conv=b_ml_conv, w_ml_q=w_ml_q, w_ml_k=w_ml_k, b_ml_i=b_ml_i,
             b_ml_f=b_ml_f, ml_norm_g=ml_norm_g, w_branch=w_branch, w_out=w_out, ln1_g=ln1_g, ln1_b=ln1_b,
             ln2_g=ln2_g, ln2_b=ln2_b, w_router=w_router, b_router=b_router, w_up=w_up, b_up=b_up,
             w_down=w_down, b_down=b_down)
    depth = w_in.shape[0]
    lb_cum = jnp.cumsum(jax.nn.softmax(hg_lb_logits.astype(F32), axis=0), axis=0)
    lb = lb_cum - lb_cum[0:1]

    lane = jnp.arange(MIX_W) // HEAD_DIM
    ones_bd = (lane[:, None] == lane[None, :]).astype(BF16)
    expand = (jnp.arange(LANES)[:, None] == lane[None, :]).astype(BF16)
    consts = (ones_bd, expand)
    moe_w = (w_up.astype(BF16), b_up.reshape(depth, N_EXPERTS, 1, 2 * D_FF).astype(F32),
             w_down.astype(BF16), b_down.reshape(depth, N_EXPERTS, 1, D_MODEL).astype(F32))

    bp, lp0, _ = x_prompt.shape
    bs, ls, _ = x_sample.shape
    meta = jnp.broadcast_to(meta_tokens.astype(F32)[None], (bp, N_META, D_MODEL))
    xp_in = jnp.concatenate([meta, x_prompt], axis=1)
    lp = lp0 + N_META
    cfgs = [_group_cfg(bp, lp), _group_cfg(bs, ls)]
    g_emb, b_emb = _row(ln_emb_g), _row(ln_emb_b)
    xp = _ln_call(xp_in.reshape(bp * lp, D_MODEL), g_emb, b_emb, cfgs[0][4]).reshape(bp, lp, D_MODEL)
    xs_ = _ln_call(x_sample.reshape(bs * ls, D_MODEL), g_emb, b_emb, cfgs[1][4]).reshape(bs, ls, D_MODEL)

    sample_states = (state_lru_conv, state_lru_h, state_gdn_conv, state_gdn, state_hgrn,
                     state_mlstm_conv, state_mlstm_c, state_mlstm_n, state_mlstm_m)
    zero_p = _zero_states(bp)
    xs = [xp, xs_]
    collected = ([], [])
    for l in range(depth):
        p = _layer_params(P, lb, l)
        states = [_states_to_kernel(zero_p), _states_to_kernel(tuple(s[l] for s in sample_states))]
        xs, new_states = _trunk_layer(l, xs, states, p, moe_w, consts, cfgs)
        for grp in range(2):
            collected[grp].append(_states_from_kernel(new_states[grp]))
    new_p = tuple(jnp.stack([layer[i] for layer in collected[0]]) for i in range(9))
    new_s = tuple(jnp.stack([layer[i] for layer in collected[1]]) for i in range(9))
    y_prompt = xs[0][:, N_META:]
    y_sample = xs[1]
    out = [y_prompt, y_sample]
    for i in range(9):
        out.append(new_p[i])
        out.append(new_s[i])
    return tuple(out)
```

```python
import functools

import jax
import jax.numpy as jnp
from jax import lax
from jax.experimental import pallas as pl
from jax.experimental.pallas import tpu as pltpu

F32 = jnp.float32
BF16 = jnp.bfloat16
HIGHEST = lax.Precision.HIGHEST

D_MODEL = 1024
DEPTH = 4
N_META = 16
N_BRANCH = 4
MIX_W = 256
N_HEADS = 4
HEAD_DIM = 64
CONV_W = 4
LRU_C = 8.0
N_EXPERTS = 32
TOP_K = 4
D_FF = 1024
SWIGLU_LIMIT = 7.0
SWIGLU_ALPHA = 1.702
LN_EPS = 1e-5
NORM_EPS = 1e-6
DEEPNORM_ALPHA = (2 * DEPTH) ** 0.25

LANES = 128
SUBLANES = 8
BD = 256
VMEM_LIMIT_BYTES = 56 * 1024 * 1024
PROMPT_CHUNK = 48
MOE_BLOCK_ROWS = 256
COMBINE_ROWS = 128

_A0, _B0, _C0, _D0, _G0 = 0, 512, 1544, 2568, 3344

_NN = (((1,), (0,)), ((), ()))
_NT = (((1,), (1,)), ((), ()))
_TN = (((0,), (0,)), ((), ()))


def _mm(a, b, dims=_NN, precision=None):
    return lax.dot_general(a, b, dims, precision=precision, preferred_element_type=F32)


def _bmm(a, b, dims=_NN):
    return _mm(a.astype(BF16), b.astype(BF16), dims)


def _split_bf16(x):
    hi = x.astype(BF16)
    return hi, (x - hi.astype(F32)).astype(BF16)


def _mm3(a, b, dims=_NN):
    ah, al = _split_bf16(a)
    bh, bl = _split_bf16(b)
    return _mm(ah, bh, dims) + (_mm(al, bh, dims) + _mm(ah, bl, dims))


def _mm_sel(x, sel, dims=_NN, terms=2):
    out = None
    r = x
    for _ in range(terms):
        p = r.astype(BF16)
        t = _mm(p, sel, dims)
        out = t if out is None else out + t
        r = r - p.astype(F32)
    return out


def _sigmoid(x):
    return jax.nn.sigmoid(x)


def _softplus(x):
    return jnp.maximum(x, 0.0) + jnp.log1p(jnp.exp(-jnp.abs(x)))


def _log_sigmoid(x):
    return jnp.minimum(x, 0.0) - jnp.log1p(jnp.exp(-jnp.abs(x)))


def _expm1(x):
    return jnp.tanh(0.5 * x) * (jnp.exp(x) + 1.0)


def _silu(x):
    return x * _sigmoid(x)


def _gelu_tanh(x):
    return 0.5 * x * (1.0 + jnp.tanh(0.7978845608028654 * (x + 0.044715 * (x * x * x))))


def _layernorm(x, g, b):
    mu = jnp.mean(x, axis=-1, keepdims=True)
    xc = x - mu
    var = jnp.mean(xc * xc, axis=-1, keepdims=True)
    return xc * lax.rsqrt(var + LN_EPS) * g + b


def _cparams(*sem):
    return pltpu.CompilerParams(dimension_semantics=sem, vmem_limit_bytes=VMEM_LIMIT_BYTES)


def _full(a):
    nd = a.ndim
    return pl.BlockSpec(a.shape, lambda *_: (0,) * nd)


def _ln_kernel(x_ref, g_ref, b_ref, o_ref):
    o_ref[...] = _layernorm(x_ref[...], g_ref[...], b_ref[...])


def _ln_call(x2, g, b, tm):
    t, d = x2.shape
    return pl.pallas_call(
        _ln_kernel,
        out_shape=jax.ShapeDtypeStruct((t, d), F32),
        grid=(t // tm,),
        in_specs=[pl.BlockSpec((tm, d), lambda i: (i, 0)), _full(g), _full(b)],
        out_specs=pl.BlockSpec((tm, d), lambda i: (i, 0)),
        compiler_params=_cparams("parallel"),
        name="ln_rows",
    )(x2, g, b)


def _combine_ln2_kernel(n, dcur_ref, dnxt_ref, x_ref, gate_ref, g_ref, b_ref, ys_hbm, o_ref, buf_ref, sem_ref):
    i = pl.program_id(0)
    tm = x_ref.shape[0]
    nrow = TOP_K * tm

    def row_copy(d_ref, r, slot):
        return pltpu.make_async_copy(ys_hbm.at[pl.ds(d_ref[0, 0, r], 1)], buf_ref.at[slot, pl.ds(r, 1)],
                                     sem_ref.at[slot])

    def issue(d_ref, slot):
        def body(r, carry):
            row_copy(d_ref, r, slot).start()
            return carry

        lax.fori_loop(0, nrow, body, 0, unroll=8)

    @pl.when(i == 0)
    def _():
        issue(dcur_ref, 0)

    @pl.when(i + 1 < n)
    def _():
        issue(dnxt_ref, (i + 1) % 2)

    slot = i % 2
    pltpu.make_async_copy(ys_hbm.at[pl.ds(0, nrow)], buf_ref.at[slot], sem_ref.at[slot]).wait()
    gate = gate_ref[...]
    ff = None
    for k in range(TOP_K):
        term = buf_ref[slot, pl.ds(k * tm, tm), :] * gate[:, k:k + 1]
        ff = term if ff is None else ff + term
    o_ref[...] = _layernorm(DEEPNORM_ALPHA * x_ref[...] + ff, g_ref[...], b_ref[...])


def _combine_ln2_call(x2, gates, dest_tiles, ys, g, b):
    t, d = x2.shape
    tm = COMBINE_ROWS
    n = t // tm
    row = pl.BlockSpec((tm, d), lambda i: (i, 0))
    dspec = lambda f: pl.BlockSpec((1, 1, TOP_K * tm), f, memory_space=pltpu.SMEM)
    return pl.pallas_call(
        functools.partial(_combine_ln2_kernel, n),
        out_shape=jax.ShapeDtypeStruct((t, d), F32),
        grid=(n,),
        in_specs=[dspec(lambda i: (i, 0, 0)), dspec(lambda i: (jnp.minimum(i + 1, n - 1), 0, 0)), row,
                  pl.BlockSpec((tm, LANES), lambda i: (i, 0)), _full(g), _full(b),
                  pl.BlockSpec(memory_space=pl.ANY)],
        out_specs=row,
        scratch_shapes=[pltpu.VMEM((2, TOP_K * tm, d), F32), pltpu.SemaphoreType.DMA((2,))],
        compiler_params=_cparams("arbitrary"),
        name="combine_ln2",
    )(dest_tiles, dest_tiles, x2, gates, g, b, ys)


def _merge_kernel(x_ref, ya_ref, yb_ref, yc_ref, yd_ref, wg_ref, wbr_ref, wout_ref, g_ref, b_ref, wr_ref, br_ref,
                  x1_ref, te_ref, tg_ref):
    x = x_ref[...]
    xb = x.astype(BF16)
    merged = None
    for n, y_ref in enumerate((ya_ref, yb_ref, yc_ref, yd_ref)):
        up = _mm(y_ref[...].astype(BF16), wbr_ref[n])
        gate = _sigmoid(_mm(xb, wg_ref[:, n * D_MODEL:(n + 1) * D_MODEL]))
        merged = up * gate if merged is None else merged + up * gate
    out = _mm(merged.astype(BF16), wout_ref[...])
    x1 = _layernorm(DEEPNORM_ALPHA * x + out, g_ref[...], b_ref[...])
    x1_ref[...] = x1
    logits = _mm(x1.astype(BF16), wr_ref[...]) + br_ref[...]
    lane = lax.broadcasted_iota(jnp.int32, logits.shape, 1)
    vals = []
    idxs = []
    v = logits
    for _ in range(TOP_K):
        m = jnp.max(v, axis=-1, keepdims=True)
        idx = jnp.min(jnp.where(v == m, lane, LANES), axis=-1, keepdims=True)
        vals.append(m)
        idxs.append(idx)
        v = jnp.where(lane == idx, -jnp.inf, v)
    es = [jnp.exp(m - vals[0]) for m in vals]
    den = es[0] + es[1] + es[2] + es[3]
    te = jnp.zeros(logits.shape, jnp.int32)
    tg = jnp.zeros(logits.shape, F32)
    for j in range(TOP_K):
        te = jnp.where(lane == j, idxs[j], te)
        tg = jnp.where(lane == j, es[j] / den, tg)
    te_ref[...] = te
    tg_ref[...] = tg


def _merge_call(x2, ys, wg, wbr, wout, g, b, wr, br, tm):
    t, d = x2.shape
    row = pl.BlockSpec((tm, d), lambda i: (i, 0))
    yrow = pl.BlockSpec((tm, MIX_W), lambda i: (i, 0))
    lrow = pl.BlockSpec((tm, LANES), lambda i: (i, 0))
    return pl.pallas_call(
        _merge_kernel,
        out_shape=(jax.ShapeDtypeStruct((t, d), F32),
                   jax.ShapeDtypeStruct((t, LANES), jnp.int32), jax.ShapeDtypeStruct((t, LANES), F32)),
        grid=(t // tm,),
        in_specs=[row, yrow, yrow, yrow, yrow, _full(wg), _full(wbr), _full(wout), _full(g), _full(b),
                  _full(wr), _full(br)],
        out_specs=(row, lrow, lrow),
        compiler_params=_cparams("parallel"),
        name="merge_ln_router",
    )(x2, *ys, wg, wbr, wout, g, b, wr, br)


def _dispatch_kernel(n_p, n_s, n, d_ref, xp_ref, xs_ref, o_hbm, buf_ref, sem_ref):
    i = pl.program_id(0)
    tm = xp_ref.shape[0]
    nrow = TOP_K * tm
    slot = i % 2

    def wait_slot(s):
        pltpu.make_async_copy(o_hbm.at[pl.ds(0, nrow)], o_hbm.at[pl.ds(0, nrow)], sem_ref.at[s]).wait()

    @pl.when(i >= 2)
    def _():
        wait_slot(slot)

    @pl.when(i < n_p)
    def _():
        buf_ref[slot] = xp_ref[...]

    @pl.when((i >= n_p) & (i < n_p + n_s))
    def _():
        buf_ref[slot] = xs_ref[...]

    @pl.when(i >= n_p + n_s)
    def _():
        buf_ref[slot] = jnp.zeros((tm, D_MODEL), F32)

    for k in range(TOP_K):
        def body(t, carry, k=k):
            pltpu.make_async_copy(buf_ref.at[slot, pl.ds(t, 1)], o_hbm.at[pl.ds(d_ref[0, 0, k * tm + t], 1)],
                                  sem_ref.at[slot]).start()
            return carry

        lax.fori_loop(0, tm, body, 0, unroll=8)

    @pl.when(i == n - 1)
    def _():
        if n >= 2:
            wait_slot(1 - slot)
        wait_slot(slot)


def _dispatch_call(x_p, x_s, dest_tiles, mp):
    tm = COMBINE_ROWS
    d = x_p.shape[1]
    n_p, n_s, n = x_p.shape[0] // tm, x_s.shape[0] // tm, dest_tiles.shape[0]
    return pl.pallas_call(
        functools.partial(_dispatch_kernel, n_p, n_s, n),
        out_shape=jax.ShapeDtypeStruct((mp, d), F32),
        grid=(n,),
        in_specs=[pl.BlockSpec((1, 1, TOP_K * tm), lambda i: (i, 0, 0), memory_space=pltpu.SMEM),
                  pl.BlockSpec((tm, d), lambda i: (jnp.minimum(i, n_p - 1), 0)),
                  pl.BlockSpec((tm, d), lambda i: (jnp.clip(i - n_p, 0, n_s - 1), 0))],
        out_specs=pl.BlockSpec(memory_space=pl.ANY),
        scratch_shapes=[pltpu.VMEM((2, tm, d), F32), pltpu.SemaphoreType.DMA((2,))],
        compiler_params=_cparams("arbitrary"),
        name="moe_dispatch",
    )(dest_tiles, x_p, x_s)


def _moe_kernel(be_ref, nused_ref, xs_ref, wu_ref, bu_ref, wd_ref, bd_ref, o_ref):
    i = pl.program_id(0)

    @pl.when(i < nused_ref[0])
    def _():
        h = _mm(xs_ref[...].astype(BF16), wu_ref[0, 0]) + bu_ref[0, 0]
        g = jnp.minimum(h[:, :D_FF], SWIGLU_LIMIT)
        u = jnp.clip(h[:, D_FF:], -SWIGLU_LIMIT, SWIGLU_LIMIT)
        act = (u + 1.0) * g * _sigmoid(SWIGLU_ALPHA * g)
        o_ref[...] = _mm(act.astype(BF16), wd_ref[0, 0]) + bd_ref[0, 0]

    @pl.when(i >= nused_ref[0])
    def _():
        o_ref[...] = jnp.zeros(o_ref.shape, F32)


def _moe_call(layer, block_e, n_used, xs, wu, bu, wd, bd):
    mp, d = xs.shape
    nb = mp // MOE_BLOCK_ROWS
    grid_spec = pltpu.PrefetchScalarGridSpec(
        num_scalar_prefetch=2,
        grid=(nb,),
        in_specs=[
            pl.BlockSpec((MOE_BLOCK_ROWS, d), lambda i, be, nu: (i, 0)),
            pl.BlockSpec((1, 1, d, 2 * D_FF), lambda i, be, nu: (layer, be[i], 0, 0)),
            pl.BlockSpec((1, 1, 1, 2 * D_FF), lambda i, be, nu: (layer, be[i], 0, 0)),
            pl.BlockSpec((1, 1, D_FF, d), lambda i, be, nu: (layer, be[i], 0, 0)),
            pl.BlockSpec((1, 1, 1, d), lambda i, be, nu: (layer, be[i], 0, 0)),
        ],
        out_specs=pl.BlockSpec((MOE_BLOCK_ROWS, d), lambda i, be, nu: (i, 0)),
    )
    return pl.pallas_call(
        _moe_kernel,
        out_shape=jax.ShapeDtypeStruct((mp, d), F32),
        grid_spec=grid_spec,
        compiler_params=_cparams("arbitrary"),
        name="moe_experts",
    )(block_e, n_used, xs, wu, bu, wd, bd)


def _chunk_pos(bb, c, width):
    return lax.broadcasted_iota(jnp.int32, (bb, c, width), 1).reshape(bb * c, width)


def _seg_scan(x, pos, c, op, ident):
    s = 1
    while s < c:
        x = op(x, jnp.where(pos >= s, pltpu.roll(x, s, 0), ident))
        s *= 2
    return x


def _chunk_last(x, pos, bb, c):
    w = x.shape[1]
    x3 = jnp.where(pos == c - 1, x, 0.0).reshape(bb, c, w)
    return jnp.broadcast_to(jnp.sum(x3, axis=1, keepdims=True), (bb, c, w)).reshape(bb * c, w)


def _causal_conv(x3, ext_ref, prev_ref, cw_ref, c):
    ext_ref[:, 0:SUBLANES, :] = prev_ref[...]
    ext_ref[:, SUBLANES:SUBLANES + c, :] = x3
    y = None
    for j in range(CONV_W):
        lo = SUBLANES - (CONV_W - 1) + j
        tap = ext_ref[:, lo:lo + c, :] * cw_ref[j:j + 1, :]
        y = tap if y is None else y + tap
    prev_ref[...] = ext_ref[:, c:c + SUBLANES, :]
    return y


def _head_masks(width=MIX_W):
    lane = lax.broadcasted_iota(jnp.int32, (1, width), 1)
    return [(lane >= h * HEAD_DIM) & (lane < (h + 1) * HEAD_DIM) for h in range(N_HEADS)]


def _lane_masks():
    lane = lax.broadcasted_iota(jnp.int32, (1, LANES), 1)
    return [lane == h for h in range(N_HEADS)]


def _stack(xg, masks, g, c, cs):
    pieces = []
    for b in range(g):
        xb = xg[b * c:(b + 1) * c]
        for m in masks:
            pieces.append(jnp.where(m, xb, 0.0))
            if cs > c:
                pieces.append(jnp.zeros((cs - c, xg.shape[1]), xg.dtype))
    return jnp.concatenate(pieces, axis=0)


def _unstack(y, g, c, cs):
    outs = []
    for b in range(g):
        acc = None
        for h in range(N_HEADS):
            r0 = (b * N_HEADS + h) * cs
            piece = y[r0:r0 + c]
            acc = piece if acc is None else acc + piece
        outs.append(acc)
    return outs[0] if g == 1 else jnp.concatenate(outs, axis=0)


def _bd_masks(cs):
    r = lax.broadcasted_iota(jnp.int32, (BD, BD), 0)
    q = lax.broadcasted_iota(jnp.int32, (BD, BD), 1)
    shift = cs.bit_length() - 1
    same = (r >> shift) == (q >> shift)
    tr = r & (cs - 1)
    tq = q & (cs - 1)
    return same & (tq <= tr), same & (tq < tr)


def _head_block_mask():
    r = lax.broadcasted_iota(jnp.int32, (MIX_W, MIX_W), 0)
    q = lax.broadcasted_iota(jnp.int32, (MIX_W, MIX_W), 1)
    return (r // HEAD_DIM) == (q // HEAD_DIM)


def _head_sum(x, ones_bd):
    return _mm_sel(x, ones_bd)


def _to_hb(cols, expand):
    return _mm_sel(cols, expand, terms=3)


def _stack_cols(cols_g, g, c, cs):
    st = _stack(cols_g, _lane_masks(), g, c, cs)
    col = jnp.sum(st, axis=1, keepdims=True)
    ones = jnp.ones((SUBLANES, LANES), F32)
    row = _mm(ones, st, _NT, precision=HIGHEST)[0:1, :]
    return col, row


def _group_rows(ref, gi, g, c):
    w = ref.shape[2]
    if g == 1:
        return ref[gi]
    return ref[pl.ds(gi * g, g)].reshape(g * c, w)


def _seq_specs(bb, c):
    x_spec = pl.BlockSpec((bb, c, D_MODEL), lambda i, k: (i, k, 0))
    y_spec = pl.BlockSpec((bb, c, MIX_W), lambda i, k: (i, k, 0))
    return x_spec, y_spec


def _state_spec(bb, *tail):
    nt = len(tail)
    return pl.BlockSpec((bb,) + tuple(tail), lambda i, k: (i,) + (0,) * nt)


def _lru_kernel(x_ref, w_ref, cw_ref, cb_ref, wra_ref, bra_ref, wix_ref, bix_ref, lam_ref, conv0_ref, h0_ref,
                y_ref, conv1_ref, h1_ref, ext_ref, prev_ref, hst_ref):
    k = pl.program_id(1)
    bb, c, _ = x_ref.shape
    n = bb * c

    @pl.when(k == 0)
    def _():
        prev_ref[...] = conv0_ref[...]
        hst_ref[...] = h0_ref[...]

    xb = x_ref[...].reshape(n, D_MODEL).astype(BF16)
    proj = _mm(xb, w_ref[...])
    a_gate = proj[:, MIX_W:]
    xa = _causal_conv(proj[:, :MIX_W].reshape(bb, c, MIX_W), ext_ref, prev_ref, cw_ref, c) + cb_ref[...]
    xa = xa.reshape(n, MIX_W)
    xab = xa.astype(BF16)
    r = _sigmoid(_mm(xab, wra_ref[...]) + bra_ref[...])
    i = _sigmoid(_mm(xab, wix_ref[...]) + bix_ref[...])
    log_a = -LRU_C * r * _softplus(-lam_ref[...])
    a_cum = jnp.exp(log_a)
    b_cum = jnp.sqrt(-_expm1(2.0 * log_a)) * (i * xa)
    pos = _chunk_pos(bb, c, MIX_W)
    s = 1
    while s < c:
        keep = pos >= s
        a_sh = pltpu.roll(a_cum, s, 0)
        b_sh = pltpu.roll(b_cum, s, 0)
        b_cum = jnp.where(keep, a_cum * b_sh + b_cum, b_cum)
        a_cum = jnp.where(keep, a_cum * a_sh, a_cum)
        s *= 2
    h0 = jnp.broadcast_to(hst_ref[:, SUBLANES - 1:SUBLANES, :], (bb, c, MIX_W)).reshape(n, MIX_W)
    h = a_cum * h0 + b_cum
    y_ref[...] = (h * _gelu_tanh(a_gate)).reshape(bb, c, MIX_W)
    hst_ref[...] = h.reshape(bb, c, MIX_W)[:, c - SUBLANES:c, :]

    @pl.when(k == pl.num_programs(1) - 1)
    def _():
        conv1_ref[...] = prev_ref[...]
        h1_ref[...] = hst_ref[...]


def _lru_call(x3, w, cw, cb, wra, bra, wix, bix, lam, conv0, h0, bb, c):
    b, l, _ = x3.shape
    x_spec, y_spec = _seq_specs(bb, c)
    st = _state_spec(bb, SUBLANES, MIX_W)
    consts = (w, cw, cb, wra, bra, wix, bix, lam)
    return pl.pallas_call(
        _lru_kernel,
        out_shape=(jax.ShapeDtypeStruct((b, l, MIX_W), F32), jax.ShapeDtypeStruct((b, SUBLANES, MIX_W), F32),
                   jax.ShapeDtypeStruct((b, SUBLANES, MIX_W), F32)),
        grid=(b // bb, l // c),
        in_specs=[x_spec] + [_full(a) for a in consts] + [st, st],
        out_specs=(y_spec, st, st),
        scratch_shapes=[pltpu.VMEM((bb, c + SUBLANES, MIX_W), F32), pltpu.VMEM((bb, SUBLANES, MIX_W), F32),
                        pltpu.VMEM((bb, SUBLANES, MIX_W), F32)],
        compiler_params=_cparams("parallel", "arbitrary"),
        name="mixer_rglru",
    )(x3, *consts, conv0, h0)


def _hgrn_kernel(x_ref, w_ref, wlo_ref, loglb_ref, log1mlb_ref, onemlb_ref, ng_ref, ones_ref, st0_ref,
                 y_ref, st1_ref, st_ref, q_s, k_s, i_s, b_s, qe_s, kd_s, el_s, o_s):
    kk = pl.program_id(1)
    bb, c, _ = x_ref.shape
    n = bb * c
    nblk = c // SUBLANES

    @pl.when(kk == 0)
    def _():
        st_ref[...] = st0_ref[...]

    xb, xlo = _split_bf16(x_ref[...].reshape(n, D_MODEL))
    proj = _mm(xb, w_ref[...])
    q = proj[:, 0:MIX_W] * HEAD_DIM ** -0.5
    z = proj[:, MIX_W:2 * MIX_W] + _mm(xlo, w_ref[:, MIX_W:2 * MIX_W]) + _mm(xb, wlo_ref[...])
    iv = proj[:, 2 * MIX_W:3 * MIX_W]
    cg = proj[:, 3 * MIX_W:4 * MIX_W]
    lo = loglb_ref[...]
    hi = log1mlb_ref[...] + _log_sigmoid(z)
    log_f = jnp.maximum(lo, hi) + jnp.log1p(jnp.exp(-jnp.abs(lo - hi)))
    kc = onemlb_ref[...] * _sigmoid(-z)
    pos = _chunk_pos(bb, c, MIX_W)
    bc = _seg_scan(log_f, pos, c, jnp.add, 0.0)
    b_last = _chunk_last(bc, pos, bb, c)
    ones_bd = ones_ref[...]

    sub = pos & (SUBLANES - 1)
    o_band = None
    for d in range(SUBLANES):
        kr = pltpu.roll(kc, d, 0) if d else kc
        br = pltpu.roll(bc, d, 0) if d else bc
        ir = pltpu.roll(iv, d, 0) if d else iv
        wd = jnp.where(sub >= d, q * kr * jnp.exp(jnp.minimum(bc - br, 0.0)), 0.0)
        term = _head_sum(wd, ones_bd) * ir
        o_band = term if o_band is None else o_band + term

    q_s[...] = q.reshape(bb, c, MIX_W)
    k_s[...] = kc.reshape(bb, c, MIX_W)
    i_s[...] = iv.reshape(bb, c, MIX_W)
    b_s[...] = bc.reshape(bb, c, MIX_W)
    qe_s[...] = (q * jnp.exp(bc)).reshape(bb, c, MIX_W)
    kd_s[...] = (kc * jnp.exp(b_last - bc)).reshape(bb, c, MIX_W)
    el_s[...] = jnp.exp(b_last).reshape(bb, c, MIX_W)

    hmasks = _head_masks()
    blockmask = _head_block_mask()
    pad_rows = LANES - c
    rowid = lax.broadcasted_iota(jnp.int32, (LANES, MIX_W), 0)
    zpad = jnp.zeros((pad_rows, MIX_W), F32)

    def per_b(b, carry):
        st = st_ref[b]
        o = _bmm(qe_s[b], st, _NT)
        if nblk > 1:
            kp = jnp.concatenate([k_s[b], zpad], axis=0)
            bp = jnp.concatenate([b_s[b], zpad], axis=0)
            ip = jnp.concatenate([i_s[b], zpad], axis=0).astype(BF16)
            pieces = [jnp.zeros((SUBLANES, MIX_W), F32)]
            for blk in range(1, nblk):
                r0 = blk * SUBLANES
                ref_b = b_s[b, r0 - 1:r0, :]
                qi = q_s[b, r0:r0 + SUBLANES, :] * jnp.exp(jnp.minimum(b_s[b, r0:r0 + SUBLANES, :] - ref_b, 0.0))
                ki = jnp.where(rowid < r0, kp * jnp.exp(jnp.minimum(ref_b - bp, 0.0)), 0.0)
                qst = jnp.concatenate([jnp.where(m, qi, 0.0) for m in hmasks], axis=0)
                att = _bmm(qst, ki, _NT)
                res = _mm(att.astype(BF16), ip)
                acc = None
                for h in range(N_HEADS):
                    part = jnp.where(hmasks[h], res[h * SUBLANES:(h + 1) * SUBLANES], 0.0)
                    acc = part if acc is None else acc + part
                pieces.append(acc)
            o = o + jnp.concatenate(pieces, axis=0)
        o_s[b] = o
        upd = _mm3(jnp.concatenate([i_s[b], zpad], axis=0), jnp.concatenate([kd_s[b], zpad], axis=0), _TN)
        st_ref[b] = st * el_s[b, 0:1, :] + jnp.where(blockmask, upd, 0.0)
        return carry

    lax.fori_loop(0, bb, per_b, 0)

    o = o_s[...].reshape(n, MIX_W) + o_band
    ms = _head_sum(o * o, ones_bd) * (1.0 / HEAD_DIM)
    y = o * lax.rsqrt(ms + NORM_EPS) * ng_ref[...] * _silu(cg)
    y_ref[...] = y.reshape(bb, c, MIX_W)

    @pl.when(kk == pl.num_programs(1) - 1)
    def _():
        st1_ref[...] = st_ref[...]


def _hgrn_call(x3, w, wlo, loglb, log1mlb, onemlb, ng, ones_bd, st0, bb, c):
    b, l, _ = x3.shape
    x_spec, y_spec = _seq_specs(bb, c)
    st = _state_spec(bb, MIX_W, MIX_W)
    consts = (w, wlo, loglb, log1mlb, onemlb, ng, ones_bd)
    rows = pltpu.VMEM((bb, c, MIX_W), F32)
    return pl.pallas_call(
        _hgrn_kernel,
        out_shape=(jax.ShapeDtypeStruct((b, l, MIX_W), F32), jax.ShapeDtypeStruct((b, MIX_W, MIX_W), F32)),
        grid=(b // bb, l // c),
        in_specs=[x_spec] + [_full(a) for a in consts] + [st],
        out_specs=(y_spec, st),
        scratch_shapes=[pltpu.VMEM((bb, MIX_W, MIX_W), F32)] + [rows] * 8,
        compiler_params=_cparams("parallel", "arbitrary"),
        name="mixer_hgrn2",
    )(x3, *consts, st0)


def _mlstm_kernel(g, cs, x_ref, w_ref, wlo_ref, cw_ref, cb_ref, wq_ref, wk_ref, bi_ref, bf_ref, ng_ref, ones_ref, exp_ref,
                  conv0_ref, c0_ref, n0_ref, m0_ref,
                  y_ref, conv1_ref, c1_ref, n1_ref, m1_ref,
                  ext_ref, prev_ref, cst_ref, nst_ref, mst_ref, q_s, k_s, v_s, kw_s, dec_s, a_s, g_s, num_s, den_s,
                  qc_s):
    kk = pl.program_id(1)
    bb, c, _ = x_ref.shape
    n = bb * c
    ngroups = bb // g

    @pl.when(kk == 0)
    def _():
        prev_ref[...] = conv0_ref[...]
        cst_ref[...] = c0_ref[...]
        nst_ref[...] = n0_ref[...]
        mst_ref[...] = m0_ref[...]

    xb, xlo = _split_bf16(x_ref[...].reshape(n, D_MODEL))
    proj = _mm(xb, w_ref[...])
    gates = proj[:, 3 * MIX_W:] + _mm(xlo, w_ref[:, 3 * MIX_W:]) + _mm(xb, wlo_ref[...])
    xm = _causal_conv(proj[:, 0:MIX_W].reshape(bb, c, MIX_W), ext_ref, prev_ref, cw_ref, c) + cb_ref[...]
    xm = _silu(xm).reshape(n, MIX_W).astype(BF16)
    q = _mm(xm, wq_ref[...])
    k = _mm(xm, wk_ref[...]) * HEAD_DIM ** -0.5
    v = proj[:, MIX_W:2 * MIX_W]
    d_o = proj[:, 2 * MIX_W:3 * MIX_W]
    log_i = gates[:, 0:LANES] + bi_ref[...]
    log_f = _log_sigmoid(gates[:, LANES:2 * LANES] + bf_ref[...])
    pos = _chunk_pos(bb, c, LANES)
    bc = _seg_scan(log_f, pos, c, jnp.add, 0.0)
    gg = log_i - bc
    cm = _seg_scan(gg, pos, c, jnp.maximum, -jnp.inf)
    m0 = jnp.broadcast_to(mst_ref[:, SUBLANES - 1:SUBLANES, :], (bb, c, LANES)).reshape(n, LANES)
    m_t = jnp.maximum(m0 + bc, bc + cm)
    b_last = _chunk_last(bc, pos, bb, c)
    m_last = _chunk_last(m_t, pos, bb, c)
    expand = exp_ref[...]
    ones_bd = ones_ref[...]
    inter = _to_hb(jnp.exp(m0 + bc - m_t), expand)
    e_negm = _to_hb(jnp.exp(-m_t), expand)
    wl = _to_hb(jnp.exp(gg + b_last - m_last), expand)
    nrows = jnp.broadcast_to(nst_ref[:, 0:1, :], (bb, c, MIX_W)).reshape(n, MIX_W)
    qn = _head_sum(q * nrows, ones_bd)

    q_s[...] = q.reshape(bb, c, MIX_W)
    k_s[...] = k.reshape(bb, c, MIX_W)
    v_s[...] = v.reshape(bb, c, MIX_W)
    kw_s[...] = (k * wl).reshape(bb, c, MIX_W)
    dec_s[...] = _to_hb(jnp.exp(m0 + b_last - m_last), expand).reshape(bb, c, MIX_W)
    a_s[...] = (bc - m_t).reshape(bb, c, LANES)
    g_s[...] = gg.reshape(bb, c, LANES)

    hmasks = _head_masks()
    causal, _ = _bd_masks(cs)
    rowb = lax.broadcasted_iota(jnp.int32, (BD, MIX_W), 0) // (N_HEADS * cs)
    sel = _stack(jnp.ones((g * c, MIX_W), F32), hmasks, g, c, cs)

    def per_group(gi, carry):
        qg = _group_rows(q_s, gi, g, c)
        qst = _stack(qg, hmasks, g, c, cs).astype(BF16)
        kst = _stack(_group_rows(k_s, gi, g, c), hmasks, g, c, cs).astype(BF16)
        vst = _stack(_group_rows(v_s, gi, g, c), hmasks, g, c, cs).astype(BF16)
        kwst = _stack(_group_rows(kw_s, gi, g, c), hmasks, g, c, cs)
        acol, _ = _stack_cols(_group_rows(a_s, gi, g, c), g, c, cs)
        _, grow = _stack_cols(_group_rows(g_s, gi, g, c), g, c, cs)
        wmat = jnp.where(causal, jnp.exp(jnp.minimum(acol + grow, 0.0)), 0.0)
        qkw = _mm(qst, kst, _NT) * wmat
        num = _unstack(_mm(qkw.astype(BF16), vst), g, c, cs)
        den = _unstack(jnp.sum(qkw, axis=1, keepdims=True) * sel, g, c, cs)
        for bl in range(g):
            b = gi * g + bl
            cmat = cst_ref[b]
            qc_s[b] = _bmm(qg[bl * c:(bl + 1) * c], cmat)
            num_s[b] = num[bl * c:(bl + 1) * c]
            den_s[b] = den[bl * c:(bl + 1) * c]
            kwb = kwst if g == 1 else jnp.where(rowb == bl, kwst, 0.0)
            dec_row = dec_s[b, 0:1, :]
            cst_ref[b] = cmat * dec_row + _mm(kwb.astype(BF16), vst, _TN)
            ksum = jnp.sum(kw_s[b], axis=0, keepdims=True)
            nst_ref[b] = nst_ref[b] * dec_row + jnp.broadcast_to(ksum, (SUBLANES, MIX_W))
        return carry

    lax.fori_loop(0, ngroups, per_group, 0)

    num = num_s[...].reshape(n, MIX_W) + inter * qc_s[...].reshape(n, MIX_W)
    den = den_s[...].reshape(n, MIX_W) + inter * qn
    h = num / jnp.maximum(jnp.abs(den), e_negm)
    mu = _head_sum(h, ones_bd) * (1.0 / HEAD_DIM)
    hc = h - mu
    var = _head_sum(hc * hc, ones_bd) * (1.0 / HEAD_DIM)
    y = hc * lax.rsqrt(var + NORM_EPS) * ng_ref[...] * _sigmoid(d_o)
    y_ref[...] = y.reshape(bb, c, MIX_W)
    mst_ref[...] = m_t.reshape(bb, c, LANES)[:, c - SUBLANES:c, :]

    @pl.when(kk == pl.num_programs(1) - 1)
    def _():
        conv1_ref[...] = prev_ref[...]
        c1_ref[...] = cst_ref[...]
        n1_ref[...] = nst_ref[...]
        m1_ref[...] = mst_ref[...]


def _mlstm_call(x3, w, wlo, cw, cb, wq, wk, bi, bf, ng, ones_bd, expand, conv0, c0, n0, m0, bb, c, g, cs):
    b, l, _ = x3.shape
    x_spec, y_spec = _seq_specs(bb, c)
    st_conv = _state_spec(bb, SUBLANES, MIX_W)
    st_c = _state_spec(bb, MIX_W, MIX_W)
    st_n = _state_spec(bb, SUBLANES, MIX_W)
    st_m = _state_spec(bb, SUBLANES, LANES)
    consts = (w, wlo, cw, cb, wq, wk, bi, bf, ng, ones_bd, expand)
    rows = pltpu.VMEM((bb, c, MIX_W), F32)
    cols = pltpu.VMEM((bb, c, LANES), F32)
    return pl.pallas_call(
        functools.partial(_mlstm_kernel, g, cs),
        out_shape=(jax.ShapeDtypeStruct((b, l, MIX_W), F32), jax.ShapeDtypeStruct((b, SUBLANES, MIX_W), F32),
                   jax.ShapeDtypeStruct((b, MIX_W, MIX_W), F32), jax.ShapeDtypeStruct((b, SUBLANES, MIX_W), F32),
                   jax.ShapeDtypeStruct((b, SUBLANES, LANES), F32)),
        grid=(b // bb, l // c),
        in_specs=[x_spec] + [_full(a) for a in consts] + [st_conv, st_c, st_n, st_m],
        out_specs=(y_spec, st_conv, st_c, st_n, st_m),
        scratch_shapes=[pltpu.VMEM((bb, c + SUBLANES, MIX_W), F32), pltpu.VMEM((bb, SUBLANES, MIX_W), F32),
                        pltpu.VMEM((bb, MIX_W, MIX_W), F32), pltpu.VMEM((bb, SUBLANES, MIX_W), F32),
                        pltpu.VMEM((bb, SUBLANES, LANES), F32),
                        rows, rows, rows, rows, rows, cols, cols, rows, rows, rows],
        compiler_params=_cparams("parallel", "arbitrary"),
        name="mixer_mlstm",
    )(x3, *consts, conv0, c0, n0, m0)


def _gdn_kernel(g, cs, x_ref, w_ref, wlo_ref, cw_ref, alog_ref, dtb_ref, ng_ref, ones_ref, exp_ref, conv0_ref, s0_ref,
                y_ref, conv1_ref, s1_ref,
                ext_ref, prev_ref, sst_ref, q_s, k_s, vb_s, kb_s, qe_s, kd_s, el_s, gc_s, be_s, o_s):
    kk = pl.program_id(1)
    bb, c, _ = x_ref.shape
    n = bb * c
    ngroups = bb // g
    qkv_w = 3 * MIX_W

    @pl.when(kk == 0)
    def _():
        prev_ref[...] = conv0_ref[...]
        sst_ref[...] = s0_ref[...]

    xb, xlo = _split_bf16(x_ref[...].reshape(n, D_MODEL))
    proj = _mm(xb, w_ref[...])
    gates = proj[:, 4 * MIX_W:] + _mm(xlo, w_ref[:, 4 * MIX_W:]) + _mm(xb, wlo_ref[...])
    qkv = _causal_conv(proj[:, 0:qkv_w].reshape(bb, c, qkv_w), ext_ref, prev_ref, cw_ref, c)
    qkv = _silu(qkv).reshape(n, qkv_w)
    ones_bd = ones_ref[...]
    expand = exp_ref[...]
    q = qkv[:, 0:MIX_W]
    k = qkv[:, MIX_W:2 * MIX_W]
    v = qkv[:, 2 * MIX_W:3 * MIX_W]
    q = q * lax.rsqrt(_head_sum(q * q, ones_bd) + NORM_EPS) * HEAD_DIM ** -0.5
    k = k * lax.rsqrt(_head_sum(k * k, ones_bd) + NORM_EPS)
    z = proj[:, qkv_w:qkv_w + MIX_W]
    a_in = gates[:, 0:LANES]
    b_in = gates[:, LANES:2 * LANES]
    gdec = -jnp.exp(alog_ref[...]) * _softplus(a_in + dtb_ref[...])
    beta = _sigmoid(b_in)
    pos = _chunk_pos(bb, c, LANES)
    gcum = _seg_scan(gdec, pos, c, jnp.add, 0.0)
    g_last = _chunk_last(gcum, pos, bb, c)
    beta_hb = _to_hb(beta, expand)
    eg_hb = _to_hb(jnp.exp(gcum), expand)

    q_s[...] = q.reshape(bb, c, MIX_W)
    k_s[...] = k.reshape(bb, c, MIX_W)
    vb_s[...] = (v * beta_hb).reshape(bb, c, MIX_W)
    kb_s[...] = (k * beta_hb * eg_hb).reshape(bb, c, MIX_W)
    qe_s[...] = (q * eg_hb).reshape(bb, c, MIX_W)
    kd_s[...] = (k * _to_hb(jnp.exp(g_last - gcum), expand)).reshape(bb, c, MIX_W)
    el_s[...] = _to_hb(jnp.exp(g_last), expand).reshape(bb, c, MIX_W)
    gc_s[...] = gcum.reshape(bb, c, LANES)
    be_s[...] = beta.reshape(bb, c, LANES)

    hmasks = _head_masks()
    causal, strict = _bd_masks(cs)
    rowb = lax.broadcasted_iota(jnp.int32, (BD, MIX_W), 0) // (N_HEADS * cs)
    rows_b = N_HEADS * cs

    def per_group(gi, carry):
        qg = _group_rows(qe_s, gi, g, c)
        qst = _stack(_group_rows(q_s, gi, g, c), hmasks, g, c, cs).astype(BF16)
        kst = _stack(_group_rows(k_s, gi, g, c), hmasks, g, c, cs).astype(BF16)
        kdst = _stack(_group_rows(kd_s, gi, g, c), hmasks, g, c, cs)
        rhs = jnp.concatenate([_stack(_group_rows(vb_s, gi, g, c), hmasks, g, c, cs),
                               _stack(_group_rows(kb_s, gi, g, c), hmasks, g, c, cs)], axis=1)
        gcol, grow = _stack_cols(_group_rows(gc_s, gi, g, c), g, c, cs)
        bcol, _ = _stack_cols(_group_rows(be_s, gi, g, c), g, c, cs)
        decay = jnp.exp(jnp.minimum(gcol - grow, 0.0))
        amat = jnp.where(strict, bcol * _mm(kst, kst, _NT) * decay, 0.0)
        x = rhs - _bmm(amat, rhs)
        p = amat
        span = 2
        while span < cs:
            p = _bmm(p, p)
            x = x + _bmm(p, x)
            span *= 2
        w_v = x[:, 0:MIX_W]
        w_k = x[:, MIX_W:2 * MIX_W]
        us = []
        for bl in range(g):
            b = gi * g + bl
            r0 = bl * rows_b
            us.append(w_v[r0:r0 + rows_b] - _bmm(w_k[r0:r0 + rows_b], sst_ref[b]))
        u = us[0] if g == 1 else jnp.concatenate(us, axis=0)
        ub = u.astype(BF16)
        qk = jnp.where(causal, _mm(qst, kst, _NT) * decay, 0.0)
        o_intra = _unstack(_mm(qk.astype(BF16), ub), g, c, cs)
        for bl in range(g):
            b = gi * g + bl
            smat = sst_ref[b]
            o_s[b] = o_intra[bl * c:(bl + 1) * c] + _bmm(qg[bl * c:(bl + 1) * c], smat)
            kdb = kdst if g == 1 else jnp.where(rowb == bl, kdst, 0.0)
            sst_ref[b] = smat * el_s[b, 0:1, :] + _mm(kdb.astype(BF16), ub, _TN)
        return carry

    lax.fori_loop(0, ngroups, per_group, 0)

    o = o_s[...].reshape(n, MIX_W)
    ms = _head_sum(o * o, ones_bd) * (1.0 / HEAD_DIM)
    y = o * lax.rsqrt(ms + NORM_EPS) * ng_ref[...] * _silu(z)
    y_ref[...] = y.reshape(bb, c, MIX_W)

    @pl.when(kk == pl.num_programs(1) - 1)
    def _():
        conv1_ref[...] = prev_ref[...]
        s1_ref[...] = sst_ref[...]


def _gdn_call(x3, w, wlo, cw, alog, dtb, ng, ones_bd, expand, conv0, s0, bb, c, g, cs):
    b, l, _ = x3.shape
    x_spec, y_spec = _seq_specs(bb, c)
    st_conv = _state_spec(bb, SUBLANES, 3 * MIX_W)
    st_s = _state_spec(bb, MIX_W, MIX_W)
    consts = (w, wlo, cw, alog, dtb, ng, ones_bd, expand)
    rows = pltpu.VMEM((bb, c, MIX_W), F32)
    cols = pltpu.VMEM((bb, c, LANES), F32)
    return pl.pallas_call(
        functools.partial(_gdn_kernel, g, cs),
        out_shape=(jax.ShapeDtypeStruct((b, l, MIX_W), F32), jax.ShapeDtypeStruct((b, SUBLANES, 3 * MIX_W), F32),
                   jax.ShapeDtypeStruct((b, MIX_W, MIX_W), F32)),
        grid=(b // bb, l // c),
        in_specs=[x_spec] + [_full(a) for a in consts] + [st_conv, st_s],
        out_specs=(y_spec, st_conv, st_s),
        scratch_shapes=[pltpu.VMEM((bb, c + SUBLANES, 3 * MIX_W), F32), pltpu.VMEM((bb, SUBLANES, 3 * MIX_W), F32),
                        pltpu.VMEM((bb, MIX_W, MIX_W), F32),
                        rows, rows, rows, rows, rows, rows, rows, cols, cols, rows],
        compiler_params=_cparams("parallel", "arbitrary"),
        name="mixer_gdn",
    )(x3, *consts, conv0, s0)


def _pad_cols(a, width):
    return jnp.pad(a, ((0, 0), (0, width - a.shape[1])))


def _row(a, width=None):
    a = a.reshape(1, -1).astype(F32)
    return a if width is None else _pad_cols(a, width)


def _block_diag(w4):
    h, d, _ = w4.shape
    eye = jnp.eye(h, dtype=w4.dtype)
    return jnp.einsum("hij,hg->higj", w4, eye).reshape(h * d, h * d)


def _bd_state(s):
    b, h, d, _ = s.shape
    eye = jnp.eye(h, dtype=s.dtype)
    return jnp.einsum("bhij,hg->bhigj", s, eye).reshape(b, h * d, h * d)


def _bd_blocks(s):
    b = s.shape[0]
    s5 = s.reshape(b, N_HEADS, HEAD_DIM, N_HEADS, HEAD_DIM)
    return jnp.stack([s5[:, h, :, h, :] for h in range(N_HEADS)], axis=1)


def _tail8(buf):
    return jnp.pad(buf, ((0, 0), (SUBLANES - (CONV_W - 1), 0), (0, 0)))


def _bcast8(a):
    return jnp.broadcast_to(a[:, None, :], (a.shape[0], SUBLANES, a.shape[1]))


def _layer_params(P, lb, l):
    w_f32 = P["w_in"][l].astype(F32)
    w_in = w_f32.astype(BF16)
    w_lo = (w_f32 - w_in.astype(F32)).astype(BF16)

    def gate_tiles(w, lo, hi):
        return [_pad_cols(w[:, lo:lo + N_HEADS], LANES), _pad_cols(w[:, lo + N_HEADS:hi], LANES)]

    p = {}
    p["w_a"] = w_in[:, _A0:_B0]
    p["w_b"] = jnp.concatenate([w_in[:, _B0:_B0 + 4 * MIX_W]] + gate_tiles(w_in, _B0 + 4 * MIX_W, _C0), axis=1)
    p["w_b_lo"] = jnp.concatenate(gate_tiles(w_lo, _B0 + 4 * MIX_W, _C0), axis=1)
    p["w_c"] = w_in[:, _C0:_D0]
    p["w_c_lo"] = w_lo[:, _C0 + MIX_W:_C0 + 2 * MIX_W]
    p["w_d"] = jnp.concatenate([w_in[:, _D0:_D0 + 3 * MIX_W]] + gate_tiles(w_in, _D0 + 3 * MIX_W, _G0), axis=1)
    p["w_d_lo"] = jnp.concatenate(gate_tiles(w_lo, _D0 + 3 * MIX_W, _G0), axis=1)
    p["w_g"] = w_in[:, _G0:]
    p["lru_cw"] = P["w_lru_conv"][l]
    p["lru_cb"] = _row(P["b_lru_conv"][l])
    p["lru_wra"] = _block_diag(P["w_lru_ra"][l]).astype(BF16)
    p["lru_bra"] = _row(P["b_lru_ra"][l])
    p["lru_wix"] = _block_diag(P["w_lru_ix"][l]).astype(BF16)
    p["lru_bix"] = _row(P["b_lru_ix"][l])
    p["lru_lam"] = _row(P["lru_lambda"][l])
    p["gdn_cw"] = P["w_gdn_conv"][l]
    p["gdn_alog"] = _row(P["gdn_a_log"][l], LANES)
    p["gdn_dtb"] = _row(P["gdn_dt_bias"][l], LANES)
    p["gdn_ng"] = _row(jnp.tile(P["gdn_norm_g"][l], N_HEADS))
    p["hg_loglb"] = _row(jnp.log(lb[l]))
    p["hg_log1mlb"] = _row(jnp.log1p(-lb[l]))
    p["hg_onemlb"] = _row(1.0 - lb[l])
    p["hg_ng"] = _row(jnp.tile(P["hg_norm_g"][l], N_HEADS))
    p["ml_cw"] = P["w_ml_conv"][l]
    p["ml_cb"] = _row(P["b_ml_conv"][l])
    p["ml_wq"] = _block_diag(P["w_ml_q"][l]).astype(BF16)
    p["ml_wk"] = _block_diag(P["w_ml_k"][l]).astype(BF16)
    p["ml_bi"] = _row(P["b_ml_i"][l], LANES)
    p["ml_bf"] = _row(P["b_ml_f"][l], LANES)
    p["ml_ng"] = _row(P["ml_norm_g"][l])
    p["w_br"] = P["w_branch"][l].astype(BF16)
    p["w_out"] = P["w_out"][l].astype(BF16)
    p["ln1_g"] = _row(P["ln1_g"][l])
    p["ln1_b"] = _row(P["ln1_b"][l])
    p["ln2_g"] = _row(P["ln2_g"][l])
    p["ln2_b"] = _row(P["ln2_b"][l])
    p["w_r"] = _pad_cols(P["w_router"][l], LANES).astype(BF16)
    p["b_r"] = jnp.concatenate([P["b_router"][l].astype(F32), jnp.full((LANES - N_EXPERTS,), -1e30, F32)]).reshape(1, LANES)
    return p


def _mixers(x3, p, st, consts, bb, c, g, cs):
    ones_bd, expand = consts
    lru_conv, lru_h, gdn_conv, gdn_s, hg_st, ml_conv, ml_c, ml_n, ml_m = st
    y_a, lru_conv1, lru_h1 = _lru_call(x3, p["w_a"], p["lru_cw"], p["lru_cb"], p["lru_wra"], p["lru_bra"],
                                       p["lru_wix"], p["lru_bix"], p["lru_lam"], lru_conv, lru_h, bb, c)
    y_b, gdn_conv1, gdn_s1 = _gdn_call(x3, p["w_b"], p["w_b_lo"], p["gdn_cw"], p["gdn_alog"], p["gdn_dtb"], p["gdn_ng"],
                                       ones_bd, expand, gdn_conv, gdn_s, bb, c, g, cs)
    y_c, hg_st1 = _hgrn_call(x3, p["w_c"], p["w_c_lo"], p["hg_loglb"], p["hg_log1mlb"], p["hg_onemlb"], p["hg_ng"], ones_bd,
                             hg_st, bb, c)
    y_d, ml_conv1, ml_c1, ml_n1, ml_m1 = _mlstm_call(x3, p["w_d"], p["w_d_lo"], p["ml_cw"], p["ml_cb"], p["ml_wq"], p["ml_wk"],
                                                     p["ml_bi"], p["ml_bf"], p["ml_ng"], ones_bd, expand,
                                                     ml_conv, ml_c, ml_n, ml_m, bb, c, g, cs)
    return (y_a, y_b, y_c, y_d), (lru_conv1, lru_h1, gdn_conv1, gdn_s1, hg_st1, ml_conv1, ml_c1, ml_n1, ml_m1)


def _states_to_kernel(st):
    lru_conv, lru_h, gdn_conv, gdn_s, hg_s, ml_conv, ml_c, ml_n, ml_m = [s.astype(F32) for s in st]
    b = lru_h.shape[0]
    return (_tail8(lru_conv), _bcast8(lru_h), _tail8(gdn_conv), _bd_state(gdn_s),
            jnp.swapaxes(_bd_state(hg_s), 1, 2), _tail8(ml_conv), _bd_state(ml_c),
            _bcast8(ml_n.reshape(b, MIX_W)), _bcast8(_pad_cols(ml_m, LANES)))


def _states_from_kernel(st):
    lru_conv, lru_h, gdn_conv, gdn_s, hg_st, ml_conv, ml_c, ml_n, ml_m = st
    b = lru_h.shape[0]
    tail = SUBLANES - (CONV_W - 1)
    return (lru_conv[:, tail:], lru_h[:, SUBLANES - 1], gdn_conv[:, tail:], _bd_blocks(gdn_s),
            _bd_blocks(jnp.swapaxes(hg_st, 1, 2)), ml_conv[:, tail:], _bd_blocks(ml_c),
            ml_n[:, 0].reshape(b, N_HEADS, HEAD_DIM), ml_m[:, SUBLANES - 1, :N_HEADS])


def _route(top_e):
    t = top_e.shape[0]
    m = t * TOP_K
    flat_e = top_e.reshape(m)
    onehot = (flat_e[:, None] == jnp.arange(N_EXPERTS, dtype=jnp.int32)[None, :]).astype(jnp.int32)
    csum = jnp.cumsum(onehot, axis=0)
    rank = jnp.sum(onehot * csum, axis=1) - 1
    counts = csum[-1]
    padded = (counts + MOE_BLOCK_ROWS - 1) // MOE_BLOCK_ROWS * MOE_BLOCK_ROWS
    pad_ends = jnp.cumsum(padded)
    dest = (pad_ends - padded)[flat_e] + rank
    n_blocks = m // MOE_BLOCK_ROWS + N_EXPERTS
    mp = n_blocks * MOE_BLOCK_ROWS
    block_start = jnp.arange(n_blocks, dtype=jnp.int32) * MOE_BLOCK_ROWS
    block_e = jnp.minimum(jnp.sum((pad_ends[None, :] <= block_start[:, None]).astype(jnp.int32), axis=1),
                          N_EXPERTS - 1)
    n_used = (pad_ends[-1] // MOE_BLOCK_ROWS).astype(jnp.int32).reshape(1)
    n_pad = padded - counts
    j = jnp.arange(MOE_BLOCK_ROWS, dtype=jnp.int32)[None, :]
    is_pad = (j < n_pad[:, None]).reshape(-1)
    pad_row = ((pad_ends - n_pad)[:, None] + j).reshape(-1)
    tail_rank = jnp.cumsum(jnp.logical_not(is_pad).astype(jnp.int32)) - 1
    zero_rows = jnp.where(is_pad, pad_row, pad_ends[-1] + tail_rank).astype(jnp.int32)
    return dest.reshape(t, TOP_K), zero_rows, block_e, n_used, mp


def _dest_tiles(dest):
    t = dest.shape[0]
    n = t // COMBINE_ROWS
    return jnp.swapaxes(dest.reshape(n, COMBINE_ROWS, TOP_K), 1, 2).reshape(n, 1, TOP_K * COMBINE_ROWS)


def _trunk_layer(layer, xs, states, p, moe_w, consts, cfgs):
    x1s, tes, tgs, new_states = [], [], [], []
    for x3, st, (bb, c, g, cs, tm) in zip(xs, states, cfgs):
        b, l, _ = x3.shape
        ys, st1 = _mixers(x3, p, st, consts, bb, c, g, cs)
        x1, te, tg = _merge_call(x3.reshape(b * l, D_MODEL), [y.reshape(b * l, MIX_W) for y in ys],
                                 p["w_g"], p["w_br"], p["w_out"], p["ln1_g"], p["ln1_b"], p["w_r"], p["b_r"], tm)
        x1s.append(x1)
        tes.append(te[:, :TOP_K])
        tgs.append(tg)
        new_states.append(st1)
    dest, zero_rows, block_e, n_used, mp = _route(jnp.concatenate(tes, axis=0))
    d_tiles = []
    off = 0
    for x1 in x1s:
        d_tiles.append(_dest_tiles(lax.slice_in_dim(dest, off, off + x1.shape[0], axis=0)))
        off += x1.shape[0]
    z_tiles = zero_rows.reshape(-1, 1, TOP_K * COMBINE_ROWS)
    rows = _dispatch_call(x1s[0], x1s[1], jnp.concatenate(d_tiles + [z_tiles], axis=0), mp)
    ys = _moe_call(layer, block_e, n_used, rows, *moe_w)
    outs = []
    for x3, x1, tg, dt in zip(xs, x1s, tgs, d_tiles):
        outs.append(_combine_ln2_call(x1, tg, dt, ys, p["ln2_g"], p["ln2_b"]).reshape(x3.shape))
    return outs, new_states


def _row_tile(t):
    for tm in (512, 384, 256, 128, 64, 32, 16):
        if t % tm == 0:
            return tm
    raise ValueError(f"no row tile for {t} rows")


def _group_cfg(b, l):
    if l % PROMPT_CHUNK == 0:
        return (min(b, 8), PROMPT_CHUNK, 1, BD // N_HEADS, _row_tile(b * l))
    assert l == SUBLANES, "sequence length must be a multiple of the prompt chunk or one sublane tile"
    g = BD // (N_HEADS * SUBLANES)
    bb = min(b, 16)
    assert bb % g == 0 and b % bb == 0
    return (bb, SUBLANES, g, SUBLANES, _row_tile(b * l))


def _zero_states(b):
    return (jnp.zeros((b, CONV_W - 1, MIX_W), F32), jnp.zeros((b, MIX_W), F32),
            jnp.zeros((b, CONV_W - 1, 3 * MIX_W), F32), jnp.zeros((b, N_HEADS, HEAD_DIM, HEAD_DIM), F32),
            jnp.zeros((b, N_HEADS, HEAD_DIM, HEAD_DIM), F32), jnp.zeros((b, CONV_W - 1, MIX_W), F32),
            jnp.zeros((b, N_HEADS, HEAD_DIM, HEAD_DIM), F32), jnp.zeros((b, N_HEADS, HEAD_DIM), F32),
            jnp.zeros((b, N_HEADS), F32))


def kernel(x_prompt, x_sample, state_lru_conv, state_lru_h, state_gdn_conv, state_gdn, state_hgrn, state_mlstm_conv, state_mlstm_c, state_mlstm_n, state_mlstm_m, meta_tokens, ln_emb_g, ln_emb_b, hg_lb_logits, w_in, w_lru_conv, b_lru_conv, w_lru_ra, b_lru_ra, w_lru_ix, b_lru_ix, lru_lambda, w_gdn_conv, gdn_a_log, gdn_dt_bias, gdn_norm_g, hg_norm_g, w_ml_conv, b_ml_conv, w_ml_q, w_ml_k, b_ml_i, b_ml_f, ml_norm_g, w_branch, w_out, ln1_g, ln1_b, ln2_g, ln2_b, w_router, b_router, w_up, b_up, w_down, b_down):
    P = dict(w_in=w_in, w_lru_conv=w_lru_conv, b_lru_conv=b_lru_conv, w_lru_ra=w_lru_ra, b_lru_ra=b_lru_ra,
             w_lru_ix=w_lru_ix, b_lru_ix=b_lru_ix, lru_lambda=lru_lambda, w_gdn_conv=w_gdn_conv,
             gdn_a_log=gdn_a_log, gdn_dt_bias=gdn_dt_bias, gdn_norm_g=gdn_norm_g, hg_norm_g=hg_norm_g,
             w_ml_conv=w_ml_conv, b_ml_conv=b_ml_conv, w_ml_q=w_ml_q, w_ml_k=w_ml_k, b_ml_i=b_ml_i,
             b_ml_f=b_ml_f, ml_norm_g=ml_norm_g, w_branch=w_branch, w_out=w_out, ln1_g=ln1_g, ln1_b=ln1_b,
             ln2_g=ln2_g, ln2_b=ln2_b, w_router=w_router, b_router=b_router, w_up=w_up, b_up=b_up,
             w_down=w_down, b_down=b_down)
    depth = w_in.shape[0]
    lb_cum = jnp.cumsum(jax.nn.softmax(hg_lb_logits.astype(F32), axis=0), axis=0)
    lb = lb_cum - lb_cum[0:1]

    lane = jnp.arange(MIX_W) // HEAD_DIM
    ones_bd = (lane[:, None] == lane[None, :]).astype(BF16)
    expand = (jnp.arange(LANES)[:, None] == lane[None, :]).astype(BF16)
    consts = (ones_bd, expand)
    moe_w = (w_up.astype(BF16), b_up.reshape(depth, N_EXPERTS, 1, 2 * D_FF).astype(F32),
             w_down.astype(BF16), b_down.reshape(depth, N_EXPERTS, 1, D_MODEL).astype(F32))

    bp, lp0, _ = x_prompt.shape
    bs, ls, _ = x_sample.shape
    meta = jnp.broadcast_to(meta_tokens.astype(F32)[None], (bp, N_META, D_MODEL))
    xp_in = jnp.concatenate([meta, x_prompt], axis=1)
    lp = lp0 + N_META
    cfgs = [_group_cfg(bp, lp), _group_cfg(bs, ls)]
    g_emb, b_emb = _row(ln_emb_g), _row(ln_emb_b)
    xp = _ln_call(xp_in.reshape(bp * lp, D_MODEL), g_emb, b_emb, cfgs[0][4]).reshape(bp, lp, D_MODEL)
    xs_ = _ln_call(x_sample.reshape(bs * ls, D_MODEL), g_emb, b_emb, cfgs[1][4]).reshape(bs, ls, D_MODEL)

    sample_states = (state_lru_conv, state_lru_h, state_gdn_conv, state_gdn, state_hgrn,
                     state_mlstm_conv, state_mlstm_c, state_mlstm_n, state_mlstm_m)
    zero_p = _zero_states(bp)
    xs = [xp, xs_]
    collected = ([], [])
    for l in range(depth):
        p = _layer_params(P, lb, l)
        states = [_states_to_kernel(zero_p), _states_to_kernel(tuple(s[l] for s in sample_states))]
        xs, new_states = _trunk_layer(l, xs, states, p, moe_w, consts, cfgs)
        for grp in range(2):
            collected[grp].append(_states_from_kernel(new_states[grp]))
    new_p = tuple(jnp.stack([layer[i] for layer in collected[0]]) for i in range(9))
    new_s = tuple(jnp.stack([layer[i] for layer in collected[1]]) for i in range(9))
    y_prompt = xs[0][:, N_META:]
    y_sample = xs[1]
    out = [y_prompt, y_sample]
    for i in range(9):
        out.append(new_p[i])
        out.append(new_s[i])
    return tuple(out)
```

```python
import functools

import jax
import jax.numpy as jnp
from jax import lax
from jax.experimental import pallas as pl
from jax.experimental.pallas import tpu as pltpu

F32 = jnp.float32
BF16 = jnp.bfloat16
HIGHEST = lax.Precision.HIGHEST

D_MODEL = 1024
DEPTH = 4
N_META = 16
N_BRANCH = 4
MIX_W = 256
N_HEADS = 4
HEAD_DIM = 64
CONV_W = 4
LRU_C = 8.0
N_EXPERTS = 32
TOP_K = 4
D_FF = 1024
SWIGLU_LIMIT = 7.0
SWIGLU_ALPHA = 1.702
LN_EPS = 1e-5
NORM_EPS = 1e-6
DEEPNORM_ALPHA = (2 * DEPTH) ** 0.25

LANES = 128
SUBLANES = 8
BD = 256
VMEM_LIMIT_BYTES = 56 * 1024 * 1024
PROMPT_CHUNK = 48
MOE_BLOCK_ROWS = 256
COMBINE_ROWS = 128

_A0, _B0, _C0, _D0, _G0 = 0, 512, 1544, 2568, 3344

_NN = (((1,), (0,)), ((), ()))
_NT = (((1,), (1,)), ((), ()))
_TN = (((0,), (0,)), ((), ()))


def _mm(a, b, dims=_NN, precision=None):
    return lax.dot_general(a, b, dims, precision=precision, preferred_element_type=F32)


def _bmm(a, b, dims=_NN):
    return _mm(a.astype(BF16), b.astype(BF16), dims)


def _split_bf16(x):
    hi = x.astype(BF16)
    return hi, (x - hi.astype(F32)).astype(BF16)


def _mm3(a, b, dims=_NN):
    ah, al = _split_bf16(a)
    bh, bl = _split_bf16(b)
    return _mm(ah, bh, dims) + (_mm(al, bh, dims) + _mm(ah, bl, dims))


def _mm_sel(x, sel, dims=_NN, terms=2):
    out = None
    r = x
    for _ in range(terms):
        p = r.astype(BF16)
        t = _mm(p, sel, dims)
        out = t if out is None else out + t
        r = r - p.astype(F32)
    return out


def _sigmoid(x):
    return jax.nn.sigmoid(x)


def _softplus(x):
    return jnp.maximum(x, 0.0) + jnp.log1p(jnp.exp(-jnp.abs(x)))


def _log_sigmoid(x):
    return jnp.minimum(x, 0.0) - jnp.log1p(jnp.exp(-jnp.abs(x)))


def _expm1(x):
    return jnp.tanh(0.5 * x) * (jnp.exp(x) + 1.0)


def _silu(x):
    return x * _sigmoid(x)


def _gelu_tanh(x):
    return 0.5 * x * (1.0 + jnp.tanh(0.7978845608028654 * (x + 0.044715 * (x * x * x))))


def _layernorm(x, g, b):
    mu = jnp.mean(x, axis=-1, keepdims=True)
    xc = x - mu
    var = jnp.mean(xc * xc, axis=-1, keepdims=True)
    return xc * lax.rsqrt(var + LN_EPS) * g + b


def _cparams(*sem):
    return pltpu.CompilerParams(dimension_semantics=sem, vmem_limit_bytes=VMEM_LIMIT_BYTES)


def _full(a):
    nd = a.ndim
    return pl.BlockSpec(a.shape, lambda *_: (0,) * nd)


def _ln_kernel(x_ref, g_ref, b_ref, o_ref):
    o_ref[...] = _layernorm(x_ref[...], g_ref[...], b_ref[...])


def _ln_call(x2, g, b, tm):
    t, d = x2.shape
    return pl.pallas_call(
        _ln_kernel,
        out_shape=jax.ShapeDtypeStruct((t, d), F32),
        grid=(t // tm,),
        in_specs=[pl.BlockSpec((tm, d), lambda i: (i, 0)), _full(g), _full(b)],
        out_specs=pl.BlockSpec((tm, d), lambda i: (i, 0)),
        compiler_params=_cparams("parallel"),
        name="ln_rows",
    )(x2, g, b)


def _combine_ln2_kernel(n, dcur_ref, dnxt_ref, x_ref, gate_ref, g_ref, b_ref, ys_hbm, o_ref, buf_ref, sem_ref):
    i = pl.program_id(0)
    tm = x_ref.shape[0]
    nrow = TOP_K * tm

    def row_copy(d_ref, r, slot):
        return pltpu.make_async_copy(ys_hbm.at[pl.ds(d_ref[0, 0, r], 1)], buf_ref.at[slot, pl.ds(r, 1)],
                                     sem_ref.at[slot])

    def issue(d_ref, slot):
        def body(r2, carry):
            row_copy(d_ref, 2 * r2, slot).start(priority=0)
            row_copy(d_ref, 2 * r2 + 1, slot).start(priority=1)
            return carry

        lax.fori_loop(0, nrow // 2, body, 0, unroll=4)

    @pl.when(i == 0)
    def _():
        issue(dcur_ref, 0)

    @pl.when(i + 1 < n)
    def _():
        issue(dnxt_ref, (i + 1) % 2)

    slot = i % 2
    pltpu.make_async_copy(ys_hbm.at[pl.ds(0, nrow)], buf_ref.at[slot], sem_ref.at[slot]).wait()
    gate = gate_ref[...]
    ff = None
    for k in range(TOP_K):
        term = buf_ref[slot, pl.ds(k * tm, tm), :] * gate[:, k:k + 1]
        ff = term if ff is None else ff + term
    o_ref[...] = _layernorm(DEEPNORM_ALPHA * x_ref[...] + ff, g_ref[...], b_ref[...])


def _combine_ln2_call(x2, gates, dest_tiles, ys, g, b):
    t, d = x2.shape
    tm = COMBINE_ROWS
    n = t // tm
    row = pl.BlockSpec((tm, d), lambda i: (i, 0))
    dspec = lambda f: pl.BlockSpec((1, 1, TOP_K * tm), f, memory_space=pltpu.SMEM)
    return pl.pallas_call(
        functools.partial(_combine_ln2_kernel, n),
        out_shape=jax.ShapeDtypeStruct((t, d), F32),
        grid=(n,),
        in_specs=[dspec(lambda i: (i, 0, 0)), dspec(lambda i: (jnp.minimum(i + 1, n - 1), 0, 0)), row,
                  pl.BlockSpec((tm, LANES), lambda i: (i, 0)), _full(g), _full(b),
                  pl.BlockSpec(memory_space=pl.ANY)],
        out_specs=row,
        scratch_shapes=[pltpu.VMEM((2, TOP_K * tm, d), F32), pltpu.SemaphoreType.DMA((2,))],
        compiler_params=_cparams("arbitrary"),
        name="combine_ln2",
    )(dest_tiles, dest_tiles, x2, gates, g, b, ys)


def _merge_kernel(x_ref, ya_ref, yb_ref, yc_ref, yd_ref, wg_ref, wbr_ref, wout_ref, g_ref, b_ref, wr_ref, br_ref,
                  x1_ref, te_ref, tg_ref):
    x = x_ref[...]
    xb = x.astype(BF16)
    merged = None
    for n, y_ref in enumerate((ya_ref, yb_ref, yc_ref, yd_ref)):
        up = _mm(y_ref[...].astype(BF16), wbr_ref[n])
        gate = _sigmoid(_mm(xb, wg_ref[:, n * D_MODEL:(n + 1) * D_MODEL]))
        merged = up * gate if merged is None else merged + up * gate
    out = _mm(merged.astype(BF16), wout_ref[...])
    x1 = _layernorm(DEEPNORM_ALPHA * x + out, g_ref[...], b_ref[...])
    x1_ref[...] = x1
    logits = _mm(x1.astype(BF16), wr_ref[...]) + br_ref[...]
    lane = lax.broadcasted_iota(jnp.int32, logits.shape, 1)
    vals = []
    idxs = []
    v = logits
    for _ in range(TOP_K):
        m = jnp.max(v, axis=-1, keepdims=True)
        idx = jnp.min(jnp.where(v == m, lane, LANES), axis=-1, keepdims=True)
        vals.append(m)
        idxs.append(idx)
        v = jnp.where(lane == idx, -jnp.inf, v)
    es = [jnp.exp(m - vals[0]) for m in vals]
    den = es[0] + es[1] + es[2] + es[3]
    te = jnp.zeros(logits.shape, jnp.int32)
    tg = jnp.zeros(logits.shape, F32)
    for j in range(TOP_K):
        te = jnp.where(lane == j, idxs[j], te)
        tg = jnp.where(lane == j, es[j] / den, tg)
    te_ref[...] = te
    tg_ref[...] = tg


def _merge_call(x2, ys, wg, wbr, wout, g, b, wr, br, tm):
    t, d = x2.shape
    row = pl.BlockSpec((tm, d), lambda i: (i, 0))
    yrow = pl.BlockSpec((tm, MIX_W), lambda i: (i, 0))
    lrow = pl.BlockSpec((tm, LANES), lambda i: (i, 0))
    return pl.pallas_call(
        _merge_kernel,
        out_shape=(jax.ShapeDtypeStruct((t, d), F32),
                   jax.ShapeDtypeStruct((t, LANES), jnp.int32), jax.ShapeDtypeStruct((t, LANES), F32)),
        grid=(t // tm,),
        in_specs=[row, yrow, yrow, yrow, yrow, _full(wg), _full(wbr), _full(wout), _full(g), _full(b),
                  _full(wr), _full(br)],
        out_specs=(row, lrow, lrow),
        compiler_params=_cparams("parallel"),
        name="merge_ln_router",
    )(x2, *ys, wg, wbr, wout, g, b, wr, br)


def _dispatch_kernel(n_p, n_s, n, d_ref, xp_ref, xs_ref, o_hbm, buf_ref, sem_ref):
    i = pl.program_id(0)
    tm = xp_ref.shape[0]
    nrow = TOP_K * tm
    slot = i % 2

    def wait_slot(s):
        pltpu.make_async_copy(o_hbm.at[pl.ds(0, nrow)], o_hbm.at[pl.ds(0, nrow)], sem_ref.at[s]).wait()

    @pl.when(i >= 2)
    def _():
        wait_slot(slot)

    @pl.when(i < n_p)
    def _():
        buf_ref[slot] = xp_ref[...]

    @pl.when((i >= n_p) & (i < n_p + n_s))
    def _():
        buf_ref[slot] = xs_ref[...]

    @pl.when(i >= n_p + n_s)
    def _():
        buf_ref[slot] = jnp.zeros((tm, D_MODEL), F32)

    def row_copy(k, t):
        return pltpu.make_async_copy(buf_ref.at[slot, pl.ds(t, 1)], o_hbm.at[pl.ds(d_ref[0, 0, k * tm + t], 1)],
                                     sem_ref.at[slot])

    for k in range(TOP_K):
        def body(t2, carry, k=k):
            row_copy(k, 2 * t2).start(priority=0)
            row_copy(k, 2 * t2 + 1).start(priority=1)
            return carry

        lax.fori_loop(0, tm // 2, body, 0, unroll=4)

    @pl.when(i == n - 1)
    def _():
        if n >= 2:
            wait_slot(1 - slot)
        wait_slot(slot)


def _dispatch_call(x_p, x_s, dest_tiles, mp):
    tm = COMBINE_ROWS
    d = x_p.shape[1]
    n_p, n_s, n = x_p.shape[0] // tm, x_s.shape[0] // tm, dest_tiles.shape[0]
    return pl.pallas_call(
        functools.partial(_dispatch_kernel, n_p, n_s, n),
        out_shape=jax.ShapeDtypeStruct((mp, d), F32),
        grid=(n,),
        in_specs=[pl.BlockSpec((1, 1, TOP_K * tm), lambda i: (i, 0, 0), memory_space=pltpu.SMEM),
                  pl.BlockSpec((tm, d), lambda i: (jnp.minimum(i, n_p - 1), 0)),
                  pl.BlockSpec((tm, d), lambda i: (jnp.clip(i - n_p, 0, n_s - 1), 0))],
        out_specs=pl.BlockSpec(memory_space=pl.ANY),
        scratch_shapes=[pltpu.VMEM((2, tm, d), F32), pltpu.SemaphoreType.DMA((2,))],
        compiler_params=_cparams("arbitrary"),
        name="moe_dispatch",
    )(dest_tiles, x_p, x_s)


def _moe_kernel(be_ref, nused_ref, xs_ref, wu_ref, bu_ref, wd_ref, bd_ref, o_ref):
    i = pl.program_id(0)

    @pl.when(i < nused_ref[0])
    def _():
        h = _mm(xs_ref[...].astype(BF16), wu_ref[0, 0]) + bu_ref[0, 0]
        g = jnp.minimum(h[:, :D_FF], SWIGLU_LIMIT)
        u = jnp.clip(h[:, D_FF:], -SWIGLU_LIMIT, SWIGLU_LIMIT)
        act = (u + 1.0) * g * _sigmoid(SWIGLU_ALPHA * g)
        o_ref[...] = _mm(act.astype(BF16), wd_ref[0, 0]) + bd_ref[0, 0]

    @pl.when(i >= nused_ref[0])
    def _():
        o_ref[...] = jnp.zeros(o_ref.shape, F32)


def _moe_call(layer, block_e, n_used, xs, wu, bu, wd, bd):
    mp, d = xs.shape
    nb = mp // MOE_BLOCK_ROWS
    grid_spec = pltpu.PrefetchScalarGridSpec(
        num_scalar_prefetch=2,
        grid=(nb,),
        in_specs=[
            pl.BlockSpec((MOE_BLOCK_ROWS, d), lambda i, be, nu: (i, 0)),
            pl.BlockSpec((1, 1, d, 2 * D_FF), lambda i, be, nu: (layer, be[i], 0, 0)),
            pl.BlockSpec((1, 1, 1, 2 * D_FF), lambda i, be, nu: (layer, be[i], 0, 0)),
            pl.BlockSpec((1, 1, D_FF, d), lambda i, be, nu: (layer, be[i], 0, 0)),
            pl.BlockSpec((1, 1, 1, d), lambda i, be, nu: (layer, be[i], 0, 0)),
        ],
        out_specs=pl.BlockSpec((MOE_BLOCK_ROWS, d), lambda i, be, nu: (i, 0)),
    )
    return pl.pallas_call(
        _moe_kernel,
        out_shape=jax.ShapeDtypeStruct((mp, d), F32),
        grid_spec=grid_spec,
        compiler_params=_cparams("arbitrary"),
        name="moe_experts",
    )(block_e, n_used, xs, wu, bu, wd, bd)


def _chunk_pos(bb, c, width):
    return lax.broadcasted_iota(jnp.int32, (bb, c, width), 1).reshape(bb * c, width)


def _seg_scan(x, pos, c, op, ident):
    s = 1
    while s < c:
        x = op(x, jnp.where(pos >= s, pltpu.roll(x, s, 0), ident))
        s *= 2
    return x


def _chunk_last(x, pos, bb, c):
    w = x.shape[1]
    x3 = jnp.where(pos == c - 1, x, 0.0).reshape(bb, c, w)
    return jnp.broadcast_to(jnp.sum(x3, axis=1, keepdims=True), (bb, c, w)).reshape(bb * c, w)


def _causal_conv(x3, ext_ref, prev_ref, cw_ref, c):
    ext_ref[:, 0:SUBLANES, :] = prev_ref[...]
    ext_ref[:, SUBLANES:SUBLANES + c, :] = x3
    y = None
    for j in range(CONV_W):
        lo = SUBLANES - (CONV_W - 1) + j
        tap = ext_ref[:, lo:lo + c, :] * cw_ref[j:j + 1, :]
        y = tap if y is None else y + tap
    prev_ref[...] = ext_ref[:, c:c + SUBLANES, :]
    return y


def _head_masks(width=MIX_W):
    lane = lax.broadcasted_iota(jnp.int32, (1, width), 1)
    return [(lane >= h * HEAD_DIM) & (lane < (h + 1) * HEAD_DIM) for h in range(N_HEADS)]


def _lane_masks():
    lane = lax.broadcasted_iota(jnp.int32, (1, LANES), 1)
    return [lane == h for h in range(N_HEADS)]


def _stack(xg, masks, g, c, cs):
    pieces = []
    for b in range(g):
        xb = xg[b * c:(b + 1) * c]
        for m in masks:
            pieces.append(jnp.where(m, xb, 0.0))
            if cs > c:
                pieces.append(jnp.zeros((cs - c, xg.shape[1]), xg.dtype))
    return jnp.concatenate(pieces, axis=0)


def _unstack(y, g, c, cs):
    outs = []
    for b in range(g):
        acc = None
        for h in range(N_HEADS):
            r0 = (b * N_HEADS + h) * cs
            piece = y[r0:r0 + c]
            acc = piece if acc is None else acc + piece
        outs.append(acc)
    return outs[0] if g == 1 else jnp.concatenate(outs, axis=0)


def _bd_masks(cs):
    r = lax.broadcasted_iota(jnp.int32, (BD, BD), 0)
    q = lax.broadcasted_iota(jnp.int32, (BD, BD), 1)
    shift = cs.bit_length() - 1
    same = (r >> shift) == (q >> shift)
    tr = r & (cs - 1)
    tq = q & (cs - 1)
    return same & (tq <= tr), same & (tq < tr)


def _head_block_mask():
    r = lax.broadcasted_iota(jnp.int32, (MIX_W, MIX_W), 0)
    q = lax.broadcasted_iota(jnp.int32, (MIX_W, MIX_W), 1)
    return (r // HEAD_DIM) == (q // HEAD_DIM)


def _head_sum(x, ones_bd):
    return _mm_sel(x, ones_bd)


def _to_hb(cols, expand):
    return _mm_sel(cols, expand, terms=2)


def _stack_cols(cols_g, g, c, cs):
    st = _stack(cols_g, _lane_masks(), g, c, cs)
    col = jnp.sum(st, axis=1, keepdims=True)
    ones = jnp.ones((SUBLANES, LANES), F32)
    row = _mm(ones, st, _NT, precision=HIGHEST)[0:1, :]
    return col, row


def _group_rows(ref, gi, g, c):
    w = ref.shape[2]
    if g == 1:
        return ref[gi]
    return ref[pl.ds(gi * g, g)].reshape(g * c, w)


def _paired_loop(n, compute_many, commit, width=2):
    while n % width:
        width //= 2

    def trip(j, carry):
        idx = [width * j + w for w in range(width)]
        for i, out in zip(idx, compute_many(idx)):
            commit(i, out)
        return carry

    lax.fori_loop(0, n // width, trip, 0)


def _seq_specs(bb, c):
    x_spec = pl.BlockSpec((bb, c, D_MODEL), lambda i, k: (i, k, 0))
    y_spec = pl.BlockSpec((bb, c, MIX_W), lambda i, k: (i, k, 0))
    return x_spec, y_spec


def _state_spec(bb, *tail):
    nt = len(tail)
    return pl.BlockSpec((bb,) + tuple(tail), lambda i, k: (i,) + (0,) * nt)


def _lru_kernel(x_ref, w_ref, cw_ref, cb_ref, wra_ref, bra_ref, wix_ref, bix_ref, lam_ref, conv0_ref, h0_ref,
                y_ref, conv1_ref, h1_ref, ext_ref, prev_ref, hst_ref):
    k = pl.program_id(1)
    bb, c, _ = x_ref.shape
    n = bb * c

    @pl.when(k == 0)
    def _():
        prev_ref[...] = conv0_ref[...]
        hst_ref[...] = h0_ref[...]

    xb = x_ref[...].reshape(n, D_MODEL).astype(BF16)
    proj = _mm(xb, w_ref[...])
    a_gate = proj[:, MIX_W:]
    xa = _causal_conv(proj[:, :MIX_W].reshape(bb, c, MIX_W), ext_ref, prev_ref, cw_ref, c) + cb_ref[...]
    xa = xa.reshape(n, MIX_W)
    xab = xa.astype(BF16)
    r = _sigmoid(_mm(xab, wra_ref[...]) + bra_ref[...])
    i = _sigmoid(_mm(xab, wix_ref[...]) + bix_ref[...])
    log_a = -LRU_C * r * _softplus(-lam_ref[...])
    a_cum = jnp.exp(log_a)
    b_cum = jnp.sqrt(-_expm1(2.0 * log_a)) * (i * xa)
    pos = _chunk_pos(bb, c, MIX_W)
    s = 1
    while s < c:
        keep = pos >= s
        a_sh = pltpu.roll(a_cum, s, 0)
        b_sh = pltpu.roll(b_cum, s, 0)
        b_cum = jnp.where(keep, a_cum * b_sh + b_cum, b_cum)
        a_cum = jnp.where(keep, a_cum * a_sh, a_cum)
        s *= 2
    h0 = jnp.broadcast_to(hst_ref[:, SUBLANES - 1:SUBLANES, :], (bb, c, MIX_W)).reshape(n, MIX_W)
    h = a_cum * h0 + b_cum
    y_ref[...] = (h * _gelu_tanh(a_gate)).reshape(bb, c, MIX_W)
    hst_ref[...] = h.reshape(bb, c, MIX_W)[:, c - SUBLANES:c, :]

    @pl.when(k == pl.num_programs(1) - 1)
    def _():
        conv1_ref[...] = prev_ref[...]
        h1_ref[...] = hst_ref[...]


def _lru_call(x3, w, cw, cb, wra, bra, wix, bix, lam, conv0, h0, bb, c):
    b, l, _ = x3.shape
    x_spec, y_spec = _seq_specs(bb, c)
    st = _state_spec(bb, SUBLANES, MIX_W)
    consts = (w, cw, cb, wra, bra, wix, bix, lam)
    return pl.pallas_call(
        _lru_kernel,
        out_shape=(jax.ShapeDtypeStruct((b, l, MIX_W), F32), jax.ShapeDtypeStruct((b, SUBLANES, MIX_W), F32),
                   jax.ShapeDtypeStruct((b, SUBLANES, MIX_W), F32)),
        grid=(b // bb, l // c),
        in_specs=[x_spec] + [_full(a) for a in consts] + [st, st],
        out_specs=(y_spec, st, st),
        scratch_shapes=[pltpu.VMEM((bb, c + SUBLANES, MIX_W), F32), pltpu.VMEM((bb, SUBLANES, MIX_W), F32),
                        pltpu.VMEM((bb, SUBLANES, MIX_W), F32)],
        compiler_params=_cparams("parallel", "arbitrary"),
        name="mixer_rglru",
    )(x3, *consts, conv0, h0)


def _hgrn_kernel(x_ref, w_ref, wlo_ref, loglb_ref, log1mlb_ref, onemlb_ref, ng_ref, ones_ref, st0_ref,
                 y_ref, st1_ref, st_ref, q_s, k_s, i_s, b_s, qe_s, kd_s, el_s, o_s):
    kk = pl.program_id(1)
    bb, c, _ = x_ref.shape
    n = bb * c
    nblk = c // SUBLANES

    @pl.when(kk == 0)
    def _():
        st_ref[...] = st0_ref[...]

    xb, xlo = _split_bf16(x_ref[...].reshape(n, D_MODEL))
    proj = _mm(xb, w_ref[...])
    q = proj[:, 0:MIX_W] * HEAD_DIM ** -0.5
    z = proj[:, MIX_W:2 * MIX_W] + _mm(xlo, w_ref[:, MIX_W:2 * MIX_W]) + _mm(xb, wlo_ref[...])
    iv = proj[:, 2 * MIX_W:3 * MIX_W]
    cg = proj[:, 3 * MIX_W:4 * MIX_W]
    lo = loglb_ref[...]
    hi = log1mlb_ref[...] + _log_sigmoid(z)
    log_f = jnp.maximum(lo, hi) + jnp.log1p(jnp.exp(-jnp.abs(lo - hi)))
    kc = onemlb_ref[...] * _sigmoid(-z)
    pos = _chunk_pos(bb, c, MIX_W)
    bc = _seg_scan(log_f, pos, c, jnp.add, 0.0)
    b_last = _chunk_last(bc, pos, bb, c)
    ones_bd = ones_ref[...]

    sub = pos & (SUBLANES - 1)
    o_band = None
    for d in range(SUBLANES):
        kr = pltpu.roll(kc, d, 0) if d else kc
        br = pltpu.roll(bc, d, 0) if d else bc
        ir = pltpu.roll(iv, d, 0) if d else iv
        wd = jnp.where(sub >= d, q * kr * jnp.exp(jnp.minimum(bc - br, 0.0)), 0.0)
        term = _head_sum(wd, ones_bd) * ir
        o_band = term if o_band is None else o_band + term

    q_s[...] = q.reshape(bb, c, MIX_W)
    k_s[...] = kc.reshape(bb, c, MIX_W)
    i_s[...] = iv.reshape(bb, c, MIX_W)
    b_s[...] = bc.reshape(bb, c, MIX_W)
    qe_s[...] = (q * jnp.exp(bc)).reshape(bb, c, MIX_W)
    kd_s[...] = (kc * jnp.exp(b_last - bc)).reshape(bb, c, MIX_W)
    el_s[...] = jnp.exp(b_last).reshape(bb, c, MIX_W)

    hmasks = _head_masks()
    blockmask = _head_block_mask()
    pad_rows = LANES - c
    rowid = lax.broadcasted_iota(jnp.int32, (LANES, MIX_W), 0)
    zpad = jnp.zeros((pad_rows, MIX_W), F32)

    def per_bs(bs):
        nb = range(len(bs))
        sts = [st_ref[b] for b in bs]
        os_ = [_bmm(qe_s[b], sts[j], _NT) for j, b in enumerate(bs)]
        if nblk > 1:
            kp = [jnp.concatenate([k_s[b], zpad], axis=0) for b in bs]
            bp = [jnp.concatenate([b_s[b], zpad], axis=0) for b in bs]
            ip = [jnp.concatenate([i_s[b], zpad], axis=0).astype(BF16) for b in bs]
            pieces = [[jnp.zeros((SUBLANES, MIX_W), F32)] for _ in bs]
            for blk in range(1, nblk):
                r0 = blk * SUBLANES
                atts = []
                for j, b in enumerate(bs):
                    ref_b = b_s[b, r0 - 1:r0, :]
                    qi = q_s[b, r0:r0 + SUBLANES, :] * jnp.exp(
                        jnp.minimum(b_s[b, r0:r0 + SUBLANES, :] - ref_b, 0.0))
                    ki = jnp.where(rowid < r0, kp[j] * jnp.exp(jnp.minimum(ref_b - bp[j], 0.0)), 0.0)
                    qst = jnp.concatenate([jnp.where(m, qi, 0.0) for m in hmasks], axis=0)
                    atts.append(_bmm(qst, ki, _NT))
                ress = [_mm(atts[j].astype(BF16), ip[j]) for j in nb]
                for j in nb:
                    acc = None
                    for h in range(N_HEADS):
                        part = jnp.where(hmasks[h], ress[j][h * SUBLANES:(h + 1) * SUBLANES], 0.0)
                        acc = part if acc is None else acc + part
                    pieces[j].append(acc)
            os_ = [os_[j] + jnp.concatenate(pieces[j], axis=0) for j in nb]
        upds = [_mm3(jnp.concatenate([i_s[b], zpad], axis=0), jnp.concatenate([kd_s[b], zpad], axis=0), _TN)
                for b in bs]
        return [(os_[j], sts[j] * el_s[b, 0:1, :] + jnp.where(blockmask, upds[j], 0.0)) for j, b in enumerate(bs)]

    def commit(b, outs):
        o_s[b] = outs[0]
        st_ref[b] = outs[1]

    _paired_loop(bb, per_bs, commit, width=8)

    o = o_s[...].reshape(n, MIX_W) + o_band
    ms = _head_sum(o * o, ones_bd) * (1.0 / HEAD_DIM)
    y = o * lax.rsqrt(ms + NORM_EPS) * ng_ref[...] * _silu(cg)
    y_ref[...] = y.reshape(bb, c, MIX_W)

    @pl.when(kk == pl.num_programs(1) - 1)
    def _():
        st1_ref[...] = st_ref[...]


def _hgrn_call(x3, w, wlo, loglb, log1mlb, onemlb, ng, ones_bd, st0, bb, c):
    b, l, _ = x3.shape
    x_spec, y_spec = _seq_specs(bb, c)
    st = _state_spec(bb, MIX_W, MIX_W)
    consts = (w, wlo, loglb, log1mlb, onemlb, ng, ones_bd)
    rows = pltpu.VMEM((bb, c, MIX_W), F32)
    return pl.pallas_call(
        _hgrn_kernel,
        out_shape=(jax.ShapeDtypeStruct((b, l, MIX_W), F32), jax.ShapeDtypeStruct((b, MIX_W, MIX_W), F32)),
        grid=(b // bb, l // c),
        in_specs=[x_spec] + [_full(a) for a in consts] + [st],
        out_specs=(y_spec, st),
        scratch_shapes=[pltpu.VMEM((bb, MIX_W, MIX_W), F32)] + [rows] * 8,
        compiler_params=_cparams("parallel", "arbitrary"),
        name="mixer_hgrn2",
    )(x3, *consts, st0)


def _mlstm_kernel(g, cs, x_ref, w_ref, wlo_ref, cw_ref, cb_ref, wq_ref, wk_ref, bi_ref, bf_ref, ng_ref, ones_ref, exp_ref,
                  conv0_ref, c0_ref, n0_ref, m0_ref,
                  y_ref, conv1_ref, c1_ref, n1_ref, m1_ref,
                  ext_ref, prev_ref, cst_ref, nst_ref, mst_ref, q_s, k_s, v_s, kw_s, dec_s, a_s, g_s, num_s, den_s,
                  qc_s):
    kk = pl.program_id(1)
    bb, c, _ = x_ref.shape
    n = bb * c
    ngroups = bb // g

    @pl.when(kk == 0)
    def _():
        prev_ref[...] = conv0_ref[...]
        cst_ref[...] = c0_ref[...]
        nst_ref[...] = n0_ref[...]
        mst_ref[...] = m0_ref[...]

    xb, xlo = _split_bf16(x_ref[...].reshape(n, D_MODEL))
    proj = _mm(xb, w_ref[...])
    gates = proj[:, 3 * MIX_W:] + _mm(xlo, w_ref[:, 3 * MIX_W:]) + _mm(xb, wlo_ref[...])
    xm = _causal_conv(proj[:, 0:MIX_W].reshape(bb, c, MIX_W), ext_ref, prev_ref, cw_ref, c) + cb_ref[...]
    xm = _silu(xm).reshape(n, MIX_W).astype(BF16)
    q = _mm(xm, wq_ref[...])
    k = _mm(xm, wk_ref[...]) * HEAD_DIM ** -0.5
    v = proj[:, MIX_W:2 * MIX_W]
    d_o = proj[:, 2 * MIX_W:3 * MIX_W]
    log_i = gates[:, 0:LANES] + bi_ref[...]
    log_f = _log_sigmoid(gates[:, LANES:2 * LANES] + bf_ref[...])
    pos = _chunk_pos(bb, c, LANES)
    bc = _seg_scan(log_f, pos, c, jnp.add, 0.0)
    gg = log_i - bc
    cm = _seg_scan(gg, pos, c, jnp.maximum, -jnp.inf)
    m0 = jnp.broadcast_to(mst_ref[:, SUBLANES - 1:SUBLANES, :], (bb, c, LANES)).reshape(n, LANES)
    m_t = jnp.maximum(m0 + bc, bc + cm)
    b_last = _chunk_last(bc, pos, bb, c)
    m_last = _chunk_last(m_t, pos, bb, c)
    expand = exp_ref[...]
    ones_bd = ones_ref[...]
    inter = _to_hb(jnp.exp(m0 + bc - m_t), expand)
    e_negm = _to_hb(jnp.exp(-m_t), expand)
    wl = _to_hb(jnp.exp(gg + b_last - m_last), expand)
    nrows = jnp.broadcast_to(nst_ref[:, 0:1, :], (bb, c, MIX_W)).reshape(n, MIX_W)
    qn = _head_sum(q * nrows, ones_bd)

    q_s[...] = q.reshape(bb, c, MIX_W)
    k_s[...] = k.reshape(bb, c, MIX_W)
    v_s[...] = v.reshape(bb, c, MIX_W)
    kw_s[...] = (k * wl).reshape(bb, c, MIX_W)
    dec_s[...] = _to_hb(jnp.exp(m0 + b_last - m_last), expand).reshape(bb, c, MIX_W)
    a_s[...] = (bc - m_t).reshape(bb, c, LANES)
    g_s[...] = gg.reshape(bb, c, LANES)

    hmasks = _head_masks()
    causal, _ = _bd_masks(cs)
    rowb = lax.broadcasted_iota(jnp.int32, (BD, MIX_W), 0) // (N_HEADS * cs)
    sel = _stack(jnp.ones((g * c, MIX_W), F32), hmasks, g, c, cs)

    def per_groups(gis):
        ng = range(len(gis))
        qg = [_group_rows(q_s, gi, g, c) for gi in gis]
        qst = [_stack(qg[j], hmasks, g, c, cs).astype(BF16) for j in ng]
        kst = [_stack(_group_rows(k_s, gi, g, c), hmasks, g, c, cs).astype(BF16) for gi in gis]
        vst = [_stack(_group_rows(v_s, gi, g, c), hmasks, g, c, cs).astype(BF16) for gi in gis]
        kwst = [_stack(_group_rows(kw_s, gi, g, c), hmasks, g, c, cs) for gi in gis]
        acol = [_stack_cols(_group_rows(a_s, gi, g, c), g, c, cs)[0] for gi in gis]
        grow = [_stack_cols(_group_rows(g_s, gi, g, c), g, c, cs)[1] for gi in gis]
        qk = [_mm(qst[j], kst[j], _NT) for j in ng]
        cmats = [[cst_ref[gi * g + bl] for bl in range(g)] for gi in gis]
        qcs = [[_bmm(qg[j][bl * c:(bl + 1) * c], cmats[j][bl]) for bl in range(g)] for j in ng]
        qkw = [qk[j] * jnp.where(causal, jnp.exp(jnp.minimum(acol[j] + grow[j], 0.0)), 0.0) for j in ng]
        num = [_unstack(_mm(qkw[j].astype(BF16), vst[j]), g, c, cs) for j in ng]
        den = [_unstack(jnp.sum(qkw[j], axis=1, keepdims=True) * sel, g, c, cs) for j in ng]
        outs = []
        for j, gi in enumerate(gis):
            outs_g = []
            for bl in range(g):
                b = gi * g + bl
                kwb = kwst[j] if g == 1 else jnp.where(rowb == bl, kwst[j], 0.0)
                dec_row = dec_s[b, 0:1, :]
                ksum = jnp.sum(kw_s[b], axis=0, keepdims=True)
                outs_g.append((qcs[j][bl], num[j][bl * c:(bl + 1) * c], den[j][bl * c:(bl + 1) * c],
                               cmats[j][bl] * dec_row + _mm(kwb.astype(BF16), vst[j], _TN),
                               nst_ref[b] * dec_row + jnp.broadcast_to(ksum, (SUBLANES, MIX_W))))
            outs.append(outs_g)
        return outs

    def commit(gi, outs):
        for bl, (qc_b, num_b, den_b, c_b, n_b) in enumerate(outs):
            b = gi * g + bl
            qc_s[b] = qc_b
            num_s[b] = num_b
            den_s[b] = den_b
            cst_ref[b] = c_b
            nst_ref[b] = n_b

    _paired_loop(ngroups, per_groups, commit, width=8)

    num = num_s[...].reshape(n, MIX_W) + inter * qc_s[...].reshape(n, MIX_W)
    den = den_s[...].reshape(n, MIX_W) + inter * qn
    h = num / jnp.maximum(jnp.abs(den), e_negm)
    mu = _head_sum(h, ones_bd) * (1.0 / HEAD_DIM)
    hc = h - mu
    var = _head_sum(hc * hc, ones_bd) * (1.0 / HEAD_DIM)
    y = hc * lax.rsqrt(var + NORM_EPS) * ng_ref[...] * _sigmoid(d_o)
    y_ref[...] = y.reshape(bb, c, MIX_W)
    mst_ref[...] = m_t.reshape(bb, c, LANES)[:, c - SUBLANES:c, :]

    @pl.when(kk == pl.num_programs(1) - 1)
    def _():
        conv1_ref[...] = prev_ref[...]
        c1_ref[...] = cst_ref[...]
        n1_ref[...] = nst_ref[...]
        m1_ref[...] = mst_ref[...]


def _mlstm_call(x3, w, wlo, cw, cb, wq, wk, bi, bf, ng, ones_bd, expand, conv0, c0, n0, m0, bb, c, g, cs):
    b, l, _ = x3.shape
    x_spec, y_spec = _seq_specs(bb, c)
    st_conv = _state_spec(bb, SUBLANES, MIX_W)
    st_c = _state_spec(bb, MIX_W, MIX_W)
    st_n = _state_spec(bb, SUBLANES, MIX_W)
    st_m = _state_spec(bb, SUBLANES, LANES)
    consts = (w, wlo, cw, cb, wq, wk, bi, bf, ng, ones_bd, expand)
    rows = pltpu.VMEM((bb, c, MIX_W), F32)
    cols = pltpu.VMEM((bb, c, LANES), F32)
    return pl.pallas_call(
        functools.partial(_mlstm_kernel, g, cs),
        out_shape=(jax.ShapeDtypeStruct((b, l, MIX_W), F32), jax.ShapeDtypeStruct((b, SUBLANES, MIX_W), F32),
                   jax.ShapeDtypeStruct((b, MIX_W, MIX_W), F32), jax.ShapeDtypeStruct((b, SUBLANES, MIX_W), F32),
                   jax.ShapeDtypeStruct((b, SUBLANES, LANES), F32)),
        grid=(b // bb, l // c),
        in_specs=[x_spec] + [_full(a) for a in consts] + [st_conv, st_c, st_n, st_m],
        out_specs=(y_spec, st_conv, st_c, st_n, st_m),
        scratch_shapes=[pltpu.VMEM((bb, c + SUBLANES, MIX_W), F32), pltpu.VMEM((bb, SUBLANES, MIX_W), F32),
                        pltpu.VMEM((bb, MIX_W, MIX_W), F32), pltpu.VMEM((bb, SUBLANES, MIX_W), F32),
                        pltpu.VMEM((bb, SUBLANES, LANES), F32),
                        rows, rows, rows, rows, rows, cols, cols, rows, rows, rows],
        compiler_params=_cparams("parallel", "arbitrary"),
        name="mixer_mlstm",
    )(x3, *consts, conv0, c0, n0, m0)


def _gdn_kernel(g, cs, x_ref, w_ref, wlo_ref, cw_ref, alog_ref, dtb_ref, ng_ref, ones_ref, exp_ref, conv0_ref, s0_ref,
                y_ref, conv1_ref, s1_ref,
                ext_ref, prev_ref, sst_ref, q_s, k_s, vb_s, kb_s, qe_s, kd_s, el_s, gc_s, be_s, o_s):
    kk = pl.program_id(1)
    bb, c, _ = x_ref.shape
    n = bb * c
    ngroups = bb // g
    qkv_w = 3 * MIX_W

    @pl.when(kk == 0)
    def _():
        prev_ref[...] = conv0_ref[...]
        sst_ref[...] = s0_ref[...]

    xb, xlo = _split_bf16(x_ref[...].reshape(n, D_MODEL))
    proj = _mm(xb, w_ref[...])
    gates = proj[:, 4 * MIX_W:] + _mm(xlo, w_ref[:, 4 * MIX_W:]) + _mm(xb, wlo_ref[...])
    qkv = _causal_conv(proj[:, 0:qkv_w].reshape(bb, c, qkv_w), ext_ref, prev_ref, cw_ref, c)
    qkv = _silu(qkv).reshape(n, qkv_w)
    ones_bd = ones_ref[...]
    expand = exp_ref[...]
    q = qkv[:, 0:MIX_W]
    k = qkv[:, MIX_W:2 * MIX_W]
    v = qkv[:, 2 * MIX_W:3 * MIX_W]
    q = q * lax.rsqrt(_head_sum(q * q, ones_bd) + NORM_EPS) * HEAD_DIM ** -0.5
    k = k * lax.rsqrt(_head_sum(k * k, ones_bd) + NORM_EPS)
    z = proj[:, qkv_w:qkv_w + MIX_W]
    a_in = gates[:, 0:LANES]
    b_in = gates[:, LANES:2 * LANES]
    gdec = -jnp.exp(alog_ref[...]) * _softplus(a_in + dtb_ref[...])
    beta = _sigmoid(b_in)
    pos = _chunk_pos(bb, c, LANES)
    gcum = _seg_scan(gdec, pos, c, jnp.add, 0.0)
    g_last = _chunk_last(gcum, pos, bb, c)
    beta_hb = _to_hb(beta, expand)
    eg_hb = _to_hb(jnp.exp(gcum), expand)

    q_s[...] = q.reshape(bb, c, MIX_W)
    k_s[...] = k.reshape(bb, c, MIX_W)
    vb_s[...] = (v * beta_hb).reshape(bb, c, MIX_W)
    kb_s[...] = (k * beta_hb * eg_hb).reshape(bb, c, MIX_W)
    qe_s[...] = (q * eg_hb).reshape(bb, c, MIX_W)
    kd_s[...] = (k * _to_hb(jnp.exp(g_last - gcum), expand)).reshape(bb, c, MIX_W)
    el_s[...] = _to_hb(jnp.exp(g_last), expand).reshape(bb, c, MIX_W)
    gc_s[...] = gcum.reshape(bb, c, LANES)
    be_s[...] = beta.reshape(bb, c, LANES)

    hmasks = _head_masks()
    causal, strict = _bd_masks(cs)
    rowb = lax.broadcasted_iota(jnp.int32, (BD, MIX_W), 0) // (N_HEADS * cs)
    rows_b = N_HEADS * cs

    def per_groups(gis):
        ng = range(len(gis))
        qg = [_group_rows(qe_s, gi, g, c) for gi in gis]
        qst = [_stack(_group_rows(q_s, gi, g, c), hmasks, g, c, cs).astype(BF16) for gi in gis]
        kst = [_stack(_group_rows(k_s, gi, g, c), hmasks, g, c, cs).astype(BF16) for gi in gis]
        kdst = [_stack(_group_rows(kd_s, gi, g, c), hmasks, g, c, cs) for gi in gis]
        x = [jnp.concatenate([_stack(_group_rows(vb_s, gi, g, c), hmasks, g, c, cs),
                              _stack(_group_rows(kb_s, gi, g, c), hmasks, g, c, cs)], axis=1) for gi in gis]
        gcr = [_stack_cols(_group_rows(gc_s, gi, g, c), g, c, cs) for gi in gis]
        bcol = [_stack_cols(_group_rows(be_s, gi, g, c), g, c, cs)[0] for gi in gis]
        decay = [jnp.exp(jnp.minimum(gcr[j][0] - gcr[j][1], 0.0)) for j in ng]
        kk = [_mm(kst[j], kst[j], _NT) for j in ng]
        p = [jnp.where(strict, bcol[j] * kk[j] * decay[j], 0.0) for j in ng]
        ax = [_bmm(p[j], x[j]) for j in ng]
        x = [x[j] - ax[j] for j in ng]
        span = 2
        while span < cs:
            p = [_bmm(p[j], p[j]) for j in ng]
            px = [_bmm(p[j], x[j]) for j in ng]
            x = [x[j] + px[j] for j in ng]
            span *= 2
        qk = [_mm(qst[j], kst[j], _NT) for j in ng]
        smats = [[sst_ref[gi * g + bl] for bl in range(g)] for gi in gis]
        ws = [[_bmm(x[j][bl * rows_b:(bl + 1) * rows_b, MIX_W:2 * MIX_W], smats[j][bl]) for bl in range(g)]
              for j in ng]
        ub = []
        for j in ng:
            us = [x[j][bl * rows_b:(bl + 1) * rows_b, 0:MIX_W] - ws[j][bl] for bl in range(g)]
            ub.append((us[0] if g == 1 else jnp.concatenate(us, axis=0)).astype(BF16))
        qkd = [jnp.where(causal, qk[j] * decay[j], 0.0).astype(BF16) for j in ng]
        o_intra = [_unstack(_mm(qkd[j], ub[j]), g, c, cs) for j in ng]
        outs = []
        for j, gi in enumerate(gis):
            outs_g = []
            for bl in range(g):
                b = gi * g + bl
                o_b = o_intra[j][bl * c:(bl + 1) * c] + _bmm(qg[j][bl * c:(bl + 1) * c], smats[j][bl])
                kdb = kdst[j] if g == 1 else jnp.where(rowb == bl, kdst[j], 0.0)
                outs_g.append((o_b, smats[j][bl] * el_s[b, 0:1, :] + _mm(kdb.astype(BF16), ub[j], _TN)))
            outs.append(outs_g)
        return outs

    def commit(gi, outs):
        for bl, (o_b, s_b) in enumerate(outs):
            o_s[gi * g + bl] = o_b
            sst_ref[gi * g + bl] = s_b

    _paired_loop(ngroups, per_groups, commit, width=4)

    o = o_s[...].reshape(n, MIX_W)
    ms = _head_sum(o * o, ones_bd) * (1.0 / HEAD_DIM)
    y = o * lax.rsqrt(ms + NORM_EPS) * ng_ref[...] * _silu(z)
    y_ref[...] = y.reshape(bb, c, MIX_W)

    @pl.when(kk == pl.num_programs(1) - 1)
    def _():
        conv1_ref[...] = prev_ref[...]
        s1_ref[...] = sst_ref[...]


def _gdn_call(x3, w, wlo, cw, alog, dtb, ng, ones_bd, expand, conv0, s0, bb, c, g, cs):
    b, l, _ = x3.shape
    x_spec, y_spec = _seq_specs(bb, c)
    st_conv = _state_spec(bb, SUBLANES, 3 * MIX_W)
    st_s = _state_spec(bb, MIX_W, MIX_W)
    consts = (w, wlo, cw, alog, dtb, ng, ones_bd, expand)
    rows = pltpu.VMEM((bb, c, MIX_W), F32)
    cols = pltpu.VMEM((bb, c, LANES), F32)
    return pl.pallas_call(
        functools.partial(_gdn_kernel, g, cs),
        out_shape=(jax.ShapeDtypeStruct((b, l, MIX_W), F32), jax.ShapeDtypeStruct((b, SUBLANES, 3 * MIX_W), F32),
                   jax.ShapeDtypeStruct((b, MIX_W, MIX_W), F32)),
        grid=(b // bb, l // c),
        in_specs=[x_spec] + [_full(a) for a in consts] + [st_conv, st_s],
        out_specs=(y_spec, st_conv, st_s),
        scratch_shapes=[pltpu.VMEM((bb, c + SUBLANES, 3 * MIX_W), F32), pltpu.VMEM((bb, SUBLANES, 3 * MIX_W), F32),
                        pltpu.VMEM((bb, MIX_W, MIX_W), F32),
                        rows, rows, rows, rows, rows, rows, rows, cols, cols, rows],
        compiler_params=_cparams("parallel", "arbitrary"),
        name="mixer_gdn",
    )(x3, *consts, conv0, s0)


def _pad_cols(a, width):
    return jnp.pad(a, ((0, 0), (0, width - a.shape[1])))


def _row(a, width=None):
    a = a.reshape(1, -1).astype(F32)
    return a if width is None else _pad_cols(a, width)


def _block_diag(w4):
    h, d, _ = w4.shape
    eye = jnp.eye(h, dtype=w4.dtype)
    return jnp.einsum("hij,hg->higj", w4, eye).reshape(h * d, h * d)


def _bd_state(s):
    b, h, d, _ = s.shape
    eye = jnp.eye(h, dtype=s.dtype)
    return jnp.einsum("bhij,hg->bhigj", s, eye).reshape(b, h * d, h * d)


def _bd_blocks(s):
    b = s.shape[0]
    s5 = s.reshape(b, N_HEADS, HEAD_DIM, N_HEADS, HEAD_DIM)
    return jnp.stack([s5[:, h, :, h, :] for h in range(N_HEADS)], axis=1)


def _tail8(buf):
    return jnp.pad(buf, ((0, 0), (SUBLANES - (CONV_W - 1), 0), (0, 0)))


def _bcast8(a):
    return jnp.broadcast_to(a[:, None, :], (a.shape[0], SUBLANES, a.shape[1]))


def _layer_params(P, lb, l):
    w_f32 = P["w_in"][l].astype(F32)
    w_in = w_f32.astype(BF16)
    w_lo = (w_f32 - w_in.astype(F32)).astype(BF16)

    def gate_tiles(w, lo, hi):
        return [_pad_cols(w[:, lo:lo + N_HEADS], LANES), _pad_cols(w[:, lo + N_HEADS:hi], LANES)]

    p = {}
    p["w_a"] = w_in[:, _A0:_B0]
    p["w_b"] = jnp.concatenate([w_in[:, _B0:_B0 + 4 * MIX_W]] + gate_tiles(w_in, _B0 + 4 * MIX_W, _C0), axis=1)
    p["w_b_lo"] = jnp.concatenate(gate_tiles(w_lo, _B0 + 4 * MIX_W, _C0), axis=1)
    p["w_c"] = w_in[:, _C0:_D0]
    p["w_c_lo"] = w_lo[:, _C0 + MIX_W:_C0 + 2 * MIX_W]
    p["w_d"] = jnp.concatenate([w_in[:, _D0:_D0 + 3 * MIX_W]] + gate_tiles(w_in, _D0 + 3 * MIX_W, _G0), axis=1)
    p["w_d_lo"] = jnp.concatenate(gate_tiles(w_lo, _D0 + 3 * MIX_W, _G0), axis=1)
    p["w_g"] = w_in[:, _G0:]
    p["lru_cw"] = P["w_lru_conv"][l]
    p["lru_cb"] = _row(P["b_lru_conv"][l])
    p["lru_wra"] = _block_diag(P["w_lru_ra"][l]).astype(BF16)
    p["lru_bra"] = _row(P["b_lru_ra"][l])
    p["lru_wix"] = _block_diag(P["w_lru_ix"][l]).astype(BF16)
    p["lru_bix"] = _row(P["b_lru_ix"][l])
    p["lru_lam"] = _row(P["lru_lambda"][l])
    p["gdn_cw"] = P["w_gdn_conv"][l]
    p["gdn_alog"] = _row(P["gdn_a_log"][l], LANES)
    p["gdn_dtb"] = _row(P["gdn_dt_bias"][l], LANES)
    p["gdn_ng"] = _row(jnp.tile(P["gdn_norm_g"][l], N_HEADS))
    p["hg_loglb"] = _row(jnp.log(lb[l]))
    p["hg_log1mlb"] = _row(jnp.log1p(-lb[l]))
    p["hg_onemlb"] = _row(1.0 - lb[l])
    p["hg_ng"] = _row(jnp.tile(P["hg_norm_g"][l], N_HEADS))
    p["ml_cw"] = P["w_ml_conv"][l]
    p["ml_cb"] = _row(P["b_ml_conv"][l])
    p["ml_wq"] = _block_diag(P["w_ml_q"][l]).astype(BF16)
    p["ml_wk"] = _block_diag(P["w_ml_k"][l]).astype(BF16)
    p["ml_bi"] = _row(P["b_ml_i"][l], LANES)
    p["ml_bf"] = _row(P["b_ml_f"][l], LANES)
    p["ml_ng"] = _row(P["ml_norm_g"][l])
    p["w_br"] = P["w_branch"][l].astype(BF16)
    p["w_out"] = P["w_out"][l].astype(BF16)
    p["ln1_g"] = _row(P["ln1_g"][l])
    p["ln1_b"] = _row(P["ln1_b"][l])
    p["ln2_g"] = _row(P["ln2_g"][l])
    p["ln2_b"] = _row(P["ln2_b"][l])
    p["w_r"] = _pad_cols(P["w_router"][l], LANES).astype(BF16)
    p["b_r"] = jnp.concatenate([P["b_router"][l].astype(F32), jnp.full((LANES - N_EXPERTS,), -1e30, F32)]).reshape(1, LANES)
    return p


def _mixers(x3, p, st, consts, bb, c, g, cs):
    ones_bd, expand = consts
    lru_conv, lru_h, gdn_conv, gdn_s, hg_st, ml_conv, ml_c, ml_n, ml_m = st
    y_a, lru_conv1, lru_h1 = _lru_call(x3, p["w_a"], p["lru_cw"], p["lru_cb"], p["lru_wra"], p["lru_bra"],
                                       p["lru_wix"], p["lru_bix"], p["lru_lam"], lru_conv, lru_h, bb, c)
    y_b, gdn_conv1, gdn_s1 = _gdn_call(x3, p["w_b"], p["w_b_lo"], p["gdn_cw"], p["gdn_alog"], p["gdn_dtb"], p["gdn_ng"],
                                       ones_bd, expand, gdn_conv, gdn_s, bb, c, g, cs)
    y_c, hg_st1 = _hgrn_call(x3, p["w_c"], p["w_c_lo"], p["hg_loglb"], p["hg_log1mlb"], p["hg_onemlb"], p["hg_ng"], ones_bd,
                             hg_st, bb, c)
    y_d, ml_conv1, ml_c1, ml_n1, ml_m1 = _mlstm_call(x3, p["w_d"], p["w_d_lo"], p["ml_cw"], p["ml_cb"], p["ml_wq"], p["ml_wk"],
                                                     p["ml_bi"], p["ml_bf"], p["ml_ng"], ones_bd, expand,
                                                     ml_conv, ml_c, ml_n, ml_m, bb, c, g, cs)
    return (y_a, y_b, y_c, y_d), (lru_conv1, lru_h1, gdn_conv1, gdn_s1, hg_st1, ml_conv1, ml_c1, ml_n1, ml_m1)


def _states_to_kernel(st):
    lru_conv, lru_h, gdn_conv, gdn_s, hg_s, ml_conv, ml_c, ml_n, ml_m = [s.astype(F32) for s in st]
    b = lru_h.shape[0]
    return (_tail8(lru_conv), _bcast8(lru_h), _tail8(gdn_conv), _bd_state(gdn_s),
            jnp.swapaxes(_bd_state(hg_s), 1, 2), _tail8(ml_conv), _bd_state(ml_c),
            _bcast8(ml_n.reshape(b, MIX_W)), _bcast8(_pad_cols(ml_m, LANES)))


def _states_from_kernel(st):
    lru_conv, lru_h, gdn_conv, gdn_s, hg_st, ml_conv, ml_c, ml_n, ml_m = st
    b = lru_h.shape[0]
    tail = SUBLANES - (CONV_W - 1)
    return (lru_conv[:, tail:], lru_h[:, SUBLANES - 1], gdn_conv[:, tail:], _bd_blocks(gdn_s),
            _bd_blocks(jnp.swapaxes(hg_st, 1, 2)), ml_conv[:, tail:], _bd_blocks(ml_c),
            ml_n[:, 0].reshape(b, N_HEADS, HEAD_DIM), ml_m[:, SUBLANES - 1, :N_HEADS])


def _route(top_e):
    t = top_e.shape[0]
    m = t * TOP_K
    flat_e = top_e.reshape(m)
    onehot = (flat_e[:, None] == jnp.arange(N_EXPERTS, dtype=jnp.int32)[None, :]).astype(jnp.int32)
    csum = jnp.cumsum(onehot, axis=0)
    rank = jnp.sum(onehot * csum, axis=1) - 1
    counts = csum[-1]
    padded = (counts + MOE_BLOCK_ROWS - 1) // MOE_BLOCK_ROWS * MOE_BLOCK_ROWS
    pad_ends = jnp.cumsum(padded)
    dest = (pad_ends - padded)[flat_e] + rank
    n_blocks = m // MOE_BLOCK_ROWS + N_EXPERTS
    mp = n_blocks * MOE_BLOCK_ROWS
    block_start = jnp.arange(n_blocks, dtype=jnp.int32) * MOE_BLOCK_ROWS
    block_e = jnp.minimum(jnp.sum((pad_ends[None, :] <= block_start[:, None]).astype(jnp.int32), axis=1),
                          N_EXPERTS - 1)
    n_used = (pad_ends[-1] // MOE_BLOCK_ROWS).astype(jnp.int32).reshape(1)
    n_pad = padded - counts
    j = jnp.arange(MOE_BLOCK_ROWS, dtype=jnp.int32)[None, :]
    is_pad = (j < n_pad[:, None]).reshape(-1)
    pad_row = ((pad_ends - n_pad)[:, None] + j).reshape(-1)
    tail_rank = jnp.cumsum(jnp.logical_not(is_pad).astype(jnp.int32)) - 1
    zero_rows = jnp.where(is_pad, pad_row, pad_ends[-1] + tail_rank).astype(jnp.int32)
    return dest.reshape(t, TOP_K), zero_rows, block_e, n_used, mp


def _dest_tiles(dest):
    t = dest.shape[0]
    n = t // COMBINE_ROWS
    return jnp.swapaxes(dest.reshape(n, COMBINE_ROWS, TOP_K), 1, 2).reshape(n, 1, TOP_K * COMBINE_ROWS)


def _trunk_layer(layer, xs, states, p, moe_w, consts, cfgs):
    x1s, tes, tgs, new_states = [], [], [], []
    for x3, st, (bb, c, g, cs, tm) in zip(xs, states, cfgs):
        b, l, _ = x3.shape
        ys, st1 = _mixers(x3, p, st, consts, bb, c, g, cs)
        x1, te, tg = _merge_call(x3.reshape(b * l, D_MODEL), [y.reshape(b * l, MIX_W) for y in ys],
                                 p["w_g"], p["w_br"], p["w_out"], p["ln1_g"], p["ln1_b"], p["w_r"], p["b_r"], tm)
        x1s.append(x1)
        tes.append(te[:, :TOP_K])
        tgs.append(tg)
        new_states.append(st1)
    dest, zero_rows, block_e, n_used, mp = _route(jnp.concatenate(tes, axis=0))
    d_tiles = []
    off = 0
    for x1 in x1s:
        d_tiles.append(_dest_tiles(lax.slice_in_dim(dest, off, off + x1.shape[0], axis=0)))
        off += x1.shape[0]
    z_tiles = zero_rows.reshape(-1, 1, TOP_K * COMBINE_ROWS)
    rows = _dispatch_call(x1s[0], x1s[1], jnp.concatenate(d_tiles + [z_tiles], axis=0), mp)
    ys = _moe_call(layer, block_e, n_used, rows, *moe_w)
    outs = []
    for x3, x1, tg, dt in zip(xs, x1s, tgs, d_tiles):
        outs.append(_combine_ln2_call(x1, tg, dt, ys, p["ln2_g"], p["ln2_b"]).reshape(x3.shape))
    return outs, new_states


def _row_tile(t):
    for tm in (512, 384, 256, 128, 64, 32, 16):
        if t % tm == 0:
            return tm
    raise ValueError(f"no row tile for {t} rows")


def _group_cfg(b, l):
    if l % PROMPT_CHUNK == 0:
        return (min(b, 8), PROMPT_CHUNK, 1, BD // N_HEADS, _row_tile(b * l))
    assert l == SUBLANES, "sequence length must be a multiple of the prompt chunk or one sublane tile"
    g = BD // (N_HEADS * SUBLANES)
    bb = min(b, 16)
    assert bb % g == 0 and b % bb == 0
    return (bb, SUBLANES, g, SUBLANES, _row_tile(b * l))


def _zero_states(b):
    return (jnp.zeros((b, CONV_W - 1, MIX_W), F32), jnp.zeros((b, MIX_W), F32),
            jnp.zeros((b, CONV_W - 1, 3 * MIX_W), F32), jnp.zeros((b, N_HEADS, HEAD_DIM, HEAD_DIM), F32),
            jnp.zeros((b, N_HEADS, HEAD_DIM, HEAD_DIM), F32), jnp.zeros((b, CONV_W - 1, MIX_W), F32),
            jnp.zeros((b, N_HEADS, HEAD_DIM, HEAD_DIM), F32), jnp.zeros((b, N_HEADS, HEAD_DIM), F32),
            jnp.zeros((b, N_HEADS), F32))


def kernel(x_prompt, x_sample, state_lru_conv, state_lru_h, state_gdn_conv, state_gdn, state_hgrn, state_mlstm_conv, state_mlstm_c, state_mlstm_n, state_mlstm_m, meta_tokens, ln_emb_g, ln_emb_b, hg_lb_logits, w_in, w_lru_conv, b_lru_conv, w_lru_ra, b_lru_ra, w_lru_ix, b_lru_ix, lru_lambda, w_gdn_conv, gdn_a_log, gdn_dt_bias, gdn_norm_g, hg_norm_g, w_ml_conv, b_ml_conv, w_ml_q, w_ml_k, b_ml_i, b_ml_f, ml_norm_g, w_branch, w_out, ln1_g, ln1_b, ln2_g, ln2_b, w_router, b_router, w_up, b_up, w_down, b_down):
    P = dict(w_in=w_in, w_lru_conv=w_lru_conv, b_lru_conv=b_lru_conv, w_lru_ra=w_lru_ra, b_lru_ra=b_lru_ra,
             w_lru_ix=w_lru_ix, b_lru_ix=b_lru_ix, lru_lambda=lru_lambda, w_gdn_conv=w_gdn_conv,
             gdn_a_log=gdn_a_log, gdn_dt_bias=gdn_dt_bias, gdn_norm_g=gdn_norm_g, hg_norm_g=hg_norm_g,
             w_ml_conv=w_ml_conv, b_ml_conv=b_ml_conv, w_ml_q=w_ml_q, w_ml_k=w_ml_k, b_ml_i=b_ml_i,
             b_ml_f=b_ml_f, ml_norm_g=ml_norm_g, w_branch=w_branch, w_out=w_out, ln1_g=ln1_g, ln1_b=ln1_b,
             ln2_g=ln2_g, ln2_b=ln2_b, w_router=w_router, b_router=b_router, w_up=w_up, b_up=b_up,
             w_down=w_down, b_down=b_down)
    depth = w_in.shape[0]
    lb_cum = jnp.cumsum(jax.nn.softmax(hg_lb_logits.astype(F32), axis=0), axis=0)
    lb = lb_cum - lb_cum[0:1]

    lane = jnp.arange(MIX_W) // HEAD_DIM
    ones_bd = (lane[:, None] == lane[None, :]).astype(BF16)
    expand = (jnp.arange(LANES)[:, None] == lane[None, :]).astype(BF16)
    consts = (ones_bd, expand)
    moe_w = (w_up.astype(BF16), b_up.reshape(depth, N_EXPERTS, 1, 2 * D_FF).astype(F32),
             w_down.astype(BF16), b_down.reshape(depth, N_EXPERTS, 1, D_MODEL).astype(F32))

    bp, lp0, _ = x_prompt.shape
    bs, ls, _ = x_sample.shape
    meta = jnp.broadcast_to(meta_tokens.astype(F32)[None], (bp, N_META, D_MODEL))
    xp_in = jnp.concatenate([meta, x_prompt], axis=1)
    lp = lp0 + N_META
    cfgs = [_group_cfg(bp, lp), _group_cfg(bs, ls)]
    g_emb, b_emb = _row(ln_emb_g), _row(ln_emb_b)
    xp = _ln_call(xp_in.reshape(bp * lp, D_MODEL), g_emb, b_emb, cfgs[0][4]).reshape(bp, lp, D_MODEL)
    xs_ = _ln_call(x_sample.reshape(bs * ls, D_MODEL), g_emb, b_emb, cfgs[1][4]).reshape(bs, ls, D_MODEL)

    sample_states = (state_lru_conv, state_lru_h, state_gdn_conv, state_gdn, state_hgrn,
                     state_mlstm_conv, state_mlstm_c, state_mlstm_n, state_mlstm_m)
    zero_p = _zero_states(bp)
    xs = [xp, xs_]
    collected = ([], [])
    for l in range(depth):
        p = _layer_params(P, lb, l)
        states = [_states_to_kernel(zero_p), _states_to_kernel(tuple(s[l] for s in sample_states))]
        xs, new_states = _trunk_layer(l, xs, states, p, moe_w, consts, cfgs)
        for grp in range(2):
            collected[grp].append(_states_from_kernel(new_states[grp]))
    new_p = tuple(jnp.stack([layer[i] for layer in collected[0]]) for i in range(9))
    new_s = tuple(jnp.stack([layer[i] for layer in collected[1]]) for i in range(9))
    y_prompt = xs[0][:, N_META:]
    y_sample = xs[1]
    out = [y_prompt, y_sample]
    for i in range(9):
        out.append(new_p[i])
        out.append(new_s[i])
    return tuple(out)
```

```python
import functools

import jax
import jax.numpy as jnp
from jax import lax
from jax.experimental import pallas as pl
from jax.experimental.pallas import tpu as pltpu

F32 = jnp.float32
BF16 = jnp.bfloat16
HIGHEST = lax.Precision.HIGHEST

D_MODEL = 1024
DEPTH = 4
N_META = 16
N_BRANCH = 4
MIX_W = 256
N_HEADS = 4
HEAD_DIM = 64
CONV_W = 4
LRU_C = 8.0
N_EXPERTS = 32
TOP_K = 4
D_FF = 1024
SWIGLU_LIMIT = 7.0
SWIGLU_ALPHA = 1.702
LN_EPS = 1e-5
NORM_EPS = 1e-6
DEEPNORM_ALPHA = (2 * DEPTH) ** 0.25

LANES = 128
SUBLANES = 8
BD = 256
VMEM_LIMIT_BYTES = 56 * 1024 * 1024
PROMPT_CHUNK = 48
MOE_BLOCK_ROWS = 256
COMBINE_ROWS = 128

_A0, _B0, _C0, _D0, _G0 = 0, 512, 1544, 2568, 3344

_NN = (((1,), (0,)), ((), ()))
_NT = (((1,), (1,)), ((), ()))
_TN = (((0,), (0,)), ((), ()))


def _mm(a, b, dims=_NN, precision=None):
    return lax.dot_general(a, b, dims, precision=precision, preferred_element_type=F32)


def _bmm(a, b, dims=_NN):
    return _mm(a.astype(BF16), b.astype(BF16), dims)


def _split_bf16(x):
    hi = x.astype(BF16)
    return hi, (x - hi.astype(F32)).astype(BF16)


def _mm3(a, b, dims=_NN):
    ah, al = _split_bf16(a)
    bh, bl = _split_bf16(b)
    return _mm(ah, bh, dims) + (_mm(al, bh, dims) + _mm(ah, bl, dims))


def _mm_sel(x, sel, dims=_NN, terms=2):
    out = None
    r = x
    for _ in range(terms):
        p = r.astype(BF16)
        t = _mm(p, sel, dims)
        out = t if out is None else out + t
        r = r - p.astype(F32)
    return out


def _sigmoid(x):
    return jax.nn.sigmoid(x)


def _softplus(x):
    return jnp.maximum(x, 0.0) + jnp.log1p(jnp.exp(-jnp.abs(x)))


def _log_sigmoid(x):
    return jnp.minimum(x, 0.0) - jnp.log1p(jnp.exp(-jnp.abs(x)))


def _expm1(x):
    return jnp.tanh(0.5 * x) * (jnp.exp(x) + 1.0)


def _silu(x):
    return x * _sigmoid(x)


def _gelu_tanh(x):
    return 0.5 * x * (1.0 + jnp.tanh(0.7978845608028654 * (x + 0.044715 * (x * x * x))))


def _layernorm(x, g, b):
    mu = jnp.mean(x, axis=-1, keepdims=True)
    xc = x - mu
    var = jnp.mean(xc * xc, axis=-1, keepdims=True)
    return xc * lax.rsqrt(var + LN_EPS) * g + b


def _cparams(*sem):
    return pltpu.CompilerParams(dimension_semantics=sem, vmem_limit_bytes=VMEM_LIMIT_BYTES)


def _full(a):
    nd = a.ndim
    return pl.BlockSpec(a.shape, lambda *_: (0,) * nd)


def _ln_kernel(x_ref, g_ref, b_ref, o_ref):
    o_ref[...] = _layernorm(x_ref[...], g_ref[...], b_ref[...])


def _ln_call(x2, g, b, tm):
    t, d = x2.shape
    return pl.pallas_call(
        _ln_kernel,
        out_shape=jax.ShapeDtypeStruct((t, d), F32),
        grid=(t // tm,),
        in_specs=[pl.BlockSpec((tm, d), lambda i: (i, 0)), _full(g), _full(b)],
        out_specs=pl.BlockSpec((tm, d), lambda i: (i, 0)),
        compiler_params=_cparams("parallel"),
        name="ln_rows",
    )(x2, g, b)


def _combine_ln2_kernel(n, dcur_ref, dnxt_ref, x_ref, gate_ref, g_ref, b_ref, ys_hbm, o_ref, buf_ref, sem_ref):
    i = pl.program_id(0)
    tm = x_ref.shape[0]
    nrow = TOP_K * tm

    def row_copy(d_ref, r, slot):
        return pltpu.make_async_copy(ys_hbm.at[pl.ds(d_ref[0, 0, r], 1)], buf_ref.at[slot, pl.ds(r, 1)],
                                     sem_ref.at[slot])

    def issue(d_ref, slot):
        def body(r2, carry):
            row_copy(d_ref, 2 * r2, slot).start(priority=0)
            row_copy(d_ref, 2 * r2 + 1, slot).start(priority=1)
            return carry

        lax.fori_loop(0, nrow // 2, body, 0, unroll=4)

    @pl.when(i == 0)
    def _():
        issue(dcur_ref, 0)

    @pl.when(i + 1 < n)
    def _():
        issue(dnxt_ref, (i + 1) % 2)

    slot = i % 2
    pltpu.make_async_copy(ys_hbm.at[pl.ds(0, nrow)], buf_ref.at[slot], sem_ref.at[slot]).wait()
    gate = gate_ref[...]
    ff = None
    for k in range(TOP_K):
        term = buf_ref[slot, pl.ds(k * tm, tm), :] * gate[:, k:k + 1]
        ff = term if ff is None else ff + term
    o_ref[...] = _layernorm(DEEPNORM_ALPHA * x_ref[...] + ff, g_ref[...], b_ref[...])


def _combine_ln2_call(x2, gates, dest_tiles, ys, g, b):
    t, d = x2.shape
    tm = COMBINE_ROWS
    n = t // tm
    row = pl.BlockSpec((tm, d), lambda i: (i, 0))
    dspec = lambda f: pl.BlockSpec((1, 1, TOP_K * tm), f, memory_space=pltpu.SMEM)
    return pl.pallas_call(
        functools.partial(_combine_ln2_kernel, n),
        out_shape=jax.ShapeDtypeStruct((t, d), F32),
        grid=(n,),
        in_specs=[dspec(lambda i: (i, 0, 0)), dspec(lambda i: (jnp.minimum(i + 1, n - 1), 0, 0)), row,
                  pl.BlockSpec((tm, LANES), lambda i: (i, 0)), _full(g), _full(b),
                  pl.BlockSpec(memory_space=pl.ANY)],
        out_specs=row,
        scratch_shapes=[pltpu.VMEM((2, TOP_K * tm, d), F32), pltpu.SemaphoreType.DMA((2,))],
        compiler_params=_cparams("arbitrary"),
        name="combine_ln2",
    )(dest_tiles, dest_tiles, x2, gates, g, b, ys)


def _merge_kernel(x_ref, ya_ref, yb_ref, yc_ref, yd_ref, wg_ref, wbr_ref, wout_ref, g_ref, b_ref, wr_ref, br_ref,
                  x1_ref, te_ref, tg_ref):
    x = x_ref[...]
    xb = x.astype(BF16)
    merged = None
    for n, y_ref in enumerate((ya_ref, yb_ref, yc_ref, yd_ref)):
        up = _mm(y_ref[...].astype(BF16), wbr_ref[n])
        gate = _sigmoid(_mm(xb, wg_ref[:, n * D_MODEL:(n + 1) * D_MODEL]))
        merged = up * gate if merged is None else merged + up * gate
    out = _mm(merged.astype(BF16), wout_ref[...])
    x1 = _layernorm(DEEPNORM_ALPHA * x + out, g_ref[...], b_ref[...])
    x1_ref[...] = x1
    logits = _mm(x1.astype(BF16), wr_ref[...]) + br_ref[...]
    lane = lax.broadcasted_iota(jnp.int32, logits.shape, 1)
    vals = []
    idxs = []
    v = logits
    for _ in range(TOP_K):
        m = jnp.max(v, axis=-1, keepdims=True)
        idx = jnp.min(jnp.where(v == m, lane, LANES), axis=-1, keepdims=True)
        vals.append(m)
        idxs.append(idx)
        v = jnp.where(lane == idx, -jnp.inf, v)
    es = [jnp.exp(m - vals[0]) for m in vals]
    den = es[0] + es[1] + es[2] + es[3]
    te = jnp.zeros(logits.shape, jnp.int32)
    tg = jnp.zeros(logits.shape, F32)
    for j in range(TOP_K):
        te = jnp.where(lane == j, idxs[j], te)
        tg = jnp.where(lane == j, es[j] / den, tg)
    te_ref[...] = te
    tg_ref[...] = tg


def _merge_call(x2, ys, wg, wbr, wout, g, b, wr, br, tm):
    t, d = x2.shape
    row = pl.BlockSpec((tm, d), lambda i: (i, 0))
    yrow = pl.BlockSpec((tm, MIX_W), lambda i: (i, 0))
    lrow = pl.BlockSpec((tm, LANES), lambda i: (i, 0))
    return pl.pallas_call(
        _merge_kernel,
        out_shape=(jax.ShapeDtypeStruct((t, d), F32),
                   jax.ShapeDtypeStruct((t, LANES), jnp.int32), jax.ShapeDtypeStruct((t, LANES), F32)),
        grid=(t // tm,),
        in_specs=[row, yrow, yrow, yrow, yrow, _full(wg), _full(wbr), _full(wout), _full(g), _full(b),
                  _full(wr), _full(br)],
        out_specs=(row, lrow, lrow),
        compiler_params=_cparams("parallel"),
        name="merge_ln_router",
    )(x2, *ys, wg, wbr, wout, g, b, wr, br)


def _dispatch_kernel(n_p, n_s, n, d_ref, xp_ref, xs_ref, o_hbm, buf_ref, sem_ref):
    i = pl.program_id(0)
    tm = xp_ref.shape[0]
    nrow = TOP_K * tm
    slot = i % 2

    def wait_slot(s):
        pltpu.make_async_copy(o_hbm.at[pl.ds(0, nrow)], o_hbm.at[pl.ds(0, nrow)], sem_ref.at[s]).wait()

    @pl.when(i >= 2)
    def _():
        wait_slot(slot)

    @pl.when(i < n_p)
    def _():
        buf_ref[slot] = xp_ref[...]

    @pl.when((i >= n_p) & (i < n_p + n_s))
    def _():
        buf_ref[slot] = xs_ref[...]

    @pl.when(i >= n_p + n_s)
    def _():
        buf_ref[slot] = jnp.zeros((tm, D_MODEL), F32)

    def row_copy(k, t):
        return pltpu.make_async_copy(buf_ref.at[slot, pl.ds(t, 1)], o_hbm.at[pl.ds(d_ref[0, 0, k * tm + t], 1)],
                                     sem_ref.at[slot])

    for k in range(TOP_K):
        def body(t2, carry, k=k):
            row_copy(k, 2 * t2).start(priority=0)
            row_copy(k, 2 * t2 + 1).start(priority=1)
            return carry

        lax.fori_loop(0, tm // 2, body, 0, unroll=4)

    @pl.when(i == n - 1)
    def _():
        if n >= 2:
            wait_slot(1 - slot)
        wait_slot(slot)


def _dispatch_call(x_p, x_s, dest_tiles, mp):
    tm = COMBINE_ROWS
    d = x_p.shape[1]
    n_p, n_s, n = x_p.shape[0] // tm, x_s.shape[0] // tm, dest_tiles.shape[0]
    return pl.pallas_call(
        functools.partial(_dispatch_kernel, n_p, n_s, n),
        out_shape=jax.ShapeDtypeStruct((mp, d), F32),
        grid=(n,),
        in_specs=[pl.BlockSpec((1, 1, TOP_K * tm), lambda i: (i, 0, 0), memory_space=pltpu.SMEM),
                  pl.BlockSpec((tm, d), lambda i: (jnp.minimum(i, n_p - 1), 0)),
                  pl.BlockSpec((tm, d), lambda i: (jnp.clip(i - n_p, 0, n_s - 1), 0))],
        out_specs=pl.BlockSpec(memory_space=pl.ANY),
        scratch_shapes=[pltpu.VMEM((2, tm, d), F32), pltpu.SemaphoreType.DMA((2,))],
        compiler_params=_cparams("arbitrary"),
        name="moe_dispatch",
    )(dest_tiles, x_p, x_s)


def _moe_kernel(be_ref, nused_ref, xs_ref, wu_ref, bu_ref, wd_ref, bd_ref, o_ref, wub_ref, wdb_ref):
    i = pl.program_id(0)

    @pl.when((i == 0) | (be_ref[i] != be_ref[jnp.maximum(i - 1, 0)]))
    def _():
        wub_ref[...] = wu_ref[0, 0].astype(BF16)
        wdb_ref[...] = wd_ref[0, 0].astype(BF16)

    @pl.when(i < nused_ref[0])
    def _():
        h = _mm(xs_ref[...].astype(BF16), wub_ref[...]) + bu_ref[0, 0]
        g = jnp.minimum(h[:, :D_FF], SWIGLU_LIMIT)
        u = jnp.clip(h[:, D_FF:], -SWIGLU_LIMIT, SWIGLU_LIMIT)
        act = (u + 1.0) * g * _sigmoid(SWIGLU_ALPHA * g)
        o_ref[...] = _mm(act.astype(BF16), wdb_ref[...]) + bd_ref[0, 0]

    @pl.when(i >= nused_ref[0])
    def _():
        o_ref[...] = jnp.zeros(o_ref.shape, F32)


def _moe_call(layer, block_e, n_used, xs, wu, bu, wd, bd):
    mp, d = xs.shape
    nb = mp // MOE_BLOCK_ROWS
    grid_spec = pltpu.PrefetchScalarGridSpec(
        num_scalar_prefetch=2,
        grid=(nb,),
        in_specs=[
            pl.BlockSpec((MOE_BLOCK_ROWS, d), lambda i, be, nu: (i, 0)),
            pl.BlockSpec((1, 1, d, 2 * D_FF), lambda i, be, nu: (layer, be[i], 0, 0)),
            pl.BlockSpec((1, 1, 1, 2 * D_FF), lambda i, be, nu: (layer, be[i], 0, 0)),
            pl.BlockSpec((1, 1, D_FF, d), lambda i, be, nu: (layer, be[i], 0, 0)),
            pl.BlockSpec((1, 1, 1, d), lambda i, be, nu: (layer, be[i], 0, 0)),
        ],
        out_specs=pl.BlockSpec((MOE_BLOCK_ROWS, d), lambda i, be, nu: (i, 0)),
        scratch_shapes=[pltpu.VMEM((d, 2 * D_FF), BF16), pltpu.VMEM((D_FF, d), BF16)],
    )
    return pl.pallas_call(
        _moe_kernel,
        out_shape=jax.ShapeDtypeStruct((mp, d), F32),
        grid_spec=grid_spec,
        compiler_params=_cparams("arbitrary"),
        name="moe_experts",
    )(block_e, n_used, xs, wu, bu, wd, bd)


def _chunk_pos(bb, c, width):
    return lax.broadcasted_iota(jnp.int32, (bb, c, width), 1).reshape(bb * c, width)


def _seg_scan(x, pos, c, op, ident):
    s = 1
    while s < c:
        x = op(x, jnp.where(pos >= s, pltpu.roll(x, s, 0), ident))
        s *= 2
    return x


def _chunk_last(x, pos, bb, c):
    w = x.shape[1]
    x3 = jnp.where(pos == c - 1, x, 0.0).reshape(bb, c, w)
    return jnp.broadcast_to(jnp.sum(x3, axis=1, keepdims=True), (bb, c, w)).reshape(bb * c, w)


def _causal_conv(x3, ext_ref, prev_ref, cw_ref, c):
    ext_ref[:, 0:SUBLANES, :] = prev_ref[...]
    ext_ref[:, SUBLANES:SUBLANES + c, :] = x3
    y = None
    for j in range(CONV_W):
        lo = SUBLANES - (CONV_W - 1) + j
        tap = ext_ref[:, lo:lo + c, :] * cw_ref[j:j + 1, :]
        y = tap if y is None else y + tap
    prev_ref[...] = ext_ref[:, c:c + SUBLANES, :]
    return y


def _head_masks(width=MIX_W):
    lane = lax.broadcasted_iota(jnp.int32, (1, width), 1)
    return [(lane >= h * HEAD_DIM) & (lane < (h + 1) * HEAD_DIM) for h in range(N_HEADS)]


def _lane_masks():
    lane = lax.broadcasted_iota(jnp.int32, (1, LANES), 1)
    return [lane == h for h in range(N_HEADS)]


def _stack(xg, masks, g, c, cs):
    pieces = []
    for b in range(g):
        xb = xg[b * c:(b + 1) * c]
        for m in masks:
            pieces.append(jnp.where(m, xb, 0.0))
            if cs > c:
                pieces.append(jnp.zeros((cs - c, xg.shape[1]), xg.dtype))
    return jnp.concatenate(pieces, axis=0)


def _unstack(y, g, c, cs):
    outs = []
    for b in range(g):
        acc = None
        for h in range(N_HEADS):
            r0 = (b * N_HEADS + h) * cs
            piece = y[r0:r0 + c]
            acc = piece if acc is None else acc + piece
        outs.append(acc)
    return outs[0] if g == 1 else jnp.concatenate(outs, axis=0)


def _bd_masks(cs):
    r = lax.broadcasted_iota(jnp.int32, (BD, BD), 0)
    q = lax.broadcasted_iota(jnp.int32, (BD, BD), 1)
    shift = cs.bit_length() - 1
    same = (r >> shift) == (q >> shift)
    tr = r & (cs - 1)
    tq = q & (cs - 1)
    return same & (tq <= tr), same & (tq < tr)


def _head_block_mask():
    r = lax.broadcasted_iota(jnp.int32, (MIX_W, MIX_W), 0)
    q = lax.broadcasted_iota(jnp.int32, (MIX_W, MIX_W), 1)
    return (r // HEAD_DIM) == (q // HEAD_DIM)


def _head_sum(x, ones_bd):
    return _mm_sel(x, ones_bd)


def _to_hb(cols, expand):
    return _mm_sel(cols, expand, terms=2)


def _stack_cols(cols_g, g, c, cs):
    st = _stack(cols_g, _lane_masks(), g, c, cs)
    col = jnp.sum(st, axis=1, keepdims=True)
    ones = jnp.ones((SUBLANES, LANES), F32)
    row = _mm(ones, st, _NT, precision=HIGHEST)[0:1, :]
    return col, row


def _group_rows(ref, gi, g, c):
    w = ref.shape[2]
    if g == 1:
        return ref[gi]
    return ref[pl.ds(gi * g, g)].reshape(g * c, w)


def _paired_loop(n, compute_many, commit, width=2):
    while n % width:
        width //= 2

    def trip(j, carry):
        idx = [width * j + w for w in range(width)]
        for i, out in zip(idx, compute_many(idx)):
            commit(i, out)
        return carry

    lax.fori_loop(0, n // width, trip, 0)


def _seq_specs(bb, c):
    x_spec = pl.BlockSpec((bb, c, D_MODEL), lambda i, k: (i, k, 0))
    y_spec = pl.BlockSpec((bb, c, MIX_W), lambda i, k: (i, k, 0))
    return x_spec, y_spec


def _state_spec(bb, *tail):
    nt = len(tail)
    return pl.BlockSpec((bb,) + tuple(tail), lambda i, k: (i,) + (0,) * nt)


def _lru_kernel(x_ref, w_ref, cw_ref, cb_ref, wra_ref, bra_ref, wix_ref, bix_ref, lam_ref, conv0_ref, h0_ref,
                y_ref, conv1_ref, h1_ref, ext_ref, prev_ref, hst_ref):
    k = pl.program_id(1)
    bb, c, _ = x_ref.shape
    n = bb * c

    @pl.when(k == 0)
    def _():
        prev_ref[...] = conv0_ref[...]
        hst_ref[...] = h0_ref[...]

    xb = x_ref[...].reshape(n, D_MODEL).astype(BF16)
    proj = _mm(xb, w_ref[...])
    a_gate = proj[:, MIX_W:]
    xa = _causal_conv(proj[:, :MIX_W].reshape(bb, c, MIX_W), ext_ref, prev_ref, cw_ref, c) + cb_ref[...]
    xa = xa.reshape(n, MIX_W)
    xab = xa.astype(BF16)
    r = _sigmoid(_mm(xab, wra_ref[...]) + bra_ref[...])
    i = _sigmoid(_mm(xab, wix_ref[...]) + bix_ref[...])
    log_a = -LRU_C * r * _softplus(-lam_ref[...])
    a_cum = jnp.exp(log_a)
    b_cum = jnp.sqrt(-_expm1(2.0 * log_a)) * (i * xa)
    pos = _chunk_pos(bb, c, MIX_W)
    s = 1
    while s < c:
        keep = pos >= s
        a_sh = pltpu.roll(a_cum, s, 0)
        b_sh = pltpu.roll(b_cum, s, 0)
        b_cum = jnp.where(keep, a_cum * b_sh + b_cum, b_cum)
        a_cum = jnp.where(keep, a_cum * a_sh, a_cum)
        s *= 2
    h0 = jnp.broadcast_to(hst_ref[:, SUBLANES - 1:SUBLANES, :], (bb, c, MIX_W)).reshape(n, MIX_W)
    h = a_cum * h0 + b_cum
    y_ref[...] = (h * _gelu_tanh(a_gate)).reshape(bb, c, MIX_W)
    hst_ref[...] = h.reshape(bb, c, MIX_W)[:, c - SUBLANES:c, :]

    @pl.when(k == pl.num_programs(1) - 1)
    def _():
        conv1_ref[...] = prev_ref[...]
        h1_ref[...] = hst_ref[...]


def _lru_call(x3, w, cw, cb, wra, bra, wix, bix, lam, conv0, h0, bb, c):
    b, l, _ = x3.shape
    x_spec, y_spec = _seq_specs(bb, c)
    st = _state_spec(bb, SUBLANES, MIX_W)
    consts = (w, cw, cb, wra, bra, wix, bix, lam)
    return pl.pallas_call(
        _lru_kernel,
        out_shape=(jax.ShapeDtypeStruct((b, l, MIX_W), F32), jax.ShapeDtypeStruct((b, SUBLANES, MIX_W), F32),
                   jax.ShapeDtypeStruct((b, SUBLANES, MIX_W), F32)),
        grid=(b // bb, l // c),
        in_specs=[x_spec] + [_full(a) for a in consts] + [st, st],
        out_specs=(y_spec, st, st),
        scratch_shapes=[pltpu.VMEM((bb, c + SUBLANES, MIX_W), F32), pltpu.VMEM((bb, SUBLANES, MIX_W), F32),
                        pltpu.VMEM((bb, SUBLANES, MIX_W), F32)],
        compiler_params=_cparams("parallel", "arbitrary"),
        name="mixer_rglru",
    )(x3, *consts, conv0, h0)


def _hgrn_kernel(x_ref, w_ref, wlo_ref, loglb_ref, log1mlb_ref, onemlb_ref, ng_ref, ones_ref, st0_ref,
                 y_ref, st1_ref, st_ref, q_s, k_s, i_s, b_s, qe_s, kd_s, el_s, o_s):
    kk = pl.program_id(1)
    bb, c, _ = x_ref.shape
    n = bb * c
    nblk = c // SUBLANES

    @pl.when(kk == 0)
    def _():
        st_ref[...] = st0_ref[...]

    xb, xlo = _split_bf16(x_ref[...].reshape(n, D_MODEL))
    proj = _mm(xb, w_ref[...])
    q = proj[:, 0:MIX_W] * HEAD_DIM ** -0.5
    z = proj[:, MIX_W:2 * MIX_W] + _mm(xlo, w_ref[:, MIX_W:2 * MIX_W]) + _mm(xb, wlo_ref[...])
    iv = proj[:, 2 * MIX_W:3 * MIX_W]
    cg = proj[:, 3 * MIX_W:4 * MIX_W]
    lo = loglb_ref[...]
    hi = log1mlb_ref[...] + _log_sigmoid(z)
    log_f = jnp.maximum(lo, hi) + jnp.log1p(jnp.exp(-jnp.abs(lo - hi)))
    kc = onemlb_ref[...] * _sigmoid(-z)
    pos = _chunk_pos(bb, c, MIX_W)
    bc = _seg_scan(log_f, pos, c, jnp.add, 0.0)
    b_last = _chunk_last(bc, pos, bb, c)
    ones_bd = ones_ref[...]

    sub = pos & (SUBLANES - 1)
    o_band = None
    for d in range(SUBLANES):
        kr = pltpu.roll(kc, d, 0) if d else kc
        br = pltpu.roll(bc, d, 0) if d else bc
        ir = pltpu.roll(iv, d, 0) if d else iv
        wd = jnp.where(sub >= d, q * kr * jnp.exp(jnp.minimum(bc - br, 0.0)), 0.0)
        term = _head_sum(wd, ones_bd) * ir
        o_band = term if o_band is None else o_band + term

    q_s[...] = q.reshape(bb, c, MIX_W)
    k_s[...] = kc.reshape(bb, c, MIX_W)
    i_s[...] = iv.reshape(bb, c, MIX_W)
    b_s[...] = bc.reshape(bb, c, MIX_W)
    qe_s[...] = (q * jnp.exp(bc)).reshape(bb, c, MIX_W)
    kd_s[...] = (kc * jnp.exp(b_last - bc)).reshape(bb, c, MIX_W)
    el_s[...] = jnp.exp(b_last).reshape(bb, c, MIX_W)

    hmasks = _head_masks()
    blockmask = _head_block_mask()
    pad_rows = LANES - c
    rowid = lax.broadcasted_iota(jnp.int32, (LANES, MIX_W), 0)
    zpad = jnp.zeros((pad_rows, MIX_W), F32)

    def per_bs(bs):
        nb = range(len(bs))
        sts = [st_ref[b] for b in bs]
        os_ = [_bmm(qe_s[b], sts[j], _NT) for j, b in enumerate(bs)]
        if nblk > 1:
            kp = [jnp.concatenate([k_s[b], zpad], axis=0) for b in bs]
            bp = [jnp.concatenate([b_s[b], zpad], axis=0) for b in bs]
            ip = [jnp.concatenate([i_s[b], zpad], axis=0).astype(BF16) for b in bs]
            pieces = [[jnp.zeros((SUBLANES, MIX_W), F32)] for _ in bs]
            for blk in range(1, nblk):
                r0 = blk * SUBLANES
                atts = []
                for j, b in enumerate(bs):
                    ref_b = b_s[b, r0 - 1:r0, :]
                    qi = q_s[b, r0:r0 + SUBLANES, :] * jnp.exp(
                        jnp.minimum(b_s[b, r0:r0 + SUBLANES, :] - ref_b, 0.0))
                    ki = jnp.where(rowid < r0, kp[j] * jnp.exp(jnp.minimum(ref_b - bp[j], 0.0)), 0.0)
                    qst = jnp.concatenate([jnp.where(m, qi, 0.0) for m in hmasks], axis=0)
                    atts.append(_bmm(qst, ki, _NT))
                ress = [_mm(atts[j].astype(BF16), ip[j]) for j in nb]
                for j in nb:
                    acc = None
                    for h in range(N_HEADS):
                        part = jnp.where(hmasks[h], ress[j][h * SUBLANES:(h + 1) * SUBLANES], 0.0)
                        acc = part if acc is None else acc + part
                    pieces[j].append(acc)
            os_ = [os_[j] + jnp.concatenate(pieces[j], axis=0) for j in nb]
        upds = [_mm3(jnp.concatenate([i_s[b], zpad], axis=0), jnp.concatenate([kd_s[b], zpad], axis=0), _TN)
                for b in bs]
        return [(os_[j], sts[j] * el_s[b, 0:1, :] + jnp.where(blockmask, upds[j], 0.0)) for j, b in enumerate(bs)]

    def commit(b, outs):
        o_s[b] = outs[0]
        st_ref[b] = outs[1]

    _paired_loop(bb, per_bs, commit, width=8)

    o = o_s[...].reshape(n, MIX_W) + o_band
    ms = _head_sum(o * o, ones_bd) * (1.0 / HEAD_DIM)
    y = o * lax.rsqrt(ms + NORM_EPS) * ng_ref[...] * _silu(cg)
    y_ref[...] = y.reshape(bb, c, MIX_W)

    @pl.when(kk == pl.num_programs(1) - 1)
    def _():
        st1_ref[...] = st_ref[...]


def _hgrn_call(x3, w, wlo, loglb, log1mlb, onemlb, ng, ones_bd, st0, bb, c):
    b, l, _ = x3.shape
    x_spec, y_spec = _seq_specs(bb, c)
    st = _state_spec(bb, MIX_W, MIX_W)
    consts = (w, wlo, loglb, log1mlb, onemlb, ng, ones_bd)
    rows = pltpu.VMEM((bb, c, MIX_W), F32)
    return pl.pallas_call(
        _hgrn_kernel,
        out_shape=(jax.ShapeDtypeStruct((b, l, MIX_W), F32), jax.ShapeDtypeStruct((b, MIX_W, MIX_W), F32)),
        grid=(b // bb, l // c),
        in_specs=[x_spec] + [_full(a) for a in consts] + [st],
        out_specs=(y_spec, st),
        scratch_shapes=[pltpu.VMEM((bb, MIX_W, MIX_W), F32)] + [rows] * 8,
        compiler_params=_cparams("parallel", "arbitrary"),
        name="mixer_hgrn2",
    )(x3, *consts, st0)


def _mlstm_kernel(g, cs, x_ref, w_ref, wlo_ref, cw_ref, cb_ref, wq_ref, wk_ref, bi_ref, bf_ref, ng_ref, ones_ref, exp_ref,
                  conv0_ref, c0_ref, n0_ref, m0_ref,
                  y_ref, conv1_ref, c1_ref, n1_ref, m1_ref,
                  ext_ref, prev_ref, cst_ref, nst_ref, mst_ref, q_s, k_s, v_s, kw_s, dec_s, a_s, g_s, num_s, den_s,
                  qc_s):
    kk = pl.program_id(1)
    bb, c, _ = x_ref.shape
    n = bb * c
    ngroups = bb // g

    @pl.when(kk == 0)
    def _():
        prev_ref[...] = conv0_ref[...]
        cst_ref[...] = c0_ref[...]
        nst_ref[...] = n0_ref[...]
        mst_ref[...] = m0_ref[...]

    xb, xlo = _split_bf16(x_ref[...].reshape(n, D_MODEL))
    proj = _mm(xb, w_ref[...])
    gates = proj[:, 3 * MIX_W:] + _mm(xlo, w_ref[:, 3 * MIX_W:]) + _mm(xb, wlo_ref[...])
    xm = _causal_conv(proj[:, 0:MIX_W].reshape(bb, c, MIX_W), ext_ref, prev_ref, cw_ref, c) + cb_ref[...]
    xm = _silu(xm).reshape(n, MIX_W).astype(BF16)
    q = _mm(xm, wq_ref[...])
    k = _mm(xm, wk_ref[...]) * HEAD_DIM ** -0.5
    v = proj[:, MIX_W:2 * MIX_W]
    d_o = proj[:, 2 * MIX_W:3 * MIX_W]
    log_i = gates[:, 0:LANES] + bi_ref[...]
    log_f = _log_sigmoid(gates[:, LANES:2 * LANES] + bf_ref[...])
    pos = _chunk_pos(bb, c, LANES)
    bc = _seg_scan(log_f, pos, c, jnp.add, 0.0)
    gg = log_i - bc
    cm = _seg_scan(gg, pos, c, jnp.maximum, -jnp.inf)
    m0 = jnp.broadcast_to(mst_ref[:, SUBLANES - 1:SUBLANES, :], (bb, c, LANES)).reshape(n, LANES)
    m_t = jnp.maximum(m0 + bc, bc + cm)
    b_last = _chunk_last(bc, pos, bb, c)
    m_last = _chunk_last(m_t, pos, bb, c)
    expand = exp_ref[...]
    ones_bd = ones_ref[...]
    inter = _to_hb(jnp.exp(m0 + bc - m_t), expand)
    e_negm = _to_hb(jnp.exp(-m_t), expand)
    wl = _to_hb(jnp.exp(gg + b_last - m_last), expand)
    nrows = jnp.broadcast_to(nst_ref[:, 0:1, :], (bb, c, MIX_W)).reshape(n, MIX_W)
    qn = _head_sum(q * nrows, ones_bd)

    q_s[...] = q.reshape(bb, c, MIX_W)
    k_s[...] = k.reshape(bb, c, MIX_W)
    v_s[...] = v.reshape(bb, c, MIX_W)
    kw_s[...] = (k * wl).reshape(bb, c, MIX_W)
    dec_s[...] = _to_hb(jnp.exp(m0 + b_last - m_last), expand).reshape(bb, c, MIX_W)
    a_s[...] = (bc - m_t).reshape(bb, c, LANES)
    g_s[...] = gg.reshape(bb, c, LANES)

    hmasks = _head_masks()
    causal, _ = _bd_masks(cs)
    rowb = lax.broadcasted_iota(jnp.int32, (BD, MIX_W), 0) // (N_HEADS * cs)
    sel = _stack(jnp.ones((g * c, MIX_W), F32), hmasks, g, c, cs)

    def per_groups(gis):
        ng = range(len(gis))
        qg = [_group_rows(q_s, gi, g, c) for gi in gis]
        qst = [_stack(qg[j], hmasks, g, c, cs).astype(BF16) for j in ng]
        kst = [_stack(_group_rows(k_s, gi, g, c), hmasks, g, c, cs).astype(BF16) for gi in gis]
        vst = [_stack(_group_rows(v_s, gi, g, c), hmasks, g, c, cs).astype(BF16) for gi in gis]
        kwst = [_stack(_group_rows(kw_s, gi, g, c), hmasks, g, c, cs) for gi in gis]
        acol = [_stack_cols(_group_rows(a_s, gi, g, c), g, c, cs)[0] for gi in gis]
        grow = [_stack_cols(_group_rows(g_s, gi, g, c), g, c, cs)[1] for gi in gis]
        qk = [_mm(qst[j], kst[j], _NT) for j in ng]
        cmats = [[cst_ref[gi * g + bl] for bl in range(g)] for gi in gis]
        qcs = [[_bmm(qg[j][bl * c:(bl + 1) * c], cmats[j][bl]) for bl in range(g)] for j in ng]
        qkw = [qk[j] * jnp.where(causal, jnp.exp(jnp.minimum(acol[j] + grow[j], 0.0)), 0.0) for j in ng]
        num = [_unstack(_mm(qkw[j].astype(BF16), vst[j]), g, c, cs) for j in ng]
        den = [_unstack(jnp.sum(qkw[j], axis=1, keepdims=True) * sel, g, c, cs) for j in ng]
        outs = []
        for j, gi in enumerate(gis):
            outs_g = []
            for bl in range(g):
                b = gi * g + bl
                kwb = kwst[j] if g == 1 else jnp.where(rowb == bl, kwst[j], 0.0)
                dec_row = dec_s[b, 0:1, :]
                ksum = jnp.sum(kw_s[b], axis=0, keepdims=True)
                outs_g.append((qcs[j][bl], num[j][bl * c:(bl + 1) * c], den[j][bl * c:(bl + 1) * c],
                               cmats[j][bl] * dec_row + _mm(kwb.astype(BF16), vst[j], _TN),
                               nst_ref[b] * dec_row + jnp.broadcast_to(ksum, (SUBLANES, MIX_W))))
            outs.append(outs_g)
        return outs

    def commit(gi, outs):
        for bl, (qc_b, num_b, den_b, c_b, n_b) in enumerate(outs):
            b = gi * g + bl
            qc_s[b] = qc_b
            num_s[b] = num_b
            den_s[b] = den_b
            cst_ref[b] = c_b
            nst_ref[b] = n_b

    _paired_loop(ngroups, per_groups, commit, width=8)

    num = num_s[...].reshape(n, MIX_W) + inter * qc_s[...].reshape(n, MIX_W)
    den = den_s[...].reshape(n, MIX_W) + inter * qn
    h = num / jnp.maximum(jnp.abs(den), e_negm)
    mu = _head_sum(h, ones_bd) * (1.0 / HEAD_DIM)
    hc = h - mu
    var = _head_sum(hc * hc, ones_bd) * (1.0 / HEAD_DIM)
    y = hc * lax.rsqrt(var + NORM_EPS) * ng_ref[...] * _sigmoid(d_o)
    y_ref[...] = y.reshape(bb, c, MIX_W)
    mst_ref[...] = m_t.reshape(bb, c, LANES)[:, c - SUBLANES:c, :]

    @pl.when(kk == pl.num_programs(1) - 1)
    def _():
        conv1_ref[...] = prev_ref[...]
        c1_ref[...] = cst_ref[...]
        n1_ref[...] = nst_ref[...]
        m1_ref[...] = mst_ref[...]


def _mlstm_call(x3, w, wlo, cw, cb, wq, wk, bi, bf, ng, ones_bd, expand, conv0, c0, n0, m0, bb, c, g, cs):
    b, l, _ = x3.shape
    x_spec, y_spec = _seq_specs(bb, c)
    st_conv = _state_spec(bb, SUBLANES, MIX_W)
    st_c = _state_spec(bb, MIX_W, MIX_W)
    st_n = _state_spec(bb, SUBLANES, MIX_W)
    st_m = _state_spec(bb, SUBLANES, LANES)
    consts = (w, wlo, cw, cb, wq, wk, bi, bf, ng, ones_bd, expand)
    rows = pltpu.VMEM((bb, c, MIX_W), F32)
    cols = pltpu.VMEM((bb, c, LANES), F32)
    return pl.pallas_call(
        functools.partial(_mlstm_kernel, g, cs),
        out_shape=(jax.ShapeDtypeStruct((b, l, MIX_W), F32), jax.ShapeDtypeStruct((b, SUBLANES, MIX_W), F32),
                   jax.ShapeDtypeStruct((b, MIX_W, MIX_W), F32), jax.ShapeDtypeStruct((b, SUBLANES, MIX_W), F32),
                   jax.ShapeDtypeStruct((b, SUBLANES, LANES), F32)),
        grid=(b // bb, l // c),
        in_specs=[x_spec] + [_full(a) for a in consts] + [st_conv, st_c, st_n, st_m],
        out_specs=(y_spec, st_conv, st_c, st_n, st_m),
        scratch_shapes=[pltpu.VMEM((bb, c + SUBLANES, MIX_W), F32), pltpu.VMEM((bb, SUBLANES, MIX_W), F32),
                        pltpu.VMEM((bb, MIX_W, MIX_W), F32), pltpu.VMEM((bb, SUBLANES, MIX_W), F32),
                        pltpu.VMEM((bb, SUBLANES, LANES), F32),
                        rows, rows, rows, rows, rows, cols, cols, rows, rows, rows],
        compiler_params=_cparams("parallel", "arbitrary"),
        name="mixer_mlstm",
    )(x3, *consts, conv0, c0, n0, m0)


def _gdn_kernel(g, cs, x_ref, w_ref, wlo_ref, cw_ref, alog_ref, dtb_ref, ng_ref, ones_ref, exp_ref, conv0_ref, s0_ref,
                y_ref, conv1_ref, s1_ref,
                ext_ref, prev_ref, sst_ref, q_s, k_s, vb_s, kb_s, qe_s, kd_s, el_s, gc_s, be_s, o_s):
    kk = pl.program_id(1)
    bb, c, _ = x_ref.shape
    n = bb * c
    ngroups = bb // g
    qkv_w = 3 * MIX_W

    @pl.when(kk == 0)
    def _():
        prev_ref[...] = conv0_ref[...]
        sst_ref[...] = s0_ref[...]

    xb, xlo = _split_bf16(x_ref[...].reshape(n, D_MODEL))
    proj = _mm(xb, w_ref[...])
    gates = proj[:, 4 * MIX_W:] + _mm(xlo, w_ref[:, 4 * MIX_W:]) + _mm(xb, wlo_ref[...])
    qkv = _causal_conv(proj[:, 0:qkv_w].reshape(bb, c, qkv_w), ext_ref, prev_ref, cw_ref, c)
    qkv = _silu(qkv).reshape(n, qkv_w)
    ones_bd = ones_ref[...]
    expand = exp_ref[...]
    q = qkv[:, 0:MIX_W]
    k = qkv[:, MIX_W:2 * MIX_W]
    v = qkv[:, 2 * MIX_W:3 * MIX_W]
    q = q * lax.rsqrt(_head_sum(q * q, ones_bd) + NORM_EPS) * HEAD_DIM ** -0.5
    k = k * lax.rsqrt(_head_sum(k * k, ones_bd) + NORM_EPS)
    z = proj[:, qkv_w:qkv_w + MIX_W]
    a_in = gates[:, 0:LANES]
    b_in = gates[:, LANES:2 * LANES]
    gdec = -jnp.exp(alog_ref[...]) * _softplus(a_in + dtb_ref[...])
    beta = _sigmoid(b_in)
    pos = _chunk_pos(bb, c, LANES)
    gcum = _seg_scan(gdec, pos, c, jnp.add, 0.0)
    g_last = _chunk_last(gcum, pos, bb, c)
    beta_hb = _to_hb(beta, expand)
    eg_hb = _to_hb(jnp.exp(gcum), expand)

    q_s[...] = q.reshape(bb, c, MIX_W)
    k_s[...] = k.reshape(bb, c, MIX_W)
    vb_s[...] = (v * beta_hb).reshape(bb, c, MIX_W)
    kb_s[...] = (k * beta_hb * eg_hb).reshape(bb, c, MIX_W)
    qe_s[...] = (q * eg_hb).reshape(bb, c, MIX_W)
    kd_s[...] = (k * _to_hb(jnp.exp(g_last - gcum), expand)).reshape(bb, c, MIX_W)
    el_s[...] = _to_hb(jnp.exp(g_last), expand).reshape(bb, c, MIX_W)
    gc_s[...] = gcum.reshape(bb, c, LANES)
    be_s[...] = beta.reshape(bb, c, LANES)

    hmasks = _head_masks()
    causal, strict = _bd_masks(cs)
    rowb = lax.broadcasted_iota(jnp.int32, (BD, MIX_W), 0) // (N_HEADS * cs)
    rows_b = N_HEADS * cs
    sel = _stack(jnp.ones((g * c, MIX_W), F32), hmasks, g, c, cs) > 0.0

    def per_groups(gis):
        ng = range(len(gis))
        qg = [_group_rows(qe_s, gi, g, c) for gi in gis]
        qst = [_stack(_group_rows(q_s, gi, g, c), hmasks, g, c, cs).astype(BF16) for gi in gis]
        kst = [_stack(_group_rows(k_s, gi, g, c), hmasks, g, c, cs).astype(BF16) for gi in gis]
        kdst = [_stack(_group_rows(kd_s, gi, g, c), hmasks, g, c, cs) for gi in gis]
        x = [_stack(_group_rows(vb_s, gi, g, c), hmasks, g, c, cs)
             + pltpu.roll(_stack(_group_rows(kb_s, gi, g, c), hmasks, g, c, cs), HEAD_DIM, 1) for gi in gis]
        gcr = [_stack_cols(_group_rows(gc_s, gi, g, c), g, c, cs) for gi in gis]
        bcol = [_stack_cols(_group_rows(be_s, gi, g, c), g, c, cs)[0] for gi in gis]
        decay = [jnp.exp(jnp.minimum(gcr[j][0] - gcr[j][1], 0.0)) for j in ng]
        kk = [_mm(kst[j], kst[j], _NT) for j in ng]
        p = [jnp.where(strict, bcol[j] * kk[j] * decay[j], 0.0) for j in ng]
        ax = [_bmm(p[j], x[j]) for j in ng]
        x = [x[j] - ax[j] for j in ng]
        span = 2
        while span < cs:
            p = [_bmm(p[j], p[j]) for j in ng]
            px = [_bmm(p[j], x[j]) for j in ng]
            x = [x[j] + px[j] for j in ng]
            span *= 2
        qk = [_mm(qst[j], kst[j], _NT) for j in ng]
        smats = [[sst_ref[gi * g + bl] for bl in range(g)] for gi in gis]
        w_v = [jnp.where(sel, x[j], 0.0) for j in ng]
        w_k = [jnp.where(sel, pltpu.roll(x[j], MIX_W - HEAD_DIM, 1), 0.0) for j in ng]
        ws = [[_bmm(w_k[j][bl * rows_b:(bl + 1) * rows_b], smats[j][bl]) for bl in range(g)] for j in ng]
        ub = []
        for j in ng:
            us = [w_v[j][bl * rows_b:(bl + 1) * rows_b] - ws[j][bl] for bl in range(g)]
            ub.append((us[0] if g == 1 else jnp.concatenate(us, axis=0)).astype(BF16))
        qkd = [jnp.where(causal, qk[j] * decay[j], 0.0).astype(BF16) for j in ng]
        o_intra = [_unstack(_mm(qkd[j], ub[j]), g, c, cs) for j in ng]
        outs = []
        for j, gi in enumerate(gis):
            outs_g = []
            for bl in range(g):
                b = gi * g + bl
                o_b = o_intra[j][bl * c:(bl + 1) * c] + _bmm(qg[j][bl * c:(bl + 1) * c], smats[j][bl])
                kdb = kdst[j] if g == 1 else jnp.where(rowb == bl, kdst[j], 0.0)
                outs_g.append((o_b, smats[j][bl] * el_s[b, 0:1, :] + _mm(kdb.astype(BF16), ub[j], _TN)))
            outs.append(outs_g)
        return outs

    def commit(gi, outs):
        for bl, (o_b, s_b) in enumerate(outs):
            o_s[gi * g + bl] = o_b
            sst_ref[gi * g + bl] = s_b

    _paired_loop(ngroups, per_groups, commit, width=4)

    o = o_s[...].reshape(n, MIX_W)
    ms = _head_sum(o * o, ones_bd) * (1.0 / HEAD_DIM)
    y = o * lax.rsqrt(ms + NORM_EPS) * ng_ref[...] * _silu(z)
    y_ref[...] = y.reshape(bb, c, MIX_W)

    @pl.when(kk == pl.num_programs(1) - 1)
    def _():
        conv1_ref[...] = prev_ref[...]
        s1_ref[...] = sst_ref[...]


def _gdn_call(x3, w, wlo, cw, alog, dtb, ng, ones_bd, expand, conv0, s0, bb, c, g, cs):
    b, l, _ = x3.shape
    x_spec, y_spec = _seq_specs(bb, c)
    st_conv = _state_spec(bb, SUBLANES, 3 * MIX_W)
    st_s = _state_spec(bb, MIX_W, MIX_W)
    consts = (w, wlo, cw, alog, dtb, ng, ones_bd, expand)
    rows = pltpu.VMEM((bb, c, MIX_W), F32)
    cols = pltpu.VMEM((bb, c, LANES), F32)
    return pl.pallas_call(
        functools.partial(_gdn_kernel, g, cs),
        out_shape=(jax.ShapeDtypeStruct((b, l, MIX_W), F32), jax.ShapeDtypeStruct((b, SUBLANES, 3 * MIX_W), F32),
                   jax.ShapeDtypeStruct((b, MIX_W, MIX_W), F32)),
        grid=(b // bb, l // c),
        in_specs=[x_spec] + [_full(a) for a in consts] + [st_conv, st_s],
        out_specs=(y_spec, st_conv, st_s),
        scratch_shapes=[pltpu.VMEM((bb, c + SUBLANES, 3 * MIX_W), F32), pltpu.VMEM((bb, SUBLANES, 3 * MIX_W), F32),
                        pltpu.VMEM((bb, MIX_W, MIX_W), F32),
                        rows, rows, rows, rows, rows, rows, rows, cols, cols, rows],
        compiler_params=_cparams("parallel", "arbitrary"),
        name="mixer_gdn",
    )(x3, *consts, conv0, s0)


def _pad_cols(a, width):
    return jnp.pad(a, ((0, 0), (0, width - a.shape[1])))


def _row(a, width=None):
    a = a.reshape(1, -1).astype(F32)
    return a if width is None else _pad_cols(a, width)


def _block_diag(w4):
    h, d, _ = w4.shape
    eye = jnp.eye(h, dtype=w4.dtype)
    return jnp.einsum("hij,hg->higj", w4, eye).reshape(h * d, h * d)


def _bd_state(s):
    b, h, d, _ = s.shape
    eye = jnp.eye(h, dtype=s.dtype)
    return jnp.einsum("bhij,hg->bhigj", s, eye).reshape(b, h * d, h * d)


def _bd_blocks(s):
    b = s.shape[0]
    s5 = s.reshape(b, N_HEADS, HEAD_DIM, N_HEADS, HEAD_DIM)
    return jnp.stack([s5[:, h, :, h, :] for h in range(N_HEADS)], axis=1)


def _tail8(buf):
    return jnp.pad(buf, ((0, 0), (SUBLANES - (CONV_W - 1), 0), (0, 0)))


def _bcast8(a):
    return jnp.broadcast_to(a[:, None, :], (a.shape[0], SUBLANES, a.shape[1]))


def _layer_params(P, lb, l):
    w_f32 = P["w_in"][l].astype(F32)
    w_in = w_f32.astype(BF16)
    w_lo = (w_f32 - w_in.astype(F32)).astype(BF16)

    def gate_tiles(w, lo, hi):
        return [_pad_cols(w[:, lo:lo + N_HEADS], LANES), _pad_cols(w[:, lo + N_HEADS:hi], LANES)]

    p = {}
    p["w_a"] = w_in[:, _A0:_B0]
    p["w_b"] = jnp.concatenate([w_in[:, _B0:_B0 + 4 * MIX_W]] + gate_tiles(w_in, _B0 + 4 * MIX_W, _C0), axis=1)
    p["w_b_lo"] = jnp.concatenate(gate_tiles(w_lo, _B0 + 4 * MIX_W, _C0), axis=1)
    p["w_c"] = w_in[:, _C0:_D0]
    p["w_c_lo"] = w_lo[:, _C0 + MIX_W:_C0 + 2 * MIX_W]
    p["w_d"] = jnp.concatenate([w_in[:, _D0:_D0 + 3 * MIX_W]] + gate_tiles(w_in, _D0 + 3 * MIX_W, _G0), axis=1)
    p["w_d_lo"] = jnp.concatenate(gate_tiles(w_lo, _D0 + 3 * MIX_W, _G0), axis=1)
    p["w_g"] = w_in[:, _G0:]
    p["lru_cw"] = P["w_lru_conv"][l]
    p["lru_cb"] = _row(P["b_lru_conv"][l])
    p["lru_wra"] = _block_diag(P["w_lru_ra"][l]).astype(BF16)
    p["lru_bra"] = _row(P["b_lru_ra"][l])
    p["lru_wix"] = _block_diag(P["w_lru_ix"][l]).astype(BF16)
    p["lru_bix"] = _row(P["b_lru_ix"][l])
    p["lru_lam"] = _row(P["lru_lambda"][l])
    p["gdn_cw"] = P["w_gdn_conv"][l]
    p["gdn_alog"] = _row(P["gdn_a_log"][l], LANES)
    p["gdn_dtb"] = _row(P["gdn_dt_bias"][l], LANES)
    p["gdn_ng"] = _row(jnp.tile(P["gdn_norm_g"][l], N_HEADS))
    p["hg_loglb"] = _row(jnp.log(lb[l]))
    p["hg_log1mlb"] = _row(jnp.log1p(-lb[l]))
    p["hg_onemlb"] = _row(1.0 - lb[l])
    p["hg_ng"] = _row(jnp.tile(P["hg_norm_g"][l], N_HEADS))
    p["ml_cw"] = P["w_ml_conv"][l]
    p["ml_cb"] = _row(P["b_ml_conv"][l])
    p["ml_wq"] = _block_diag(P["w_ml_q"][l]).astype(BF16)
    p["ml_wk"] = _block_diag(P["w_ml_k"][l]).astype(BF16)
    p["ml_bi"] = _row(P["b_ml_i"][l], LANES)
    p["ml_bf"] = _row(P["b_ml_f"][l], LANES)
    p["ml_ng"] = _row(P["ml_norm_g"][l])
    p["w_br"] = P["w_branch"][l].astype(BF16)
    p["w_out"] = P["w_out"][l].astype(BF16)
    p["ln1_g"] = _row(P["ln1_g"][l])
    p["ln1_b"] = _row(P["ln1_b"][l])
    p["ln2_g"] = _row(P["ln2_g"][l])
    p["ln2_b"] = _row(P["ln2_b"][l])
    p["w_r"] = _pad_cols(P["w_router"][l], LANES).astype(BF16)
    p["b_r"] = jnp.concatenate([P["b_router"][l].astype(F32), jnp.full((LANES - N_EXPERTS,), -1e30, F32)]).reshape(1, LANES)
    return p


def _mixers(x3, p, st, consts, bb, c, g, cs):
    ones_bd, expand = consts
    lru_conv, lru_h, gdn_conv, gdn_s, hg_st, ml_conv, ml_c, ml_n, ml_m = st
    y_a, lru_conv1, lru_h1 = _lru_call(x3, p["w_a"], p["lru_cw"], p["lru_cb"], p["lru_wra"], p["lru_bra"],
                                       p["lru_wix"], p["lru_bix"], p["lru_lam"], lru_conv, lru_h, bb, c)
    y_b, gdn_conv1, gdn_s1 = _gdn_call(x3, p["w_b"], p["w_b_lo"], p["gdn_cw"], p["gdn_alog"], p["gdn_dtb"], p["gdn_ng"],
                                       ones_bd, expand, gdn_conv, gdn_s, bb, c, g, cs)
    y_c, hg_st1 = _hgrn_call(x3, p["w_c"], p["w_c_lo"], p["hg_loglb"], p["hg_log1mlb"], p["hg_onemlb"], p["hg_ng"], ones_bd,
                             hg_st, bb, c)
    y_d, ml_conv1, ml_c1, ml_n1, ml_m1 = _mlstm_call(x3, p["w_d"], p["w_d_lo"], p["ml_cw"], p["ml_cb"], p["ml_wq"], p["ml_wk"],
                                                     p["ml_bi"], p["ml_bf"], p["ml_ng"], ones_bd, expand,
                                                     ml_conv, ml_c, ml_n, ml_m, bb, c, g, cs)
    return (y_a, y_b, y_c, y_d), (lru_conv1, lru_h1, gdn_conv1, gdn_s1, hg_st1, ml_conv1, ml_c1, ml_n1, ml_m1)


def _states_to_kernel(st):
    lru_conv, lru_h, gdn_conv, gdn_s, hg_s, ml_conv, ml_c, ml_n, ml_m = [s.astype(F32) for s in st]
    b = lru_h.shape[0]
    return (_tail8(lru_conv), _bcast8(lru_h), _tail8(gdn_conv), _bd_state(gdn_s),
            jnp.swapaxes(_bd_state(hg_s), 1, 2), _tail8(ml_conv), _bd_state(ml_c),
            _bcast8(ml_n.reshape(b, MIX_W)), _bcast8(_pad_cols(ml_m, LANES)))


def _states_from_kernel(st):
    lru_conv, lru_h, gdn_conv, gdn_s, hg_st, ml_conv, ml_c, ml_n, ml_m = st
    b = lru_h.shape[0]
    tail = SUBLANES - (CONV_W - 1)
    return (lru_conv[:, tail:], lru_h[:, SUBLANES - 1], gdn_conv[:, tail:], _bd_blocks(gdn_s),
            _bd_blocks(jnp.swapaxes(hg_st, 1, 2)), ml_conv[:, tail:], _bd_blocks(ml_c),
            ml_n[:, 0].reshape(b, N_HEADS, HEAD_DIM), ml_m[:, SUBLANES - 1, :N_HEADS])


def _route(top_e):
    t = top_e.shape[0]
    m = t * TOP_K
    flat_e = top_e.reshape(m)
    onehot = (flat_e[:, None] == jnp.arange(N_EXPERTS, dtype=jnp.int32)[None, :]).astype(jnp.int32)
    csum = jnp.cumsum(onehot, axis=0)
    rank = jnp.sum(onehot * csum, axis=1) - 1
    counts = csum[-1]
    padded = (counts + MOE_BLOCK_ROWS - 1) // MOE_BLOCK_ROWS * MOE_BLOCK_ROWS
    pad_ends = jnp.cumsum(padded)
    dest = (pad_ends - padded)[flat_e] + rank
    n_blocks = m // MOE_BLOCK_ROWS + N_EXPERTS
    mp = n_blocks * MOE_BLOCK_ROWS
    block_start = jnp.arange(n_blocks, dtype=jnp.int32) * MOE_BLOCK_ROWS
    block_e = jnp.minimum(jnp.sum((pad_ends[None, :] <= block_start[:, None]).astype(jnp.int32), axis=1),
                          N_EXPERTS - 1)
    n_used = (pad_ends[-1] // MOE_BLOCK_ROWS).astype(jnp.int32).reshape(1)
    n_pad = padded - counts
    j = jnp.arange(MOE_BLOCK_ROWS, dtype=jnp.int32)[None, :]
    is_pad = (j < n_pad[:, None]).reshape(-1)
    pad_row = ((pad_ends - n_pad)[:, None] + j).reshape(-1)
    tail_rank = jnp.cumsum(jnp.logical_not(is_pad).astype(jnp.int32)) - 1
    zero_rows = jnp.where(is_pad, pad_row, pad_ends[-1] + tail_rank).astype(jnp.int32)
    return dest.reshape(t, TOP_K), zero_rows, block_e, n_used, mp


def _dest_tiles(dest):
    t = dest.shape[0]
    n = t // COMBINE_ROWS
    return jnp.swapaxes(dest.reshape(n, COMBINE_ROWS, TOP_K), 1, 2).reshape(n, 1, TOP_K * COMBINE_ROWS)


def _trunk_layer(layer, xs, states, p, moe_w, consts, cfgs):
    x1s, tes, tgs, new_states = [], [], [], []
    for x3, st, (bb, c, g, cs, tm) in zip(xs, states, cfgs):
        b, l, _ = x3.shape
        ys, st1 = _mixers(x3, p, st, consts, bb, c, g, cs)
        x1, te, tg = _merge_call(x3.reshape(b * l, D_MODEL), [y.reshape(b * l, MIX_W) for y in ys],
                                 p["w_g"], p["w_br"], p["w_out"], p["ln1_g"], p["ln1_b"], p["w_r"], p["b_r"], tm)
        x1s.append(x1)
        tes.append(te[:, :TOP_K])
        tgs.append(tg)
        new_states.append(st1)
    dest, zero_rows, block_e, n_used, mp = _route(jnp.concatenate(tes, axis=0))
    d_tiles = []
    off = 0
    for x1 in x1s:
        d_tiles.append(_dest_tiles(lax.slice_in_dim(dest, off, off + x1.shape[0], axis=0)))
        off += x1.shape[0]
    z_tiles = zero_rows.reshape(-1, 1, TOP_K * COMBINE_ROWS)
    rows = _dispatch_call(x1s[0], x1s[1], jnp.concatenate(d_tiles + [z_tiles], axis=0), mp)
    ys = _moe_call(layer, block_e, n_used, rows, *moe_w)
    outs = []
    for x3, x1, tg, dt in zip(xs, x1s, tgs, d_tiles):
        outs.append(_combine_ln2_call(x1, tg, dt, ys, p["ln2_g"], p["ln2_b"]).reshape(x3.shape))
    return outs, new_states


def _row_tile(t):
    for tm in (512, 384, 256, 128, 64, 32, 16):
        if t % tm == 0:
            return tm
    raise ValueError(f"no row tile for {t} rows")


def _group_cfg(b, l):
    if l % PROMPT_CHUNK == 0:
        return (min(b, 8), PROMPT_CHUNK, 1, BD // N_HEADS, _row_tile(b * l))
    assert l == SUBLANES, "sequence length must be a multiple of the prompt chunk or one sublane tile"
    g = BD // (N_HEADS * SUBLANES)
    bb = min(b, 16)
    assert bb % g == 0 and b % bb == 0
    return (bb, SUBLANES, g, SUBLANES, _row_tile(b * l))


def _zero_states(b):
    return (jnp.zeros((b, CONV_W - 1, MIX_W), F32), jnp.zeros((b, MIX_W), F32),
            jnp.zeros((b, CONV_W - 1, 3 * MIX_W), F32), jnp.zeros((b, N_HEADS, HEAD_DIM, HEAD_DIM), F32),
            jnp.zeros((b, N_HEADS, HEAD_DIM, HEAD_DIM), F32), jnp.zeros((b, CONV_W - 1, MIX_W), F32),
            jnp.zeros((b, N_HEADS, HEAD_DIM, HEAD_DIM), F32), jnp.zeros((b, N_HEADS, HEAD_DIM), F32),
            jnp.zeros((b, N_HEADS), F32))


def kernel(x_prompt, x_sample, state_lru_conv, state_lru_h, state_gdn_conv, state_gdn, state_hgrn, state_mlstm_conv, state_mlstm_c, state_mlstm_n, state_mlstm_m, meta_tokens, ln_emb_g, ln_emb_b, hg_lb_logits, w_in, w_lru_conv, b_lru_conv, w_lru_ra, b_lru_ra, w_lru_ix, b_lru_ix, lru_lambda, w_gdn_conv, gdn_a_log, gdn_dt_bias, gdn_norm_g, hg_norm_g, w_ml_conv, b_ml_conv, w_ml_q, w_ml_k, b_ml_i, b_ml_f, ml_norm_g, w_branch, w_out, ln1_g, ln1_b, ln2_g, ln2_b, w_router, b_router, w_up, b_up, w_down, b_down):
    P = dict(w_in=w_in, w_lru_conv=w_lru_conv, b_lru_conv=b_lru_conv, w_lru_ra=w_lru_ra, b_lru_ra=b_lru_ra,
             w_lru_ix=w_lru_ix, b_lru_ix=b_lru_ix, lru_lambda=lru_lambda, w_gdn_conv=w_gdn_conv,
             gdn_a_log=gdn_a_log, gdn_dt_bias=gdn_dt_bias, gdn_norm_g=gdn_norm_g, hg_norm_g=hg_norm_g,
             w_ml_conv=w_ml_conv, b_ml_conv=b_ml_conv, w_ml_q=w_ml_q, w_ml_k=w_ml_k, b_ml_i=b_ml_i,
             b_ml_f=b_ml_f, ml_norm_g=ml_norm_g, w_branch=w_branch, w_out=w_out, ln1_g=ln1_g, ln1_b=ln1_b,
             ln2_g=ln2_g, ln2_b=ln2_b, w_router=w_router, b_router=b_router, w_up=w_up, b_up=b_up,
             w_down=w_down, b_down=b_down)
    depth = w_in.shape[0]
    lb_cum = jnp.cumsum(jax.nn.softmax(hg_lb_logits.astype(F32), axis=0), axis=0)
    lb = lb_cum - lb_cum[0:1]

    lane = jnp.arange(MIX_W) // HEAD_DIM
    ones_bd = (lane[:, None] == lane[None, :]).astype(BF16)
    expand = (jnp.arange(LANES)[:, None] == lane[None, :]).astype(BF16)
    consts = (ones_bd, expand)
    moe_w = (w_up.astype(F32), b_up.reshape(depth, N_EXPERTS, 1, 2 * D_FF).astype(F32),
             w_down.astype(F32), b_down.reshape(depth, N_EXPERTS, 1, D_MODEL).astype(F32))

    bp, lp0, _ = x_prompt.shape
    bs, ls, _ = x_sample.shape
    meta = jnp.broadcast_to(meta_tokens.astype(F32)[None], (bp, N_META, D_MODEL))
    xp_in = jnp.concatenate([meta, x_prompt], axis=1)
    lp = lp0 + N_META
    cfgs = [_group_cfg(bp, lp), _group_cfg(bs, ls)]
    g_emb, b_emb = _row(ln_emb_g), _row(ln_emb_b)
    xp = _ln_call(xp_in.reshape(bp * lp, D_MODEL), g_emb, b_emb, cfgs[0][4]).reshape(bp, lp, D_MODEL)
    xs_ = _ln_call(x_sample.reshape(bs * ls, D_MODEL), g_emb, b_emb, cfgs[1][4]).reshape(bs, ls, D_MODEL)

    sample_states = (state_lru_conv, state_lru_h, state_gdn_conv, state_gdn, state_hgrn,
                     state_mlstm_conv, state_mlstm_c, state_mlstm_n, state_mlstm_m)
    zero_p = _zero_states(bp)
    xs = [xp, xs_]
    collected = ([], [])
    for l in range(depth):
        p = _layer_params(P, lb, l)
        states = [_states_to_kernel(zero_p), _states_to_kernel(tuple(s[l] for s in sample_states))]
        xs, new_states = _trunk_layer(l, xs, states, p, moe_w, consts, cfgs)
        for grp in range(2):
            collected[grp].append(_states_from_kernel(new_states[grp]))
    new_p = tuple(jnp.stack([layer[i] for layer in collected[0]]) for i in range(9))
    new_s = tuple(jnp.stack([layer[i] for layer in collected[1]]) for i in range(9))
    y_prompt = xs[0][:, N_META:]
    y_sample = xs[1]
    out = [y_prompt, y_sample]
    for i in range(9):
        out.append(new_p[i])
        out.append(new_s[i])
    return tuple(out)
```

```python
import functools

import jax
import jax.numpy as jnp
from jax import lax
from jax.experimental import pallas as pl
from jax.experimental.pallas import tpu as pltpu

F32 = jnp.float32
BF16 = jnp.bfloat16
HIGHEST = lax.Precision.HIGHEST

D_MODEL = 1024
DEPTH = 4
N_META = 16
N_BRANCH = 4
MIX_W = 256
N_HEADS = 4
HEAD_DIM = 64
CONV_W = 4
LRU_C = 8.0
N_EXPERTS = 32
TOP_K = 4
D_FF = 1024
SWIGLU_LIMIT = 7.0
SWIGLU_ALPHA = 1.702
LN_EPS = 1e-5
NORM_EPS = 1e-6
DEEPNORM_ALPHA = (2 * DEPTH) ** 0.25

LANES = 128
SUBLANES = 8
BD = 256
VMEM_LIMIT_BYTES = 56 * 1024 * 1024
PROMPT_CHUNK = 48
MOE_BLOCK_ROWS = 256
COMBINE_ROWS = 128

_A0, _B0, _C0, _D0, _G0 = 0, 512, 1544, 2568, 3344

_NN = (((1,), (0,)), ((), ()))
_NT = (((1,), (1,)), ((), ()))
_TN = (((0,), (0,)), ((), ()))


def _mm(a, b, dims=_NN, precision=None):
    return lax.dot_general(a, b, dims, precision=precision, preferred_element_type=F32)


def _bmm(a, b, dims=_NN):
    return _mm(a.astype(BF16), b.astype(BF16), dims)


def _split_bf16(x):
    hi = x.astype(BF16)
    return hi, (x - hi.astype(F32)).astype(BF16)


def _mm3(a, b, dims=_NN):
    ah, al = _split_bf16(a)
    bh, bl = _split_bf16(b)
    return _mm(ah, bh, dims) + (_mm(al, bh, dims) + _mm(ah, bl, dims))


def _mm_sel(x, sel, dims=_NN, terms=2):
    out = None
    r = x
    for _ in range(terms):
        p = r.astype(BF16)
        t = _mm(p, sel, dims)
        out = t if out is None else out + t
        r = r - p.astype(F32)
    return out


def _sigmoid(x):
    return jax.nn.sigmoid(x)


def _softplus(x):
    return jnp.maximum(x, 0.0) + jnp.log1p(jnp.exp(-jnp.abs(x)))


def _log_sigmoid(x):
    return jnp.minimum(x, 0.0) - jnp.log1p(jnp.exp(-jnp.abs(x)))


def _expm1(x):
    return jnp.tanh(0.5 * x) * (jnp.exp(x) + 1.0)


def _silu(x):
    return x * _sigmoid(x)


def _gelu_tanh(x):
    return 0.5 * x * (1.0 + jnp.tanh(0.7978845608028654 * (x + 0.044715 * (x * x * x))))


def _layernorm(x, g, b):
    mu = jnp.mean(x, axis=-1, keepdims=True)
    xc = x - mu
    var = jnp.mean(xc * xc, axis=-1, keepdims=True)
    return xc * lax.rsqrt(var + LN_EPS) * g + b


def _cparams(*sem):
    return pltpu.CompilerParams(dimension_semantics=sem, vmem_limit_bytes=VMEM_LIMIT_BYTES)


def _full(a):
    nd = a.ndim
    return pl.BlockSpec(a.shape, lambda *_: (0,) * nd)


def _ln_kernel(x_ref, g_ref, b_ref, o_ref):
    o_ref[...] = _layernorm(x_ref[...], g_ref[...], b_ref[...])


def _ln_call(x2, g, b, tm):
    t, d = x2.shape
    return pl.pallas_call(
        _ln_kernel,
        out_shape=jax.ShapeDtypeStruct((t, d), F32),
        grid=(t // tm,),
        in_specs=[pl.BlockSpec((tm, d), lambda i: (i, 0)), _full(g), _full(b)],
        out_specs=pl.BlockSpec((tm, d), lambda i: (i, 0)),
        compiler_params=_cparams("parallel"),
        name="ln_rows",
    )(x2, g, b)


def _combine_ln2_kernel(n, dcur_ref, dnxt_ref, x_ref, gate_ref, g_ref, b_ref, ys_hbm, o_ref, buf_ref, sem_ref):
    i = pl.program_id(0)
    tm = x_ref.shape[0]
    nrow = TOP_K * tm

    def row_copy(d_ref, r, slot):
        return pltpu.make_async_copy(ys_hbm.at[pl.ds(d_ref[0, 0, r], 1)], buf_ref.at[slot, pl.ds(r, 1)],
                                     sem_ref.at[slot])

    def issue(d_ref, slot):
        for r in range(nrow):
            row_copy(d_ref, r, slot).start()

    @pl.when(i == 0)
    def _():
        issue(dcur_ref, 0)

    @pl.when(i + 1 < n)
    def _():
        issue(dnxt_ref, (i + 1) % 2)

    slot = i % 2
    pltpu.make_async_copy(ys_hbm.at[pl.ds(0, nrow)], buf_ref.at[slot], sem_ref.at[slot]).wait()
    gate = gate_ref[...]
    ff = None
    for k in range(TOP_K):
        term = buf_ref[slot, pl.ds(k * tm, tm), :] * gate[:, k:k + 1]
        ff = term if ff is None else ff + term
    o_ref[...] = _layernorm(DEEPNORM_ALPHA * x_ref[...] + ff, g_ref[...], b_ref[...])


def _combine_ln2_call(x2, gates, dest_tiles, ys, g, b):
    t, d = x2.shape
    tm = COMBINE_ROWS
    n = t // tm
    row = pl.BlockSpec((tm, d), lambda i: (i, 0))
    dspec = lambda f: pl.BlockSpec((1, 1, TOP_K * tm), f, memory_space=pltpu.SMEM)
    return pl.pallas_call(
        functools.partial(_combine_ln2_kernel, n),
        out_shape=jax.ShapeDtypeStruct((t, d), F32),
        grid=(n,),
        in_specs=[dspec(lambda i: (i, 0, 0)), dspec(lambda i: (jnp.minimum(i + 1, n - 1), 0, 0)), row,
                  pl.BlockSpec((tm, LANES), lambda i: (i, 0)), _full(g), _full(b),
                  pl.BlockSpec(memory_space=pl.ANY)],
        out_specs=row,
        scratch_shapes=[pltpu.VMEM((2, TOP_K * tm, d), F32), pltpu.SemaphoreType.DMA((2,))],
        compiler_params=_cparams("arbitrary"),
        name="combine_ln2",
    )(dest_tiles, dest_tiles, x2, gates, g, b, ys)


def _merge_kernel(x_ref, ya_ref, yb_ref, yc_ref, yd_ref, wg_ref, wbr_ref, wout_ref, g_ref, b_ref, wr_ref, br_ref,
                  x1_ref, te_ref, tg_ref):
    x = x_ref[...]
    xb = x.astype(BF16)
    merged = None
    for n, y_ref in enumerate((ya_ref, yb_ref, yc_ref, yd_ref)):
        up = _mm(y_ref[...].astype(BF16), wbr_ref[n])
        gate = _sigmoid(_mm(xb, wg_ref[:, n * D_MODEL:(n + 1) * D_MODEL]))
        merged = up * gate if merged is None else merged + up * gate
    out = _mm(merged.astype(BF16), wout_ref[...])
    x1 = _layernorm(DEEPNORM_ALPHA * x + out, g_ref[...], b_ref[...])
    x1_ref[...] = x1
    logits = _mm(x1.astype(BF16), wr_ref[...]) + br_ref[...]
    lane = lax.broadcasted_iota(jnp.int32, logits.shape, 1)
    vals = []
    idxs = []
    v = logits
    for _ in range(TOP_K):
        m = jnp.max(v, axis=-1, keepdims=True)
        idx = jnp.min(jnp.where(v == m, lane, LANES), axis=-1, keepdims=True)
        vals.append(m)
        idxs.append(idx)
        v = jnp.where(lane == idx, -jnp.inf, v)
    es = [jnp.exp(m - vals[0]) for m in vals]
    den = es[0] + es[1] + es[2] + es[3]
    te = jnp.zeros(logits.shape, jnp.int32)
    tg = jnp.zeros(logits.shape, F32)
    for j in range(TOP_K):
        te = jnp.where(lane == j, idxs[j], te)
        tg = jnp.where(lane == j, es[j] / den, tg)
    te_ref[...] = te
    tg_ref[...] = tg


def _merge_call(x2, ys, wg, wbr, wout, g, b, wr, br, tm):
    t, d = x2.shape
    row = pl.BlockSpec((tm, d), lambda i: (i, 0))
    yrow = pl.BlockSpec((tm, MIX_W), lambda i: (i, 0))
    lrow = pl.BlockSpec((tm, LANES), lambda i: (i, 0))
    return pl.pallas_call(
        _merge_kernel,
        out_shape=(jax.ShapeDtypeStruct((t, d), F32),
                   jax.ShapeDtypeStruct((t, LANES), jnp.int32), jax.ShapeDtypeStruct((t, LANES), F32)),
        grid=(t // tm,),
        in_specs=[row, yrow, yrow, yrow, yrow, _full(wg), _full(wbr), _full(wout), _full(g), _full(b),
                  _full(wr), _full(br)],
        out_specs=(row, lrow, lrow),
        compiler_params=_cparams("parallel"),
        name="merge_ln_router",
    )(x2, *ys, wg, wbr, wout, g, b, wr, br)


def _dispatch_kernel(n_p, n_s, n, d_ref, xp_ref, xs_ref, o_hbm, buf_ref, sem_ref):
    i = pl.program_id(0)
    tm = xp_ref.shape[0]
    nrow = TOP_K * tm
    slot = i % 2

    def wait_slot(s):
        pltpu.make_async_copy(o_hbm.at[pl.ds(0, nrow)], o_hbm.at[pl.ds(0, nrow)], sem_ref.at[s]).wait()

    @pl.when(i >= 2)
    def _():
        wait_slot(slot)

    @pl.when(i < n_p)
    def _():
        buf_ref[slot] = xp_ref[...]

    @pl.when((i >= n_p) & (i < n_p + n_s))
    def _():
        buf_ref[slot] = xs_ref[...]

    @pl.when(i >= n_p + n_s)
    def _():
        buf_ref[slot] = jnp.zeros((tm, D_MODEL), F32)

    def row_copy(k, t):
        return pltpu.make_async_copy(buf_ref.at[slot, pl.ds(t, 1)], o_hbm.at[pl.ds(d_ref[0, 0, k * tm + t], 1)],
                                     sem_ref.at[slot])

    for k in range(TOP_K):
        for t in range(tm):
            row_copy(k, t).start()

    @pl.when(i == n - 1)
    def _():
        if n >= 2:
            wait_slot(1 - slot)
        wait_slot(slot)


def _dispatch_call(x_p, x_s, dest_tiles, mp):
    tm = COMBINE_ROWS
    d = x_p.shape[1]
    n_p, n_s, n = x_p.shape[0] // tm, x_s.shape[0] // tm, dest_tiles.shape[0]
    return pl.pallas_call(
        functools.partial(_dispatch_kernel, n_p, n_s, n),
        out_shape=jax.ShapeDtypeStruct((mp, d), F32),
        grid=(n,),
        in_specs=[pl.BlockSpec((1, 1, TOP_K * tm), lambda i: (i, 0, 0), memory_space=pltpu.SMEM),
                  pl.BlockSpec((tm, d), lambda i: (jnp.minimum(i, n_p - 1), 0)),
                  pl.BlockSpec((tm, d), lambda i: (jnp.clip(i - n_p, 0, n_s - 1), 0))],
        out_specs=pl.BlockSpec(memory_space=pl.ANY),
        scratch_shapes=[pltpu.VMEM((2, tm, d), F32), pltpu.SemaphoreType.DMA((2,))],
        compiler_params=_cparams("arbitrary"),
        name="moe_dispatch",
    )(dest_tiles, x_p, x_s)


def _moe_kernel(be_ref, nused_ref, xs_ref, wu_ref, bu_ref, wd_ref, bd_ref, o_ref, wub_ref, wdb_ref):
    i = pl.program_id(0)

    @pl.when((i == 0) | (be_ref[i] != be_ref[jnp.maximum(i - 1, 0)]))
    def _():
        wub_ref[...] = wu_ref[0, 0].astype(BF16)
        wdb_ref[...] = wd_ref[0, 0].astype(BF16)

    @pl.when(i < nused_ref[0])
    def _():
        h = _mm(xs_ref[...].astype(BF16), wub_ref[...]) + bu_ref[0, 0]
        g = jnp.minimum(h[:, :D_FF], SWIGLU_LIMIT)
        u = jnp.clip(h[:, D_FF:], -SWIGLU_LIMIT, SWIGLU_LIMIT)
        act = (u + 1.0) * g * _sigmoid(SWIGLU_ALPHA * g)
        o_ref[...] = _mm(act.astype(BF16), wdb_ref[...]) + bd_ref[0, 0]

    @pl.when(i >= nused_ref[0])
    def _():
        o_ref[...] = jnp.zeros(o_ref.shape, F32)


def _moe_call(layer, block_e, n_used, xs, wu, bu, wd, bd):
    mp, d = xs.shape
    nb = mp // MOE_BLOCK_ROWS
    grid_spec = pltpu.PrefetchScalarGridSpec(
        num_scalar_prefetch=2,
        grid=(nb,),
        in_specs=[
            pl.BlockSpec((MOE_BLOCK_ROWS, d), lambda i, be, nu: (i, 0)),
            pl.BlockSpec((1, 1, d, 2 * D_FF), lambda i, be, nu: (layer, be[i], 0, 0)),
            pl.BlockSpec((1, 1, 1, 2 * D_FF), lambda i, be, nu: (layer, be[i], 0, 0)),
            pl.BlockSpec((1, 1, D_FF, d), lambda i, be, nu: (layer, be[i], 0, 0)),
            pl.BlockSpec((1, 1, 1, d), lambda i, be, nu: (layer, be[i], 0, 0)),
        ],
        out_specs=pl.BlockSpec((MOE_BLOCK_ROWS, d), lambda i, be, nu: (i, 0)),
        scratch_shapes=[pltpu.VMEM((d, 2 * D_FF), BF16), pltpu.VMEM((D_FF, d), BF16)],
    )
    return pl.pallas_call(
        _moe_kernel,
        out_shape=jax.ShapeDtypeStruct((mp, d), F32),
        grid_spec=grid_spec,
        compiler_params=_cparams("arbitrary"),
        name="moe_experts",
    )(block_e, n_used, xs, wu, bu, wd, bd)


def _chunk_pos(bb, c, width):
    return lax.broadcasted_iota(jnp.int32, (bb, c, width), 1).reshape(bb * c, width)


def _seg_scan(x, pos, c, op, ident):
    s = 1
    while s < c:
        x = op(x, jnp.where(pos >= s, pltpu.roll(x, s, 0), ident))
        s *= 2
    return x


def _chunk_last(x, pos, bb, c):
    w = x.shape[1]
    x3 = jnp.where(pos == c - 1, x, 0.0).reshape(bb, c, w)
    return jnp.broadcast_to(jnp.sum(x3, axis=1, keepdims=True), (bb, c, w)).reshape(bb * c, w)


def _causal_conv(x3, ext_ref, prev_ref, cw_ref, c):
    ext_ref[:, 0:SUBLANES, :] = prev_ref[...]
    ext_ref[:, SUBLANES:SUBLANES + c, :] = x3
    y = None
    for j in range(CONV_W):
        lo = SUBLANES - (CONV_W - 1) + j
        tap = ext_ref[:, lo:lo + c, :] * cw_ref[j:j + 1, :]
        y = tap if y is None else y + tap
    prev_ref[...] = ext_ref[:, c:c + SUBLANES, :]
    return y


def _head_masks(width=MIX_W):
    lane = lax.broadcasted_iota(jnp.int32, (1, width), 1)
    return [(lane >= h * HEAD_DIM) & (lane < (h + 1) * HEAD_DIM) for h in range(N_HEADS)]


def _lane_masks():
    lane = lax.broadcasted_iota(jnp.int32, (1, LANES), 1)
    return [lane == h for h in range(N_HEADS)]


def _stack(xg, masks, g, c, cs):
    pieces = []
    for b in range(g):
        xb = xg[b * c:(b + 1) * c]
        for m in masks:
            pieces.append(jnp.where(m, xb, 0.0))
            if cs > c:
                pieces.append(jnp.zeros((cs - c, xg.shape[1]), xg.dtype))
    return jnp.concatenate(pieces, axis=0)


def _unstack(y, g, c, cs):
    outs = []
    for b in range(g):
        acc = None
        for h in range(N_HEADS):
            r0 = (b * N_HEADS + h) * cs
            piece = y[r0:r0 + c]
            acc = piece if acc is None else acc + piece
        outs.append(acc)
    return outs[0] if g == 1 else jnp.concatenate(outs, axis=0)


def _bd_masks(cs):
    r = lax.broadcasted_iota(jnp.int32, (BD, BD), 0)
    q = lax.broadcasted_iota(jnp.int32, (BD, BD), 1)
    shift = cs.bit_length() - 1
    same = (r >> shift) == (q >> shift)
    tr = r & (cs - 1)
    tq = q & (cs - 1)
    return same & (tq <= tr), same & (tq < tr)


def _head_block_mask():
    r = lax.broadcasted_iota(jnp.int32, (MIX_W, MIX_W), 0)
    q = lax.broadcasted_iota(jnp.int32, (MIX_W, MIX_W), 1)
    return (r // HEAD_DIM) == (q // HEAD_DIM)


def _head_sum(x, ones_bd):
    return _mm_sel(x, ones_bd)


def _to_hb(cols, expand):
    return _mm_sel(cols, expand, terms=2)


def _stack_cols(cols_g, g, c, cs):
    st = _stack(cols_g, _lane_masks(), g, c, cs)
    col = jnp.sum(st, axis=1, keepdims=True)
    ones = jnp.ones((SUBLANES, LANES), F32)
    row = _mm(ones, st, _NT, precision=HIGHEST)[0:1, :]
    return col, row


def _group_rows(ref, gi, g, c):
    w = ref.shape[2]
    if g == 1:
        return ref[gi]
    return ref[pl.ds(gi * g, g)].reshape(g * c, w)


def _paired_loop(n, compute_many, commit, width=2):
    while n % width:
        width //= 2

    def trip(j, carry):
        idx = [width * j + w for w in range(width)]
        for i, out in zip(idx, compute_many(idx)):
            commit(i, out)
        return carry

    lax.fori_loop(0, n // width, trip, 0)


def _seq_specs(bb, c):
    x_spec = pl.BlockSpec((bb, c, D_MODEL), lambda i, k: (i, k, 0))
    y_spec = pl.BlockSpec((bb, c, MIX_W), lambda i, k: (i, k, 0))
    return x_spec, y_spec


def _state_spec(bb, *tail):
    nt = len(tail)
    return pl.BlockSpec((bb,) + tuple(tail), lambda i, k: (i,) + (0,) * nt)


def _lru_kernel(x_ref, w_ref, cw_ref, cb_ref, wra_ref, bra_ref, wix_ref, bix_ref, lam_ref, conv0_ref, h0_ref,
                y_ref, conv1_ref, h1_ref, ext_ref, prev_ref, hst_ref):
    k = pl.program_id(1)
    bb, c, _ = x_ref.shape
    n = bb * c

    @pl.when(k == 0)
    def _():
        prev_ref[...] = conv0_ref[...]
        hst_ref[...] = h0_ref[...]

    xb = x_ref[...].reshape(n, D_MODEL).astype(BF16)
    proj = _mm(xb, w_ref[...])
    a_gate = proj[:, MIX_W:]
    xa = _causal_conv(proj[:, :MIX_W].reshape(bb, c, MIX_W), ext_ref, prev_ref, cw_ref, c) + cb_ref[...]
    xa = xa.reshape(n, MIX_W)
    xab = xa.astype(BF16)
    r = _sigmoid(_mm(xab, wra_ref[...]) + bra_ref[...])
    i = _sigmoid(_mm(xab, wix_ref[...]) + bix_ref[...])
    log_a = -LRU_C * r * _softplus(-lam_ref[...])
    a_cum = jnp.exp(log_a)
    b_cum = jnp.sqrt(-_expm1(2.0 * log_a)) * (i * xa)
    pos = _chunk_pos(bb, c, MIX_W)
    s = 1
    while s < c:
        keep = pos >= s
        a_sh = pltpu.roll(a_cum, s, 0)
        b_sh = pltpu.roll(b_cum, s, 0)
        b_cum = jnp.where(keep, a_cum * b_sh + b_cum, b_cum)
        a_cum = jnp.where(keep, a_cum * a_sh, a_cum)
        s *= 2
    h0 = jnp.broadcast_to(hst_ref[:, SUBLANES - 1:SUBLANES, :], (bb, c, MIX_W)).reshape(n, MIX_W)
    h = a_cum * h0 + b_cum
    y_ref[...] = (h * _gelu_tanh(a_gate)).reshape(bb, c, MIX_W)
    hst_ref[...] = h.reshape(bb, c, MIX_W)[:, c - SUBLANES:c, :]

    @pl.when(k == pl.num_programs(1) - 1)
    def _():
        conv1_ref[...] = prev_ref[...]
        h1_ref[...] = hst_ref[...]


def _lru_call(x3, w, cw, cb, wra, bra, wix, bix, lam, conv0, h0, bb, c):
    b, l, _ = x3.shape
    x_spec, y_spec = _seq_specs(bb, c)
    st = _state_spec(bb, SUBLANES, MIX_W)
    consts = (w, cw, cb, wra, bra, wix, bix, lam)
    return pl.pallas_call(
        _lru_kernel,
        out_shape=(jax.ShapeDtypeStruct((b, l, MIX_W), F32), jax.ShapeDtypeStruct((b, SUBLANES, MIX_W), F32),
                   jax.ShapeDtypeStruct((b, SUBLANES, MIX_W), F32)),
        grid=(b // bb, l // c),
        in_specs=[x_spec] + [_full(a) for a in consts] + [st, st],
        out_specs=(y_spec, st, st),
        scratch_shapes=[pltpu.VMEM((bb, c + SUBLANES, MIX_W), F32), pltpu.VMEM((bb, SUBLANES, MIX_W), F32),
                        pltpu.VMEM((bb, SUBLANES, MIX_W), F32)],
        compiler_params=_cparams("parallel", "arbitrary"),
        name="mixer_rglru",
    )(x3, *consts, conv0, h0)


def _hgrn_kernel(x_ref, w_ref, wlo_ref, loglb_ref, log1mlb_ref, onemlb_ref, ng_ref, ones_ref, st0_ref,
                 y_ref, st1_ref, st_ref, q_s, k_s, i_s, b_s, qe_s, kd_s, el_s, o_s):
    kk = pl.program_id(1)
    bb, c, _ = x_ref.shape
    n = bb * c
    nblk = c // SUBLANES

    @pl.when(kk == 0)
    def _():
        st_ref[...] = st0_ref[...]

    xb, xlo = _split_bf16(x_ref[...].reshape(n, D_MODEL))
    proj = _mm(xb, w_ref[...])
    q = proj[:, 0:MIX_W] * HEAD_DIM ** -0.5
    z = proj[:, MIX_W:2 * MIX_W] + _mm(xlo, w_ref[:, MIX_W:2 * MIX_W]) + _mm(xb, wlo_ref[...])
    iv = proj[:, 2 * MIX_W:3 * MIX_W]
    cg = proj[:, 3 * MIX_W:4 * MIX_W]
    lo = loglb_ref[...]
    hi = log1mlb_ref[...] + _log_sigmoid(z)
    log_f = jnp.maximum(lo, hi) + jnp.log1p(jnp.exp(-jnp.abs(lo - hi)))
    kc = onemlb_ref[...] * _sigmoid(-z)
    pos = _chunk_pos(bb, c, MIX_W)
    bc = _seg_scan(log_f, pos, c, jnp.add, 0.0)
    b_last = _chunk_last(bc, pos, bb, c)
    ones_bd = ones_ref[...]

    sub = pos & (SUBLANES - 1)
    o_band = None
    for d in range(SUBLANES):
        kr = pltpu.roll(kc, d, 0) if d else kc
        br = pltpu.roll(bc, d, 0) if d else bc
        ir = pltpu.roll(iv, d, 0) if d else iv
        wd = jnp.where(sub >= d, q * kr * jnp.exp(jnp.minimum(bc - br, 0.0)), 0.0)
        term = _head_sum(wd, ones_bd) * ir
        o_band = term if o_band is None else o_band + term

    q_s[...] = q.reshape(bb, c, MIX_W)
    k_s[...] = kc.reshape(bb, c, MIX_W)
    i_s[...] = iv.reshape(bb, c, MIX_W)
    b_s[...] = bc.reshape(bb, c, MIX_W)
    qe_s[...] = (q * jnp.exp(bc)).reshape(bb, c, MIX_W)
    kd_s[...] = (kc * jnp.exp(b_last - bc)).reshape(bb, c, MIX_W)
    el_s[...] = jnp.exp(b_last).reshape(bb, c, MIX_W)

    hmasks = _head_masks()
    blockmask = _head_block_mask()
    pad_rows = LANES - c
    rowid = lax.broadcasted_iota(jnp.int32, (LANES, MIX_W), 0)
    zpad = jnp.zeros((pad_rows, MIX_W), F32)

    def per_bs(bs):
        nb = range(len(bs))
        sts = [st_ref[b] for b in bs]
        os_ = [_bmm(qe_s[b], sts[j], _NT) for j, b in enumerate(bs)]
        if nblk > 1:
            kp = [jnp.concatenate([k_s[b], zpad], axis=0) for b in bs]
            bp = [jnp.concatenate([b_s[b], zpad], axis=0) for b in bs]
            ip = [jnp.concatenate([i_s[b], zpad], axis=0).astype(BF16) for b in bs]
            pieces = [[jnp.zeros((SUBLANES, MIX_W), F32)] for _ in bs]
            for blk in range(1, nblk):
                r0 = blk * SUBLANES
                atts = []
                for j, b in enumerate(bs):
                    ref_b = b_s[b, r0 - 1:r0, :]
                    qi = q_s[b, r0:r0 + SUBLANES, :] * jnp.exp(
                        jnp.minimum(b_s[b, r0:r0 + SUBLANES, :] - ref_b, 0.0))
                    ki = jnp.where(rowid < r0, kp[j] * jnp.exp(jnp.minimum(ref_b - bp[j], 0.0)), 0.0)
                    qst = jnp.concatenate([jnp.where(m, qi, 0.0) for m in hmasks], axis=0)
                    atts.append(_bmm(qst, ki, _NT))
                ress = [_mm(atts[j].astype(BF16), ip[j]) for j in nb]
                for j in nb:
                    acc = None
                    for h in range(N_HEADS):
                        part = jnp.where(hmasks[h], ress[j][h * SUBLANES:(h + 1) * SUBLANES], 0.0)
                        acc = part if acc is None else acc + part
                    pieces[j].append(acc)
            os_ = [os_[j] + jnp.concatenate(pieces[j], axis=0) for j in nb]
        upds = [_mm3(jnp.concatenate([i_s[b], zpad], axis=0), jnp.concatenate([kd_s[b], zpad], axis=0), _TN)
                for b in bs]
        return [(os_[j], sts[j] * el_s[b, 0:1, :] + jnp.where(blockmask, upds[j], 0.0)) for j, b in enumerate(bs)]

    def commit(b, outs):
        o_s[b] = outs[0]
        st_ref[b] = outs[1]

    _paired_loop(bb, per_bs, commit, width=8)

    o = o_s[...].reshape(n, MIX_W) + o_band
    ms = _head_sum(o * o, ones_bd) * (1.0 / HEAD_DIM)
    y = o * lax.rsqrt(ms + NORM_EPS) * ng_ref[...] * _silu(cg)
    y_ref[...] = y.reshape(bb, c, MIX_W)

    @pl.when(kk == pl.num_programs(1) - 1)
    def _():
        st1_ref[...] = st_ref[...]


def _hgrn_call(x3, w, wlo, loglb, log1mlb, onemlb, ng, ones_bd, st0, bb, c):
    b, l, _ = x3.shape
    x_spec, y_spec = _seq_specs(bb, c)
    st = _state_spec(bb, MIX_W, MIX_W)
    consts = (w, wlo, loglb, log1mlb, onemlb, ng, ones_bd)
    rows = pltpu.VMEM((bb, c, MIX_W), F32)
    return pl.pallas_call(
        _hgrn_kernel,
        out_shape=(jax.ShapeDtypeStruct((b, l, MIX_W), F32), jax.ShapeDtypeStruct((b, MIX_W, MIX_W), F32)),
        grid=(b // bb, l // c),
        in_specs=[x_spec] + [_full(a) for a in consts] + [st],
        out_specs=(y_spec, st),
        scratch_shapes=[pltpu.VMEM((bb, MIX_W, MIX_W), F32)] + [rows] * 8,
        compiler_params=_cparams("parallel", "arbitrary"),
        name="mixer_hgrn2",
    )(x3, *consts, st0)


def _mlstm_kernel(g, cs, x_ref, w_ref, wlo_ref, cw_ref, cb_ref, wq_ref, wk_ref, bi_ref, bf_ref, ng_ref, ones_ref, exp_ref,
                  conv0_ref, c0_ref, n0_ref, m0_ref,
                  y_ref, conv1_ref, c1_ref, n1_ref, m1_ref,
                  ext_ref, prev_ref, cst_ref, nst_ref, mst_ref, q_s, k_s, v_s, kw_s, dec_s, a_s, g_s, num_s, den_s,
                  qc_s):
    kk = pl.program_id(1)
    bb, c, _ = x_ref.shape
    n = bb * c
    ngroups = bb // g

    @pl.when(kk == 0)
    def _():
        prev_ref[...] = conv0_ref[...]
        cst_ref[...] = c0_ref[...]
        nst_ref[...] = n0_ref[...]
        mst_ref[...] = m0_ref[...]

    xb, xlo = _split_bf16(x_ref[...].reshape(n, D_MODEL))
    proj = _mm(xb, w_ref[...])
    gates = proj[:, 3 * MIX_W:] + _mm(xlo, w_ref[:, 3 * MIX_W:]) + _mm(xb, wlo_ref[...])
    xm = _causal_conv(proj[:, 0:MIX_W].reshape(bb, c, MIX_W), ext_ref, prev_ref, cw_ref, c) + cb_ref[...]
    xm = _silu(xm).reshape(n, MIX_W).astype(BF16)
    q = _mm(xm, wq_ref[...])
    k = _mm(xm, wk_ref[...]) * HEAD_DIM ** -0.5
    v = proj[:, MIX_W:2 * MIX_W]
    d_o = proj[:, 2 * MIX_W:3 * MIX_W]
    log_i = gates[:, 0:LANES] + bi_ref[...]
    log_f = _log_sigmoid(gates[:, LANES:2 * LANES] + bf_ref[...])
    pos = _chunk_pos(bb, c, LANES)
    bc = _seg_scan(log_f, pos, c, jnp.add, 0.0)
    gg = log_i - bc
    cm = _seg_scan(gg, pos, c, jnp.maximum, -jnp.inf)
    m0 = jnp.broadcast_to(mst_ref[:, SUBLANES - 1:SUBLANES, :], (bb, c, LANES)).reshape(n, LANES)
    m_t = jnp.maximum(m0 + bc, bc + cm)
    b_last = _chunk_last(bc, pos, bb, c)
    m_last = _chunk_last(m_t, pos, bb, c)
    expand = exp_ref[...]
    ones_bd = ones_ref[...]
    inter = _to_hb(jnp.exp(m0 + bc - m_t), expand)
    e_negm = _to_hb(jnp.exp(-m_t), expand)
    wl = _to_hb(jnp.exp(gg + b_last - m_last), expand)
    nrows = jnp.broadcast_to(nst_ref[:, 0:1, :], (bb, c, MIX_W)).reshape(n, MIX_W)
    qn = _head_sum(q * nrows, ones_bd)

    q_s[...] = q.reshape(bb, c, MIX_W)
    k_s[...] = k.reshape(bb, c, MIX_W)
    v_s[...] = v.reshape(bb, c, MIX_W)
    kw_s[...] = (k * wl).reshape(bb, c, MIX_W)
    dec_s[...] = _to_hb(jnp.exp(m0 + b_last - m_last), expand).reshape(bb, c, MIX_W)
    a_s[...] = (bc - m_t).reshape(bb, c, LANES)
    g_s[...] = gg.reshape(bb, c, LANES)

    hmasks = _head_masks()
    causal, _ = _bd_masks(cs)
    rowb = lax.broadcasted_iota(jnp.int32, (BD, MIX_W), 0) // (N_HEADS * cs)
    sel = _stack(jnp.ones((g * c, MIX_W), F32), hmasks, g, c, cs)

    def per_groups(gis):
        ng = range(len(gis))
        qg = [_group_rows(q_s, gi, g, c) for gi in gis]
        qst = [_stack(qg[j], hmasks, g, c, cs).astype(BF16) for j in ng]
        kst = [_stack(_group_rows(k_s, gi, g, c), hmasks, g, c, cs).astype(BF16) for gi in gis]
        vst = [_stack(_group_rows(v_s, gi, g, c), hmasks, g, c, cs).astype(BF16) for gi in gis]
        kwst = [_stack(_group_rows(kw_s, gi, g, c), hmasks, g, c, cs) for gi in gis]
        acol = [_stack_cols(_group_rows(a_s, gi, g, c), g, c, cs)[0] for gi in gis]
        grow = [_stack_cols(_group_rows(g_s, gi, g, c), g, c, cs)[1] for gi in gis]
        qk = [_mm(qst[j], kst[j], _NT) for j in ng]
        cmats = [[cst_ref[gi * g + bl] for bl in range(g)] for gi in gis]
        qcs = [[_bmm(qg[j][bl * c:(bl + 1) * c], cmats[j][bl]) for bl in range(g)] for j in ng]
        qkw = [qk[j] * jnp.where(causal, jnp.exp(jnp.minimum(acol[j] + grow[j], 0.0)), 0.0) for j in ng]
        num = [_unstack(_mm(qkw[j].astype(BF16), vst[j]), g, c, cs) for j in ng]
        den = [_unstack(jnp.sum(qkw[j], axis=1, keepdims=True) * sel, g, c, cs) for j in ng]
        outs = []
        for j, gi in enumerate(gis):
            outs_g = []
            for bl in range(g):
                b = gi * g + bl
                kwb = kwst[j] if g == 1 else jnp.where(rowb == bl, kwst[j], 0.0)
                dec_row = dec_s[b, 0:1, :]
                ksum = jnp.sum(kw_s[b], axis=0, keepdims=True)
                outs_g.append((qcs[j][bl], num[j][bl * c:(bl + 1) * c], den[j][bl * c:(bl + 1) * c],
                               cmats[j][bl] * dec_row + _mm(kwb.astype(BF16), vst[j], _TN),
                               nst_ref[b] * dec_row + jnp.broadcast_to(ksum, (SUBLANES, MIX_W))))
            outs.append(outs_g)
        return outs

    def commit(gi, outs):
        for bl, (qc_b, num_b, den_b, c_b, n_b) in enumerate(outs):
            b = gi * g + bl
            qc_s[b] = qc_b
            num_s[b] = num_b
            den_s[b] = den_b
            cst_ref[b] = c_b
            nst_ref[b] = n_b

    _paired_loop(ngroups, per_groups, commit, width=8)

    num = num_s[...].reshape(n, MIX_W) + inter * qc_s[...].reshape(n, MIX_W)
    den = den_s[...].reshape(n, MIX_W) + inter * qn
    h = num / jnp.maximum(jnp.abs(den), e_negm)
    mu = _head_sum(h, ones_bd) * (1.0 / HEAD_DIM)
    hc = h - mu
    var = _head_sum(hc * hc, ones_bd) * (1.0 / HEAD_DIM)
    y = hc * lax.rsqrt(var + NORM_EPS) * ng_ref[...] * _sigmoid(d_o)
    y_ref[...] = y.reshape(bb, c, MIX_W)
    mst_ref[...] = m_t.reshape(bb, c, LANES)[:, c - SUBLANES:c, :]

    @pl.when(kk == pl.num_programs(1) - 1)
    def _():
        conv1_ref[...] = prev_ref[...]
        c1_ref[...] = cst_ref[...]
        n1_ref[...] = nst_ref[...]
        m1_ref[...] = mst_ref[...]


def _mlstm_call(x3, w, wlo, cw, cb, wq, wk, bi, bf, ng, ones_bd, expand, conv0, c0, n0, m0, bb, c, g, cs):
    b, l, _ = x3.shape
    x_spec, y_spec = _seq_specs(bb, c)
    st_conv = _state_spec(bb, SUBLANES, MIX_W)
    st_c = _state_spec(bb, MIX_W, MIX_W)
    st_n = _state_spec(bb, SUBLANES, MIX_W)
    st_m = _state_spec(bb, SUBLANES, LANES)
    consts = (w, wlo, cw, cb, wq, wk, bi, bf, ng, ones_bd, expand)
    rows = pltpu.VMEM((bb, c, MIX_W), F32)
    cols = pltpu.VMEM((bb, c, LANES), F32)
    return pl.pallas_call(
        functools.partial(_mlstm_kernel, g, cs),
        out_shape=(jax.ShapeDtypeStruct((b, l, MIX_W), F32), jax.ShapeDtypeStruct((b, SUBLANES, MIX_W), F32),
                   jax.ShapeDtypeStruct((b, MIX_W, MIX_W), F32), jax.ShapeDtypeStruct((b, SUBLANES, MIX_W), F32),
                   jax.ShapeDtypeStruct((b, SUBLANES, LANES), F32)),
        grid=(b // bb, l // c),
        in_specs=[x_spec] + [_full(a) for a in consts] + [st_conv, st_c, st_n, st_m],
        out_specs=(y_spec, st_conv, st_c, st_n, st_m),
        scratch_shapes=[pltpu.VMEM((bb, c + SUBLANES, MIX_W), F32), pltpu.VMEM((bb, SUBLANES, MIX_W), F32),
                        pltpu.VMEM((bb, MIX_W, MIX_W), F32), pltpu.VMEM((bb, SUBLANES, MIX_W), F32),
                        pltpu.VMEM((bb, SUBLANES, LANES), F32),
                        rows, rows, rows, rows, rows, cols, cols, rows, rows, rows],
        compiler_params=_cparams("parallel", "arbitrary"),
        name="mixer_mlstm",
    )(x3, *consts, conv0, c0, n0, m0)


def _gdn_kernel(g, cs, x_ref, w_ref, wlo_ref, cw_ref, alog_ref, dtb_ref, ng_ref, ones_ref, exp_ref, conv0_ref, s0_ref,
                y_ref, conv1_ref, s1_ref,
                ext_ref, prev_ref, sst_ref, q_s, k_s, vb_s, kb_s, qe_s, kd_s, el_s, gc_s, be_s, o_s):
    kk = pl.program_id(1)
    bb, c, _ = x_ref.shape
    n = bb * c
    ngroups = bb // g
    qkv_w = 3 * MIX_W

    @pl.when(kk == 0)
    def _():
        prev_ref[...] = conv0_ref[...]
        sst_ref[...] = s0_ref[...]

    xb, xlo = _split_bf16(x_ref[...].reshape(n, D_MODEL))
    proj = _mm(xb, w_ref[...])
    gates = proj[:, 4 * MIX_W:] + _mm(xlo, w_ref[:, 4 * MIX_W:]) + _mm(xb, wlo_ref[...])
    qkv = _causal_conv(proj[:, 0:qkv_w].reshape(bb, c, qkv_w), ext_ref, prev_ref, cw_ref, c)
    qkv = _silu(qkv).reshape(n, qkv_w)
    ones_bd = ones_ref[...]
    expand = exp_ref[...]
    q = qkv[:, 0:MIX_W]
    k = qkv[:, MIX_W:2 * MIX_W]
    v = qkv[:, 2 * MIX_W:3 * MIX_W]
    q = q * lax.rsqrt(_head_sum(q * q, ones_bd) + NORM_EPS) * HEAD_DIM ** -0.5
    k = k * lax.rsqrt(_head_sum(k * k, ones_bd) + NORM_EPS)
    z = proj[:, qkv_w:qkv_w + MIX_W]
    a_in = gates[:, 0:LANES]
    b_in = gates[:, LANES:2 * LANES]
    gdec = -jnp.exp(alog_ref[...]) * _softplus(a_in + dtb_ref[...])
    beta = _sigmoid(b_in)
    pos = _chunk_pos(bb, c, LANES)
    gcum = _seg_scan(gdec, pos, c, jnp.add, 0.0)
    g_last = _chunk_last(gcum, pos, bb, c)
    beta_hb = _to_hb(beta, expand)
    eg_hb = _to_hb(jnp.exp(gcum), expand)

    q_s[...] = q.reshape(bb, c, MIX_W)
    k_s[...] = k.reshape(bb, c, MIX_W)
    vb_s[...] = (v * beta_hb).reshape(bb, c, MIX_W)
    kb_s[...] = (k * beta_hb * eg_hb).reshape(bb, c, MIX_W)
    qe_s[...] = (q * eg_hb).reshape(bb, c, MIX_W)
    kd_s[...] = (k * _to_hb(jnp.exp(g_last - gcum), expand)).reshape(bb, c, MIX_W)
    el_s[...] = _to_hb(jnp.exp(g_last), expand).reshape(bb, c, MIX_W)
    gc_s[...] = gcum.reshape(bb, c, LANES)
    be_s[...] = beta.reshape(bb, c, LANES)

    hmasks = _head_masks()
    causal, strict = _bd_masks(cs)
    rowb = lax.broadcasted_iota(jnp.int32, (BD, MIX_W), 0) // (N_HEADS * cs)
    rows_b = N_HEADS * cs
    sel = _stack(jnp.ones((g * c, MIX_W), F32), hmasks, g, c, cs) > 0.0

    def per_groups(gis):
        ng = range(len(gis))
        qg = [_group_rows(qe_s, gi, g, c) for gi in gis]
        qst = [_stack(_group_rows(q_s, gi, g, c), hmasks, g, c, cs).astype(BF16) for gi in gis]
        kst = [_stack(_group_rows(k_s, gi, g, c), hmasks, g, c, cs).astype(BF16) for gi in gis]
        kdst = [_stack(_group_rows(kd_s, gi, g, c), hmasks, g, c, cs) for gi in gis]
        x = [_stack(_group_rows(vb_s, gi, g, c), hmasks, g, c, cs)
             + pltpu.roll(_stack(_group_rows(kb_s, gi, g, c), hmasks, g, c, cs), HEAD_DIM, 1) for gi in gis]
        gcr = [_stack_cols(_group_rows(gc_s, gi, g, c), g, c, cs) for gi in gis]
        bcol = [_stack_cols(_group_rows(be_s, gi, g, c), g, c, cs)[0] for gi in gis]
        decay = [jnp.exp(jnp.minimum(gcr[j][0] - gcr[j][1], 0.0)) for j in ng]
        kk = [_mm(kst[j], kst[j], _NT) for j in ng]
        p = [jnp.where(strict, bcol[j] * kk[j] * decay[j], 0.0) for j in ng]
        ax = [_bmm(p[j], x[j]) for j in ng]
        x = [x[j] - ax[j] for j in ng]
        span = 2
        while span < cs:
            p = [_bmm(p[j], p[j]) for j in ng]
            px = [_bmm(p[j], x[j]) for j in ng]
            x = [x[j] + px[j] for j in ng]
            span *= 2
        qk = [_mm(qst[j], kst[j], _NT) for j in ng]
        smats = [[sst_ref[gi * g + bl] for bl in range(g)] for gi in gis]
        w_v = [jnp.where(sel, x[j], 0.0) for j in ng]
        w_k = [jnp.where(sel, pltpu.roll(x[j], MIX_W - HEAD_DIM, 1), 0.0) for j in ng]
        ws = [[_bmm(w_k[j][bl * rows_b:(bl + 1) * rows_b], smats[j][bl]) for bl in range(g)] for j in ng]
        ub = []
        for j in ng:
            us = [w_v[j][bl * rows_b:(bl + 1) * rows_b] - ws[j][bl] for bl in range(g)]
            ub.append((us[0] if g == 1 else jnp.concatenate(us, axis=0)).astype(BF16))
        qkd = [jnp.where(causal, qk[j] * decay[j], 0.0).astype(BF16) for j in ng]
        o_intra = [_unstack(_mm(qkd[j], ub[j]), g, c, cs) for j in ng]
        outs = []
        for j, gi in enumerate(gis):
            outs_g = []
            for bl in range(g):
                b = gi * g + bl
                o_b = o_intra[j][bl * c:(bl + 1) * c] + _bmm(qg[j][bl * c:(bl + 1) * c], smats[j][bl])
                kdb = kdst[j] if g == 1 else jnp.where(rowb == bl, kdst[j], 0.0)
                outs_g.append((o_b, smats[j][bl] * el_s[b, 0:1, :] + _mm(kdb.astype(BF16), ub[j], _TN)))
            outs.append(outs_g)
        return outs

    def commit(gi, outs):
        for bl, (o_b, s_b) in enumerate(outs):
            o_s[gi * g + bl] = o_b
            sst_ref[gi * g + bl] = s_b

    _paired_loop(ngroups, per_groups, commit, width=4)

    o = o_s[...].reshape(n, MIX_W)
    ms = _head_sum(o * o, ones_bd) * (1.0 / HEAD_DIM)
    y = o * lax.rsqrt(ms + NORM_EPS) * ng_ref[...] * _silu(z)
    y_ref[...] = y.reshape(bb, c, MIX_W)

    @pl.when(kk == pl.num_programs(1) - 1)
    def _():
        conv1_ref[...] = prev_ref[...]
        s1_ref[...] = sst_ref[...]


def _gdn_call(x3, w, wlo, cw, alog, dtb, ng, ones_bd, expand, conv0, s0, bb, c, g, cs):
    b, l, _ = x3.shape
    x_spec, y_spec = _seq_specs(bb, c)
    st_conv = _state_spec(bb, SUBLANES, 3 * MIX_W)
    st_s = _state_spec(bb, MIX_W, MIX_W)
    consts = (w, wlo, cw, alog, dtb, ng, ones_bd, expand)
    rows = pltpu.VMEM((bb, c, MIX_W), F32)
    cols = pltpu.VMEM((bb, c, LANES), F32)
    return pl.pallas_call(
        functools.partial(_gdn_kernel, g, cs),
        out_shape=(jax.ShapeDtypeStruct((b, l, MIX_W), F32), jax.ShapeDtypeStruct((b, SUBLANES, 3 * MIX_W), F32),
                   jax.ShapeDtypeStruct((b, MIX_W, MIX_W), F32)),
        grid=(b // bb, l // c),
        in_specs=[x_spec] + [_full(a) for a in consts] + [st_conv, st_s],
        out_specs=(y_spec, st_conv, st_s),
        scratch_shapes=[pltpu.VMEM((bb, c + SUBLANES, 3 * MIX_W), F32), pltpu.VMEM((bb, SUBLANES, 3 * MIX_W), F32),
                        pltpu.VMEM((bb, MIX_W, MIX_W), F32),
                        rows, rows, rows, rows, rows, rows, rows, cols, cols, rows],
        compiler_params=_cparams("parallel", "arbitrary"),
        name="mixer_gdn",
    )(x3, *consts, conv0, s0)


def _pad_cols(a, width):
    return jnp.pad(a, ((0, 0), (0, width - a.shape[1])))


def _row(a, width=None):
    a = a.reshape(1, -1).astype(F32)
    return a if width is None else _pad_cols(a, width)


def _block_diag(w4):
    h, d, _ = w4.shape
    eye = jnp.eye(h, dtype=w4.dtype)
    return jnp.einsum("hij,hg->higj", w4, eye).reshape(h * d, h * d)


def _bd_state(s):
    b, h, d, _ = s.shape
    eye = jnp.eye(h, dtype=s.dtype)
    return jnp.einsum("bhij,hg->bhigj", s, eye).reshape(b, h * d, h * d)


def _bd_blocks(s):
    b = s.shape[0]
    s5 = s.reshape(b, N_HEADS, HEAD_DIM, N_HEADS, HEAD_DIM)
    return jnp.stack([s5[:, h, :, h, :] for h in range(N_HEADS)], axis=1)


def _tail8(buf):
    return jnp.pad(buf, ((0, 0), (SUBLANES - (CONV_W - 1), 0), (0, 0)))


def _bcast8(a):
    return jnp.broadcast_to(a[:, None, :], (a.shape[0], SUBLANES, a.shape[1]))


def _layer_params(P, lb, l):
    w_f32 = P["w_in"][l].astype(F32)
    w_in = w_f32.astype(BF16)
    w_lo = (w_f32 - w_in.astype(F32)).astype(BF16)

    def gate_tiles(w, lo, hi):
        return [_pad_cols(w[:, lo:lo + N_HEADS], LANES), _pad_cols(w[:, lo + N_HEADS:hi], LANES)]

    p = {}
    p["w_a"] = w_in[:, _A0:_B0]
    p["w_b"] = jnp.concatenate([w_in[:, _B0:_B0 + 4 * MIX_W]] + gate_tiles(w_in, _B0 + 4 * MIX_W, _C0), axis=1)
    p["w_b_lo"] = jnp.concatenate(gate_tiles(w_lo, _B0 + 4 * MIX_W, _C0), axis=1)
    p["w_c"] = w_in[:, _C0:_D0]
    p["w_c_lo"] = w_lo[:, _C0 + MIX_W:_C0 + 2 * MIX_W]
    p["w_d"] = jnp.concatenate([w_in[:, _D0:_D0 + 3 * MIX_W]] + gate_tiles(w_in, _D0 + 3 * MIX_W, _G0), axis=1)
    p["w_d_lo"] = jnp.concatenate(gate_tiles(w_lo, _D0 + 3 * MIX_W, _G0), axis=1)
    p["w_g"] = w_in[:, _G0:]
    p["lru_cw"] = P["w_lru_conv"][l]
    p["lru_cb"] = _row(P["b_lru_conv"][l])
    p["lru_wra"] = _block_diag(P["w_lru_ra"][l]).astype(BF16)
    p["lru_bra"] = _row(P["b_lru_ra"][l])
    p["lru_wix"] = _block_diag(P["w_lru_ix"][l]).astype(BF16)
    p["lru_bix"] = _row(P["b_lru_ix"][l])
    p["lru_lam"] = _row(P["lru_lambda"][l])
    p["gdn_cw"] = P["w_gdn_conv"][l]
    p["gdn_alog"] = _row(P["gdn_a_log"][l], LANES)
    p["gdn_dtb"] = _row(P["gdn_dt_bias"][l], LANES)
    p["gdn_ng"] = _row(jnp.tile(P["gdn_norm_g"][l], N_HEADS))
    p["hg_loglb"] = _row(jnp.log(lb[l]))
    p["hg_log1mlb"] = _row(jnp.log1p(-lb[l]))
    p["hg_onemlb"] = _row(1.0 - lb[l])
    p["hg_ng"] = _row(jnp.tile(P["hg_norm_g"][l], N_HEADS))
    p["ml_cw"] = P["w_ml_conv"][l]
    p["ml_cb"] = _row(P["b_ml_conv"][l])
    p["ml_wq"] = _block_diag(P["w_ml_q"][l]).astype(BF16)
    p["ml_wk"] = _block_diag(P["w_ml_k"][l]).astype(BF16)
    p["ml_bi"] = _row(P["b_ml_i"][l], LANES)
    p["ml_bf"] = _row(P["b_ml_f"][l], LANES)
    p["ml_ng"] = _row(P["ml_norm_g"][l])
    p["w_br"] = P["w_branch"][l].astype(BF16)
    p["w_out"] = P["w_out"][l].astype(BF16)
    p["ln1_g"] = _row(P["ln1_g"][l])
    p["ln1_b"] = _row(P["ln1_b"][l])
    p["ln2_g"] = _row(P["ln2_g"][l])
    p["ln2_b"] = _row(P["ln2_b"][l])
    p["w_r"] = _pad_cols(P["w_router"][l], LANES).astype(BF16)
    p["b_r"] = jnp.concatenate([P["b_router"][l].astype(F32), jnp.full((LANES - N_EXPERTS,), -1e30, F32)]).reshape(1, LANES)
    return p


def _mixers(x3, p, st, consts, bb, c, g, cs):
    ones_bd, expand = consts
    lru_conv, lru_h, gdn_conv, gdn_s, hg_st, ml_conv, ml_c, ml_n, ml_m = st
    y_a, lru_conv1, lru_h1 = _lru_call(x3, p["w_a"], p["lru_cw"], p["lru_cb"], p["lru_wra"], p["lru_bra"],
                                       p["lru_wix"], p["lru_bix"], p["lru_lam"], lru_conv, lru_h, bb, c)
    y_b, gdn_conv1, gdn_s1 = _gdn_call(x3, p["w_b"], p["w_b_lo"], p["gdn_cw"], p["gdn_alog"], p["gdn_dtb"], p["gdn_ng"],
                                       ones_bd, expand, gdn_conv, gdn_s, bb, c, g, cs)
    y_c, hg_st1 = _hgrn_call(x3, p["w_c"], p["w_c_lo"], p["hg_loglb"], p["hg_log1mlb"], p["hg_onemlb"], p["hg_ng"], ones_bd,
                             hg_st, bb, c)
    y_d, ml_conv1, ml_c1, ml_n1, ml_m1 = _mlstm_call(x3, p["w_d"], p["w_d_lo"], p["ml_cw"], p["ml_cb"], p["ml_wq"], p["ml_wk"],
                                                     p["ml_bi"], p["ml_bf"], p["ml_ng"], ones_bd, expand,
                                                     ml_conv, ml_c, ml_n, ml_m, bb, c, g, cs)
    return (y_a, y_b, y_c, y_d), (lru_conv1, lru_h1, gdn_conv1, gdn_s1, hg_st1, ml_conv1, ml_c1, ml_n1, ml_m1)


def _states_to_kernel(st):
    lru_conv, lru_h, gdn_conv, gdn_s, hg_s, ml_conv, ml_c, ml_n, ml_m = [s.astype(F32) for s in st]
    b = lru_h.shape[0]
    return (_tail8(lru_conv), _bcast8(lru_h), _tail8(gdn_conv), _bd_state(gdn_s),
            jnp.swapaxes(_bd_state(hg_s), 1, 2), _tail8(ml_conv), _bd_state(ml_c),
            _bcast8(ml_n.reshape(b, MIX_W)), _bcast8(_pad_cols(ml_m, LANES)))


def _states_from_kernel(st):
    lru_conv, lru_h, gdn_conv, gdn_s, hg_st, ml_conv, ml_c, ml_n, ml_m = st
    b = lru_h.shape[0]
    tail = SUBLANES - (CONV_W - 1)
    return (lru_conv[:, tail:], lru_h[:, SUBLANES - 1], gdn_conv[:, tail:], _bd_blocks(gdn_s),
            _bd_blocks(jnp.swapaxes(hg_st, 1, 2)), ml_conv[:, tail:], _bd_blocks(ml_c),
            ml_n[:, 0].reshape(b, N_HEADS, HEAD_DIM), ml_m[:, SUBLANES - 1, :N_HEADS])


def _route(top_e):
    t = top_e.shape[0]
    m = t * TOP_K
    flat_e = top_e.reshape(m)
    onehot = (flat_e[:, None] == jnp.arange(N_EXPERTS, dtype=jnp.int32)[None, :]).astype(jnp.int32)
    csum = jnp.cumsum(onehot, axis=0)
    rank = jnp.sum(onehot * csum, axis=1) - 1
    counts = csum[-1]
    padded = (counts + MOE_BLOCK_ROWS - 1) // MOE_BLOCK_ROWS * MOE_BLOCK_ROWS
    pad_ends = jnp.cumsum(padded)
    dest = (pad_ends - padded)[flat_e] + rank
    n_blocks = m // MOE_BLOCK_ROWS + N_EXPERTS
    mp = n_blocks * MOE_BLOCK_ROWS
    block_start = jnp.arange(n_blocks, dtype=jnp.int32) * MOE_BLOCK_ROWS
    block_e = jnp.minimum(jnp.sum((pad_ends[None, :] <= block_start[:, None]).astype(jnp.int32), axis=1),
                          N_EXPERTS - 1)
    n_used = (pad_ends[-1] // MOE_BLOCK_ROWS).astype(jnp.int32).reshape(1)
    n_pad = padded - counts
    j = jnp.arange(MOE_BLOCK_ROWS, dtype=jnp.int32)[None, :]
    is_pad = (j < n_pad[:, None]).reshape(-1)
    pad_row = ((pad_ends - n_pad)[:, None] + j).reshape(-1)
    tail_rank = jnp.cumsum(jnp.logical_not(is_pad).astype(jnp.int32)) - 1
    zero_rows = jnp.where(is_pad, pad_row, pad_ends[-1] + tail_rank).astype(jnp.int32)
    return dest.reshape(t, TOP_K), zero_rows, block_e, n_used, mp


def _dest_tiles(dest):
    t = dest.shape[0]
    n = t // COMBINE_ROWS
    return jnp.swapaxes(dest.reshape(n, COMBINE_ROWS, TOP_K), 1, 2).reshape(n, 1, TOP_K * COMBINE_ROWS)


def _trunk_layer(layer, xs, states, p, moe_w, consts, cfgs):
    x1s, tes, tgs, new_states = [], [], [], []
    for x3, st, (bb, c, g, cs, tm) in zip(xs, states, cfgs):
        b, l, _ = x3.shape
        ys, st1 = _mixers(x3, p, st, consts, bb, c, g, cs)
        x1, te, tg = _merge_call(x3.reshape(b * l, D_MODEL), [y.reshape(b * l, MIX_W) for y in ys],
                                 p["w_g"], p["w_br"], p["w_out"], p["ln1_g"], p["ln1_b"], p["w_r"], p["b_r"], tm)
        x1s.append(x1)
        tes.append(te[:, :TOP_K])
        tgs.append(tg)
        new_states.append(st1)
    dest, zero_rows, block_e, n_used, mp = _route(jnp.concatenate(tes, axis=0))
    d_tiles = []
    off = 0
    for x1 in x1s:
        d_tiles.append(_dest_tiles(lax.slice_in_dim(dest, off, off + x1.shape[0], axis=0)))
        off += x1.shape[0]
    z_tiles = zero_rows.reshape(-1, 1, TOP_K * COMBINE_ROWS)
    rows = _dispatch_call(x1s[0], x1s[1], jnp.concatenate(d_tiles + [z_tiles], axis=0), mp)
    ys = _moe_call(layer, block_e, n_used, rows, *moe_w)
    outs = []
    for x3, x1, tg, dt in zip(xs, x1s, tgs, d_tiles):
        outs.append(_combine_ln2_call(x1, tg, dt, ys, p["ln2_g"], p["ln2_b"]).reshape(x3.shape))
    return outs, new_states


def _row_tile(t):
    for tm in (512, 384, 256, 128, 64, 32, 16):
        if t % tm == 0:
            return tm
    raise ValueError(f"no row tile for {t} rows")


def _group_cfg(b, l):
    if l % PROMPT_CHUNK == 0:
        return (min(b, 8), PROMPT_CHUNK, 1, BD // N_HEADS, _row_tile(b * l))
    assert l == SUBLANES, "sequence length must be a multiple of the prompt chunk or one sublane tile"
    g = BD // (N_HEADS * SUBLANES)
    bb = min(b, 16)
    assert bb % g == 0 and b % bb == 0
    return (bb, SUBLANES, g, SUBLANES, _row_tile(b * l))


def _zero_states(b):
    return (jnp.zeros((b, CONV_W - 1, MIX_W), F32), jnp.zeros((b, MIX_W), F32),
            jnp.zeros((b, CONV_W - 1, 3 * MIX_W), F32), jnp.zeros((b, N_HEADS, HEAD_DIM, HEAD_DIM), F32),
            jnp.zeros((b, N_HEADS, HEAD_DIM, HEAD_DIM), F32), jnp.zeros((b, CONV_W - 1, MIX_W), F32),
            jnp.zeros((b, N_HEADS, HEAD_DIM, HEAD_DIM), F32), jnp.zeros((b, N_HEADS, HEAD_DIM), F32),
            jnp.zeros((b, N_HEADS), F32))


def kernel(x_prompt, x_sample, state_lru_conv, state_lru_h, state_gdn_conv, state_gdn, state_hgrn, state_mlstm_conv, state_mlstm_c, state_mlstm_n, state_mlstm_m, meta_tokens, ln_emb_g, ln_emb_b, hg_lb_logits, w_in, w_lru_conv, b_lru_conv, w_lru_ra, b_lru_ra, w_lru_ix, b_lru_ix, lru_lambda, w_gdn_conv, gdn_a_log, gdn_dt_bias, gdn_norm_g, hg_norm_g, w_ml_conv, b_ml_conv, w_ml_q, w_ml_k, b_ml_i, b_ml_f, ml_norm_g, w_branch, w_out, ln1_g, ln1_b, ln2_g, ln2_b, w_router, b_router, w_up, b_up, w_down, b_down):
    P = dict(w_in=w_in, w_lru_conv=w_lru_conv, b_lru_conv=b_lru_conv, w_lru_ra=w_lru_ra, b_lru_ra=b_lru_ra,
             w_lru_ix=w_lru_ix, b_lru_ix=b_lru_ix, lru_lambda=lru_lambda, w_gdn_conv=w_gdn_conv,
             gdn_a_log=gdn_a_log, gdn_dt_bias=gdn_dt_bias, gdn_norm_g=gdn_norm_g, hg_norm_g=hg_norm_g,
             w_ml_conv=w_ml_conv, b_ml_conv=b_ml_conv, w_ml_q=w_ml_q, w_ml_k=w_ml_k, b_ml_i=b_ml_i,
             b_ml_f=b_ml_f, ml_norm_g=ml_norm_g, w_branch=w_branch, w_out=w_out, ln1_g=ln1_g, ln1_b=ln1_b,
             ln2_g=ln2_g, ln2_b=ln2_b, w_router=w_router, b_router=b_router, w_up=w_up, b_up=b_up,
             w_down=w_down, b_down=b_down)
    depth = w_in.shape[0]
    lb_cum = jnp.cumsum(jax.nn.softmax(hg_lb_logits.astype(F32), axis=0), axis=0)
    lb = lb_cum - lb_cum[0:1]

    lane = jnp.arange(MIX_W) // HEAD_DIM
    ones_bd = (lane[:, None] == lane[None, :]).astype(BF16)
    expand = (jnp.arange(LANES)[:, None] == lane[None, :]).astype(BF16)
    consts = (ones_bd, expand)
    moe_w = (w_up.astype(F32), b_up.reshape(depth, N_EXPERTS, 1, 2 * D_FF).astype(F32),
             w_down.astype(F32), b_down.reshape(depth, N_EXPERTS, 1, D_MODEL).astype(F32))

    bp, lp0, _ = x_prompt.shape
    bs, ls, _ = x_sample.shape
    meta = jnp.broadcast_to(meta_tokens.astype(F32)[None], (bp, N_META, D_MODEL))
    xp_in = jnp.concatenate([meta, x_prompt], axis=1)
    lp = lp0 + N_META
    cfgs = [_group_cfg(bp, lp), _group_cfg(bs, ls)]
    g_emb, b_emb = _row(ln_emb_g), _row(ln_emb_b)
    xp = _ln_call(xp_in.reshape(bp * lp, D_MODEL), g_emb, b_emb, cfgs[0][4]).reshape(bp, lp, D_MODEL)
    xs_ = _ln_call(x_sample.reshape(bs * ls, D_MODEL), g_emb, b_emb, cfgs[1][4]).reshape(bs, ls, D_MODEL)

    sample_states = (state_lru_conv, state_lru_h, state_gdn_conv, state_gdn, state_hgrn,
                     state_mlstm_conv, state_mlstm_c, state_mlstm_n, state_mlstm_m)
    zero_p = _zero_states(bp)
    xs = [xp, xs_]
    collected = ([], [])
    for l in range(depth):
        p = _layer_params(P, lb, l)
        states = [_states_to_kernel(zero_p), _states_to_kernel(tuple(s[l] for s in sample_states))]
        xs, new_states = _trunk_layer(l, xs, states, p, moe_w, consts, cfgs)
        for grp in range(2):
            collected[grp].append(_states_from_kernel(new_states[grp]))
    new_p = tuple(jnp.stack([layer[i] for layer in collected[0]]) for i in range(9))
    new_s = tuple(jnp.stack([layer[i] for layer in collected[1]]) for i in range(9))
    y_prompt = xs[0][:, N_META:]
    y_sample = xs[1]
    out = [y_prompt, y_sample]
    for i in range(9):
        out.append(new_p[i])
        out.append(new_s[i])
    return tuple(out)
```

```python
import functools

import jax
import jax.numpy as jnp
from jax import lax
from jax.experimental import pallas as pl
from jax.experimental.pallas import tpu as pltpu

F32 = jnp.float32
BF16 = jnp.bfloat16
HIGHEST = lax.Precision.HIGHEST

D_MODEL = 1024
DEPTH = 4
N_META = 16
N_BRANCH = 4
MIX_W = 256
N_HEADS = 4
HEAD_DIM = 64
CONV_W = 4
LRU_C = 8.0
N_EXPERTS = 32
TOP_K = 4
D_FF = 1024
SWIGLU_LIMIT = 7.0
SWIGLU_ALPHA = 1.702
LN_EPS = 1e-5
NORM_EPS = 1e-6
DEEPNORM_ALPHA = (2 * DEPTH) ** 0.25

LANES = 128
SUBLANES = 8
BD = 256
VMEM_LIMIT_BYTES = 56 * 1024 * 1024
PROMPT_CHUNK = 48
MOE_BLOCK_ROWS = 256
COMBINE_ROWS = 128

_A0, _B0, _C0, _D0, _G0 = 0, 512, 1544, 2568, 3344

_NN = (((1,), (0,)), ((), ()))
_NT = (((1,), (1,)), ((), ()))
_TN = (((0,), (0,)), ((), ()))


def _mm(a, b, dims=_NN, precision=None):
    return lax.dot_general(a, b, dims, precision=precision, preferred_element_type=F32)


def _bmm(a, b, dims=_NN):
    return _mm(a.astype(BF16), b.astype(BF16), dims)


def _split_bf16(x):
    hi = x.astype(BF16)
    return hi, (x - hi.astype(F32)).astype(BF16)


def _mm3(a, b, dims=_NN):
    ah, al = _split_bf16(a)
    bh, bl = _split_bf16(b)
    return _mm(ah, bh, dims) + (_mm(al, bh, dims) + _mm(ah, bl, dims))


def _mm_sel(x, sel, dims=_NN, terms=2):
    out = None
    r = x
    for _ in range(terms):
        p = r.astype(BF16)
        t = _mm(p, sel, dims)
        out = t if out is None else out + t
        r = r - p.astype(F32)
    return out


def _sigmoid(x):
    return jax.nn.sigmoid(x)


def _softplus(x):
    return jnp.maximum(x, 0.0) + jnp.log1p(jnp.exp(-jnp.abs(x)))


def _log_sigmoid(x):
    return jnp.minimum(x, 0.0) - jnp.log1p(jnp.exp(-jnp.abs(x)))


def _expm1(x):
    return jnp.tanh(0.5 * x) * (jnp.exp(x) + 1.0)


def _silu(x):
    return x * _sigmoid(x)


def _gelu_tanh(x):
    return 0.5 * x * (1.0 + jnp.tanh(0.7978845608028654 * (x + 0.044715 * (x * x * x))))


def _layernorm(x, g, b):
    mu = jnp.mean(x, axis=-1, keepdims=True)
    xc = x - mu
    var = jnp.mean(xc * xc, axis=-1, keepdims=True)
    return xc * lax.rsqrt(var + LN_EPS) * g + b


def _cparams(*sem):
    return pltpu.CompilerParams(dimension_semantics=sem, vmem_limit_bytes=VMEM_LIMIT_BYTES)


def _full(a):
    nd = a.ndim
    return pl.BlockSpec(a.shape, lambda *_: (0,) * nd)


def _ln_kernel(x_ref, g_ref, b_ref, o_ref):
    o_ref[...] = _layernorm(x_ref[...], g_ref[...], b_ref[...])


def _ln_call(x2, g, b, tm):
    t, d = x2.shape
    return pl.pallas_call(
        _ln_kernel,
        out_shape=jax.ShapeDtypeStruct((t, d), F32),
        grid=(t // tm,),
        in_specs=[pl.BlockSpec((tm, d), lambda i: (i, 0)), _full(g), _full(b)],
        out_specs=pl.BlockSpec((tm, d), lambda i: (i, 0)),
        compiler_params=_cparams("parallel"),
        name="ln_rows",
    )(x2, g, b)


def _combine_ln2_kernel(n, dcur_ref, dnxt_ref, x_ref, gate_ref, g_ref, b_ref, ys_hbm, o_ref, buf_ref, sem_ref):
    i = pl.program_id(0)
    tm = x_ref.shape[0]
    nrow = TOP_K * tm

    def row_copy(d_ref, r, slot):
        return pltpu.make_async_copy(ys_hbm.at[pl.ds(d_ref[0, 0, r], 1)], buf_ref.at[slot, pl.ds(r, 1)],
                                     sem_ref.at[slot])

    def issue(d_ref, slot):
        for r in range(nrow):
            row_copy(d_ref, r, slot).start()

    @pl.when(i == 0)
    def _():
        issue(dcur_ref, 0)

    @pl.when(i + 1 < n)
    def _():
        issue(dnxt_ref, (i + 1) % 2)

    slot = i % 2
    pltpu.make_async_copy(ys_hbm.at[pl.ds(0, nrow)], buf_ref.at[slot], sem_ref.at[slot]).wait()
    gate = gate_ref[...]
    ff = None
    for k in range(TOP_K):
        term = buf_ref[slot, pl.ds(k * tm, tm), :] * gate[:, k:k + 1]
        ff = term if ff is None else ff + term
    o_ref[...] = _layernorm(DEEPNORM_ALPHA * x_ref[...] + ff, g_ref[...], b_ref[...])


def _combine_ln2_call(x2, gates, dest_tiles, ys, g, b):
    t, d = x2.shape
    tm = COMBINE_ROWS
    n = t // tm
    row = pl.BlockSpec((tm, d), lambda i: (i, 0))
    dspec = lambda f: pl.BlockSpec((1, 1, TOP_K * tm), f, memory_space=pltpu.SMEM)
    return pl.pallas_call(
        functools.partial(_combine_ln2_kernel, n),
        out_shape=jax.ShapeDtypeStruct((t, d), F32),
        grid=(n,),
        in_specs=[dspec(lambda i: (i, 0, 0)), dspec(lambda i: (jnp.minimum(i + 1, n - 1), 0, 0)), row,
                  pl.BlockSpec((tm, LANES), lambda i: (i, 0)), _full(g), _full(b),
                  pl.BlockSpec(memory_space=pl.ANY)],
        out_specs=row,
        scratch_shapes=[pltpu.VMEM((2, TOP_K * tm, d), F32), pltpu.SemaphoreType.DMA((2,))],
        compiler_params=_cparams("arbitrary"),
        name="combine_ln2",
    )(dest_tiles, dest_tiles, x2, gates, g, b, ys)


def _merge_kernel(x_ref, ya_ref, yb_ref, yc_ref, yd_ref, wg_ref, wbr_ref, wout_ref, g_ref, b_ref, wr_ref, br_ref,
                  x1_ref, te_ref, tg_ref):
    x = x_ref[...]
    xb = x.astype(BF16)
    merged = None
    for n, y_ref in enumerate((ya_ref, yb_ref, yc_ref, yd_ref)):
        up = _mm(y_ref[...].astype(BF16), wbr_ref[n])
        gate = _sigmoid(_mm(xb, wg_ref[:, n * D_MODEL:(n + 1) * D_MODEL]))
        merged = up * gate if merged is None else merged + up * gate
    out = _mm(merged.astype(BF16), wout_ref[...])
    x1 = _layernorm(DEEPNORM_ALPHA * x + out, g_ref[...], b_ref[...])
    x1_ref[...] = x1
    logits = _mm(x1.astype(BF16), wr_ref[...]) + br_ref[...]
    lane = lax.broadcasted_iota(jnp.int32, logits.shape, 1)
    vals = []
    idxs = []
    v = logits
    for _ in range(TOP_K):
        m = jnp.max(v, axis=-1, keepdims=True)
        idx = jnp.min(jnp.where(v == m, lane, LANES), axis=-1, keepdims=True)
        vals.append(m)
        idxs.append(idx)
        v = jnp.where(lane == idx, -jnp.inf, v)
    es = [jnp.exp(m - vals[0]) for m in vals]
    den = es[0] + es[1] + es[2] + es[3]
    te = jnp.zeros(logits.shape, jnp.int32)
    tg = jnp.zeros(logits.shape, F32)
    for j in range(TOP_K):
        te = jnp.where(lane == j, idxs[j], te)
        tg = jnp.where(lane == j, es[j] / den, tg)
    te_ref[...] = te
    tg_ref[...] = tg


def _merge_call(x2, ys, wg, wbr, wout, g, b, wr, br, tm):
    t, d = x2.shape
    row = pl.BlockSpec((tm, d), lambda i: (i, 0))
    yrow = pl.BlockSpec((tm, MIX_W), lambda i: (i, 0))
    lrow = pl.BlockSpec((tm, LANES), lambda i: (i, 0))
    return pl.pallas_call(
        _merge_kernel,
        out_shape=(jax.ShapeDtypeStruct((t, d), F32),
                   jax.ShapeDtypeStruct((t, LANES), jnp.int32), jax.ShapeDtypeStruct((t, LANES), F32)),
        grid=(t // tm,),
        in_specs=[row, yrow, yrow, yrow, yrow, _full(wg), _full(wbr), _full(wout), _full(g), _full(b),
                  _full(wr), _full(br)],
        out_specs=(row, lrow, lrow),
        compiler_params=_cparams("parallel"),
        name="merge_ln_router",
    )(x2, *ys, wg, wbr, wout, g, b, wr, br)


def _dispatch_kernel(n_p, n_s, n, d_ref, xp_ref, xs_ref, o_hbm, buf_ref, sem_ref):
    i = pl.program_id(0)
    tm = xp_ref.shape[0]
    nrow = TOP_K * tm
    slot = i % 2

    def wait_slot(s):
        pltpu.make_async_copy(o_hbm.at[pl.ds(0, nrow)], o_hbm.at[pl.ds(0, nrow)], sem_ref.at[s]).wait()

    @pl.when(i >= 2)
    def _():
        wait_slot(slot)

    @pl.when(i < n_p)
    def _():
        buf_ref[slot] = xp_ref[...]

    @pl.when((i >= n_p) & (i < n_p + n_s))
    def _():
        buf_ref[slot] = xs_ref[...]

    @pl.when(i >= n_p + n_s)
    def _():
        buf_ref[slot] = jnp.zeros((tm, D_MODEL), F32)

    def row_copy(k, t):
        return pltpu.make_async_copy(buf_ref.at[slot, pl.ds(t, 1)], o_hbm.at[pl.ds(d_ref[0, 0, k * tm + t], 1)],
                                     sem_ref.at[slot])

    for k in range(TOP_K):
        for t in range(tm):
            row_copy(k, t).start()

    @pl.when(i == n - 1)
    def _():
        if n >= 2:
            wait_slot(1 - slot)
        wait_slot(slot)


def _dispatch_call(x_p, x_s, dest_tiles, mp):
    tm = COMBINE_ROWS
    d = x_p.shape[1]
    n_p, n_s, n = x_p.shape[0] // tm, x_s.shape[0] // tm, dest_tiles.shape[0]
    return pl.pallas_call(
        functools.partial(_dispatch_kernel, n_p, n_s, n),
        out_shape=jax.ShapeDtypeStruct((mp, d), F32),
        grid=(n,),
        in_specs=[pl.BlockSpec((1, 1, TOP_K * tm), lambda i: (i, 0, 0), memory_space=pltpu.SMEM),
                  pl.BlockSpec((tm, d), lambda i: (jnp.minimum(i, n_p - 1), 0)),
                  pl.BlockSpec((tm, d), lambda i: (jnp.clip(i - n_p, 0, n_s - 1), 0))],
        out_specs=pl.BlockSpec(memory_space=pl.ANY),
        scratch_shapes=[pltpu.VMEM((2, tm, d), F32), pltpu.SemaphoreType.DMA((2,))],
        compiler_params=_cparams("arbitrary"),
        name="moe_dispatch",
    )(dest_tiles, x_p, x_s)


def _moe_kernel(be_ref, nused_ref, xs_ref, wu_ref, bu_ref, wd_ref, bd_ref, o_ref, wub_ref, wdb_ref):
    i = pl.program_id(0)

    @pl.when((i == 0) | (be_ref[i] != be_ref[jnp.maximum(i - 1, 0)]))
    def _():
        wub_ref[...] = wu_ref[0, 0].astype(BF16)
        wdb_ref[...] = wd_ref[0, 0].astype(BF16)

    @pl.when(i < nused_ref[0])
    def _():
        h = _mm(xs_ref[...].astype(BF16), wub_ref[...]) + bu_ref[0, 0]
        g = jnp.minimum(h[:, :D_FF], SWIGLU_LIMIT)
        u = jnp.clip(h[:, D_FF:], -SWIGLU_LIMIT, SWIGLU_LIMIT)
        act = (u + 1.0) * g * _sigmoid(SWIGLU_ALPHA * g)
        o_ref[...] = _mm(act.astype(BF16), wdb_ref[...]) + bd_ref[0, 0]

    @pl.when(i >= nused_ref[0])
    def _():
        o_ref[...] = jnp.zeros(o_ref.shape, F32)


def _moe_call(layer, block_e, n_used, xs, wu, bu, wd, bd):
    mp, d = xs.shape
    nb = mp // MOE_BLOCK_ROWS
    grid_spec = pltpu.PrefetchScalarGridSpec(
        num_scalar_prefetch=2,
        grid=(nb,),
        in_specs=[
            pl.BlockSpec((MOE_BLOCK_ROWS, d), lambda i, be, nu: (i, 0)),
            pl.BlockSpec((1, 1, d, 2 * D_FF), lambda i, be, nu: (layer, be[i], 0, 0)),
            pl.BlockSpec((1, 1, 1, 2 * D_FF), lambda i, be, nu: (layer, be[i], 0, 0)),
            pl.BlockSpec((1, 1, D_FF, d), lambda i, be, nu: (layer, be[i], 0, 0)),
            pl.BlockSpec((1, 1, 1, d), lambda i, be, nu: (layer, be[i], 0, 0)),
        ],
        out_specs=pl.BlockSpec((MOE_BLOCK_ROWS, d), lambda i, be, nu: (i, 0)),
        scratch_shapes=[pltpu.VMEM((d, 2 * D_FF), BF16), pltpu.VMEM((D_FF, d), BF16)],
    )
    return pl.pallas_call(
        _moe_kernel,
        out_shape=jax.ShapeDtypeStruct((mp, d), F32),
        grid_spec=grid_spec,
        compiler_params=_cparams("arbitrary"),
        name="moe_experts",
    )(block_e, n_used, xs, wu, bu, wd, bd)


def _chunk_pos(bb, c, width):
    return lax.broadcasted_iota(jnp.int32, (bb, c, width), 1).reshape(bb * c, width)


def _seg_scan(x, pos, c, op, ident):
    s = 1
    while s < c:
        x = op(x, jnp.where(pos >= s, pltpu.roll(x, s, 0), ident))
        s *= 2
    return x


def _chunk_last(x, pos, bb, c):
    w = x.shape[1]
    x3 = jnp.where(pos == c - 1, x, 0.0).reshape(bb, c, w)
    return jnp.broadcast_to(jnp.sum(x3, axis=1, keepdims=True), (bb, c, w)).reshape(bb * c, w)


def _causal_conv(x3, ext_ref, prev_ref, cw_ref, c):
    ext_ref[:, 0:SUBLANES, :] = prev_ref[...]
    ext_ref[:, SUBLANES:SUBLANES + c, :] = x3
    y = None
    for j in range(CONV_W):
        lo = SUBLANES - (CONV_W - 1) + j
        tap = ext_ref[:, lo:lo + c, :] * cw_ref[j:j + 1, :]
        y = tap if y is None else y + tap
    prev_ref[...] = ext_ref[:, c:c + SUBLANES, :]
    return y


def _head_masks(width=MIX_W):
    lane = lax.broadcasted_iota(jnp.int32, (1, width), 1)
    return [(lane >= h * HEAD_DIM) & (lane < (h + 1) * HEAD_DIM) for h in range(N_HEADS)]


def _lane_masks():
    lane = lax.broadcasted_iota(jnp.int32, (1, LANES), 1)
    return [lane == h for h in range(N_HEADS)]


def _stack(xg, masks, g, c, cs):
    pieces = []
    for b in range(g):
        xb = xg[b * c:(b + 1) * c]
        for m in masks:
            pieces.append(jnp.where(m, xb, 0.0))
            if cs > c:
                pieces.append(jnp.zeros((cs - c, xg.shape[1]), xg.dtype))
    return jnp.concatenate(pieces, axis=0)


def _unstack(y, g, c, cs):
    outs = []
    for b in range(g):
        acc = None
        for h in range(N_HEADS):
            r0 = (b * N_HEADS + h) * cs
            piece = y[r0:r0 + c]
            acc = piece if acc is None else acc + piece
        outs.append(acc)
    return outs[0] if g == 1 else jnp.concatenate(outs, axis=0)


def _bd_masks(cs):
    r = lax.broadcasted_iota(jnp.int32, (BD, BD), 0)
    q = lax.broadcasted_iota(jnp.int32, (BD, BD), 1)
    shift = cs.bit_length() - 1
    same = (r >> shift) == (q >> shift)
    tr = r & (cs - 1)
    tq = q & (cs - 1)
    return same & (tq <= tr), same & (tq < tr)


def _head_block_mask():
    r = lax.broadcasted_iota(jnp.int32, (MIX_W, MIX_W), 0)
    q = lax.broadcasted_iota(jnp.int32, (MIX_W, MIX_W), 1)
    return (r // HEAD_DIM) == (q // HEAD_DIM)


def _head_sum(x, ones_bd):
    return _mm_sel(x, ones_bd)


def _to_hb(cols, expand):
    return _mm_sel(cols, expand, terms=2)


def _stack_cols(cols_g, g, c, cs):
    st = _stack(cols_g, _lane_masks(), g, c, cs)
    col = jnp.sum(st, axis=1, keepdims=True)
    ones = jnp.ones((SUBLANES, LANES), F32)
    row = _mm(ones, st, _NT, precision=HIGHEST)[0:1, :]
    return col, row


def _group_rows(ref, gi, g, c):
    w = ref.shape[2]
    if g == 1:
        return ref[gi]
    return ref[pl.ds(gi * g, g)].reshape(g * c, w)


def _paired_loop(n, compute_many, commit, width=2):
    while n % width:
        width //= 2

    def trip(j, carry):
        idx = [width * j + w for w in range(width)]
        for i, out in zip(idx, compute_many(idx)):
            commit(i, out)
        return carry

    lax.fori_loop(0, n // width, trip, 0)


def _seq_specs(bb, c):
    x_spec = pl.BlockSpec((bb, c, D_MODEL), lambda i, k: (i, k, 0))
    y_spec = pl.BlockSpec((bb, c, MIX_W), lambda i, k: (i, k, 0))
    return x_spec, y_spec


def _state_spec(bb, *tail):
    nt = len(tail)
    return pl.BlockSpec((bb,) + tuple(tail), lambda i, k: (i,) + (0,) * nt)


def _lru_kernel(x_ref, w_ref, cw_ref, cb_ref, wra_ref, bra_ref, wix_ref, bix_ref, lam_ref, conv0_ref, h0_ref,
                y_ref, conv1_ref, h1_ref, ext_ref, prev_ref, hst_ref):
    k = pl.program_id(1)
    bb, c, _ = x_ref.shape
    n = bb * c

    @pl.when(k == 0)
    def _():
        prev_ref[...] = conv0_ref[...]
        hst_ref[...] = h0_ref[...]

    xb = x_ref[...].reshape(n, D_MODEL).astype(BF16)
    proj = _mm(xb, w_ref[...])
    a_gate = proj[:, MIX_W:]
    xa = _causal_conv(proj[:, :MIX_W].reshape(bb, c, MIX_W), ext_ref, prev_ref, cw_ref, c) + cb_ref[...]
    xa = xa.reshape(n, MIX_W)
    xab = xa.astype(BF16)
    r = _sigmoid(_mm(xab, wra_ref[...]) + bra_ref[...])
    i = _sigmoid(_mm(xab, wix_ref[...]) + bix_ref[...])
    log_a = -LRU_C * r * _softplus(-lam_ref[...])
    a_cum = jnp.exp(log_a)
    b_cum = jnp.sqrt(-_expm1(2.0 * log_a)) * (i * xa)
    pos = _chunk_pos(bb, c, MIX_W)
    s = 1
    while s < c:
        keep = pos >= s
        a_sh = pltpu.roll(a_cum, s, 0)
        b_sh = pltpu.roll(b_cum, s, 0)
        b_cum = jnp.where(keep, a_cum * b_sh + b_cum, b_cum)
        a_cum = jnp.where(keep, a_cum * a_sh, a_cum)
        s *= 2
    h0 = jnp.broadcast_to(hst_ref[:, SUBLANES - 1:SUBLANES, :], (bb, c, MIX_W)).reshape(n, MIX_W)
    h = a_cum * h0 + b_cum
    y_ref[...] = (h * _gelu_tanh(a_gate)).reshape(bb, c, MIX_W)
    hst_ref[...] = h.reshape(bb, c, MIX_W)[:, c - SUBLANES:c, :]

    @pl.when(k == pl.num_programs(1) - 1)
    def _():
        conv1_ref[...] = prev_ref[...]
        h1_ref[...] = hst_ref[...]


def _lru_call(x3, w, cw, cb, wra, bra, wix, bix, lam, conv0, h0, bb, c):
    b, l, _ = x3.shape
    x_spec, y_spec = _seq_specs(bb, c)
    st = _state_spec(bb, SUBLANES, MIX_W)
    consts = (w, cw, cb, wra, bra, wix, bix, lam)
    return pl.pallas_call(
        _lru_kernel,
        out_shape=(jax.ShapeDtypeStruct((b, l, MIX_W), F32), jax.ShapeDtypeStruct((b, SUBLANES, MIX_W), F32),
                   jax.ShapeDtypeStruct((b, SUBLANES, MIX_W), F32)),
        grid=(b // bb, l // c),
        in_specs=[x_spec] + [_full(a) for a in consts] + [st, st],
        out_specs=(y_spec, st, st),
        scratch_shapes=[pltpu.VMEM((bb, c + SUBLANES, MIX_W), F32), pltpu.VMEM((bb, SUBLANES, MIX_W), F32),
                        pltpu.VMEM((bb, SUBLANES, MIX_W), F32)],
        compiler_params=_cparams("parallel", "arbitrary"),
        name="mixer_rglru",
    )(x3, *consts, conv0, h0)


def _hgrn_kernel(x_ref, w_ref, wlo_ref, loglb_ref, log1mlb_ref, onemlb_ref, ng_ref, ones_ref, st0_ref,
                 y_ref, st1_ref, st_ref, q_s, k_s, i_s, b_s, qe_s, kd_s, el_s, o_s):
    kk = pl.program_id(1)
    bb, c, _ = x_ref.shape
    n = bb * c
    nblk = c // SUBLANES

    @pl.when(kk == 0)
    def _():
        st_ref[...] = st0_ref[...]

    xb, xlo = _split_bf16(x_ref[...].reshape(n, D_MODEL))
    proj = _mm(xb, w_ref[...])
    q = proj[:, 0:MIX_W] * HEAD_DIM ** -0.5
    z = proj[:, MIX_W:2 * MIX_W] + _mm(xlo, w_ref[:, MIX_W:2 * MIX_W]) + _mm(xb, wlo_ref[...])
    iv = proj[:, 2 * MIX_W:3 * MIX_W]
    cg = proj[:, 3 * MIX_W:4 * MIX_W]
    lo = loglb_ref[...]
    hi = log1mlb_ref[...] + _log_sigmoid(z)
    log_f = jnp.maximum(lo, hi) + jnp.log1p(jnp.exp(-jnp.abs(lo - hi)))
    kc = onemlb_ref[...] * _sigmoid(-z)
    pos = _chunk_pos(bb, c, MIX_W)
    bc = _seg_scan(log_f, pos, c, jnp.add, 0.0)
    b_last = _chunk_last(bc, pos, bb, c)
    ones_bd = ones_ref[...]

    sub = pos & (SUBLANES - 1)
    o_band = None
    for d in range(SUBLANES):
        kr = pltpu.roll(kc, d, 0) if d else kc
        br = pltpu.roll(bc, d, 0) if d else bc
        ir = pltpu.roll(iv, d, 0) if d else iv
        wd = jnp.where(sub >= d, q * kr * jnp.exp(jnp.minimum(bc - br, 0.0)), 0.0)
        term = _head_sum(wd, ones_bd) * ir
        o_band = term if o_band is None else o_band + term

    q_s[...] = q.reshape(bb, c, MIX_W)
    k_s[...] = kc.reshape(bb, c, MIX_W)
    i_s[...] = iv.reshape(bb, c, MIX_W)
    b_s[...] = bc.reshape(bb, c, MIX_W)
    qe_s[...] = (q * jnp.exp(bc)).reshape(bb, c, MIX_W)
    kd_s[...] = (kc * jnp.exp(b_last - bc)).reshape(bb, c, MIX_W)
    el_s[...] = jnp.exp(b_last).reshape(bb, c, MIX_W)

    hmasks = _head_masks()
    blockmask = _head_block_mask()
    pad_rows = LANES - c
    rowid = lax.broadcasted_iota(jnp.int32, (LANES, MIX_W), 0)
    zpad = jnp.zeros((pad_rows, MIX_W), F32)

    def per_bs(bs):
        nb = range(len(bs))
        sts = [st_ref[b] for b in bs]
        os_ = [_bmm(qe_s[b], sts[j], _NT) for j, b in enumerate(bs)]
        if nblk > 1:
            kp = [jnp.concatenate([k_s[b], zpad], axis=0) for b in bs]
            bp = [jnp.concatenate([b_s[b], zpad], axis=0) for b in bs]
            ip = [jnp.concatenate([i_s[b], zpad], axis=0).astype(BF16) for b in bs]
            pieces = [[jnp.zeros((SUBLANES, MIX_W), F32)] for _ in bs]
            for blk in range(1, nblk):
                r0 = blk * SUBLANES
                atts = []
                for j, b in enumerate(bs):
                    ref_b = b_s[b, r0 - 1:r0, :]
                    qi = q_s[b, r0:r0 + SUBLANES, :] * jnp.exp(
                        jnp.minimum(b_s[b, r0:r0 + SUBLANES, :] - ref_b, 0.0))
                    ki = jnp.where(rowid < r0, kp[j] * jnp.exp(jnp.minimum(ref_b - bp[j], 0.0)), 0.0)
                    qst = jnp.concatenate([jnp.where(m, qi, 0.0) for m in hmasks], axis=0)
                    atts.append(_bmm(qst, ki, _NT))
                ress = [_mm(atts[j].astype(BF16), ip[j]) for j in nb]
                for j in nb:
                    acc = None
                    for h in range(N_HEADS):
                        part = jnp.where(hmasks[h], ress[j][h * SUBLANES:(h + 1) * SUBLANES], 0.0)
                        acc = part if acc is None else acc + part
                    pieces[j].append(acc)
            os_ = [os_[j] + jnp.concatenate(pieces[j], axis=0) for j in nb]
        upds = [_mm3(jnp.concatenate([i_s[b], zpad], axis=0), jnp.concatenate([kd_s[b], zpad], axis=0), _TN)
                for b in bs]
        return [(os_[j], sts[j] * el_s[b, 0:1, :] + jnp.where(blockmask, upds[j], 0.0)) for j, b in enumerate(bs)]

    def commit(b, outs):
        o_s[b] = outs[0]
        st_ref[b] = outs[1]

    _paired_loop(bb, per_bs, commit, width=8)

    o = o_s[...].reshape(n, MIX_W) + o_band
    ms = _head_sum(o * o, ones_bd) * (1.0 / HEAD_DIM)
    y = o * lax.rsqrt(ms + NORM_EPS) * ng_ref[...] * _silu(cg)
    y_ref[...] = y.reshape(bb, c, MIX_W)

    @pl.when(kk == pl.num_programs(1) - 1)
    def _():
        st1_ref[...] = st_ref[...]


def _hgrn_call(x3, w, wlo, loglb, log1mlb, onemlb, ng, ones_bd, st0, bb, c):
    b, l, _ = x3.shape
    x_spec, y_spec = _seq_specs(bb, c)
    st = _state_spec(bb, MIX_W, MIX_W)
    consts = (w, wlo, loglb, log1mlb, onemlb, ng, ones_bd)
    rows = pltpu.VMEM((bb, c, MIX_W), F32)
    return pl.pallas_call(
        _hgrn_kernel,
        out_shape=(jax.ShapeDtypeStruct((b, l, MIX_W), F32), jax.ShapeDtypeStruct((b, MIX_W, MIX_W), F32)),
        grid=(b // bb, l // c),
        in_specs=[x_spec] + [_full(a) for a in consts] + [st],
        out_specs=(y_spec, st),
        scratch_shapes=[pltpu.VMEM((bb, MIX_W, MIX_W), F32)] + [rows] * 8,
        compiler_params=_cparams("parallel", "arbitrary"),
        name="mixer_hgrn2",
    )(x3, *consts, st0)


def _mlstm_kernel(g, cs, x_ref, w_ref, wlo_ref, cw_ref, cb_ref, wq_ref, wk_ref, bi_ref, bf_ref, ng_ref, ones_ref, exp_ref,
                  conv0_ref, c0_ref, n0_ref, m0_ref,
                  y_ref, conv1_ref, c1_ref, n1_ref, m1_ref,
                  ext_ref, prev_ref, cst_ref, nst_ref, mst_ref, q_s, k_s, v_s, kw_s, dec_s, a_s, g_s, num_s, den_s,
                  qc_s):
    kk = pl.program_id(1)
    bb, c, _ = x_ref.shape
    n = bb * c
    ngroups = bb // g

    @pl.when(kk == 0)
    def _():
        prev_ref[...] = conv0_ref[...]
        cst_ref[...] = c0_ref[...]
        nst_ref[...] = n0_ref[...]
        mst_ref[...] = m0_ref[...]

    xb, xlo = _split_bf16(x_ref[...].reshape(n, D_MODEL))
    proj = _mm(xb, w_ref[...])
    gates = proj[:, 3 * MIX_W:] + _mm(xlo, w_ref[:, 3 * MIX_W:]) + _mm(xb, wlo_ref[...])
    xm = _causal_conv(proj[:, 0:MIX_W].reshape(bb, c, MIX_W), ext_ref, prev_ref, cw_ref, c) + cb_ref[...]
    xm = _silu(xm).reshape(n, MIX_W).astype(BF16)
    q = _mm(xm, wq_ref[...])
    k = _mm(xm, wk_ref[...]) * HEAD_DIM ** -0.5
    v = proj[:, MIX_W:2 * MIX_W]
    d_o = proj[:, 2 * MIX_W:3 * MIX_W]
    log_i = gates[:, 0:LANES] + bi_ref[...]
    log_f = _log_sigmoid(gates[:, LANES:2 * LANES] + bf_ref[...])
    pos = _chunk_pos(bb, c, LANES)
    bc = _seg_scan(log_f, pos, c, jnp.add, 0.0)
    gg = log_i - bc
    cm = _seg_scan(gg, pos, c, jnp.maximum, -jnp.inf)
    m0 = jnp.broadcast_to(mst_ref[:, SUBLANES - 1:SUBLANES, :], (bb, c, LANES)).reshape(n, LANES)
    m_t = jnp.maximum(m0 + bc, bc + cm)
    b_last = _chunk_last(bc, pos, bb, c)
    m_last = _chunk_last(m_t, pos, bb, c)
    expand = exp_ref[...]
    ones_bd = ones_ref[...]
    inter = _to_hb(jnp.exp(m0 + bc - m_t), expand)
    e_negm = _to_hb(jnp.exp(-m_t), expand)
    wl = _to_hb(jnp.exp(gg + b_last - m_last), expand)
    nrows = jnp.broadcast_to(nst_ref[:, 0:1, :], (bb, c, MIX_W)).reshape(n, MIX_W)
    qn = _head_sum(q * nrows, ones_bd)

    q_s[...] = q.reshape(bb, c, MIX_W)
    k_s[...] = k.reshape(bb, c, MIX_W)
    v_s[...] = v.reshape(bb, c, MIX_W)
    kw_s[...] = (k * wl).reshape(bb, c, MIX_W)
    dec_s[...] = _to_hb(jnp.exp(m0 + b_last - m_last), expand).reshape(bb, c, MIX_W)
    a_s[...] = (bc - m_t).reshape(bb, c, LANES)
    g_s[...] = gg.reshape(bb, c, LANES)

    hmasks = _head_masks()
    causal, _ = _bd_masks(cs)
    rowb = lax.broadcasted_iota(jnp.int32, (BD, MIX_W), 0) // (N_HEADS * cs)
    sel = _stack(jnp.ones((g * c, MIX_W), F32), hmasks, g, c, cs)

    def per_groups(gis):
        ng = range(len(gis))
        qg = [_group_rows(q_s, gi, g, c) for gi in gis]
        qst = [_stack(qg[j], hmasks, g, c, cs).astype(BF16) for j in ng]
        kst = [_stack(_group_rows(k_s, gi, g, c), hmasks, g, c, cs).astype(BF16) for gi in gis]
        vst = [_stack(_group_rows(v_s, gi, g, c), hmasks, g, c, cs).astype(BF16) for gi in gis]
        kwst = [_stack(_group_rows(kw_s, gi, g, c), hmasks, g, c, cs) for gi in gis]
        acol = [_stack_cols(_group_rows(a_s, gi, g, c), g, c, cs)[0] for gi in gis]
        grow = [_stack_cols(_group_rows(g_s, gi, g, c), g, c, cs)[1] for gi in gis]
        qk = [_mm(qst[j], kst[j], _NT) for j in ng]
        cmats = [[cst_ref[gi * g + bl] for bl in range(g)] for gi in gis]
        qcs = [[_bmm(qg[j][bl * c:(bl + 1) * c], cmats[j][bl]) for bl in range(g)] for j in ng]
        qkw = [qk[j] * jnp.where(causal, jnp.exp(jnp.minimum(acol[j] + grow[j], 0.0)), 0.0) for j in ng]
        num = [_unstack(_mm(qkw[j].astype(BF16), vst[j]), g, c, cs) for j in ng]
        den = [_unstack(jnp.sum(qkw[j], axis=1, keepdims=True) * sel, g, c, cs) for j in ng]
        outs = []
        for j, gi in enumerate(gis):
            outs_g = []
            for bl in range(g):
                b = gi * g + bl
                kwb = kwst[j] if g == 1 else jnp.where(rowb == bl, kwst[j], 0.0)
                dec_row = dec_s[b, 0:1, :]
                ksum = jnp.sum(kw_s[b], axis=0, keepdims=True)
                outs_g.append((qcs[j][bl], num[j][bl * c:(bl + 1) * c], den[j][bl * c:(bl + 1) * c],
                               cmats[j][bl] * dec_row + _mm(kwb.astype(BF16), vst[j], _TN),
                               nst_ref[b] * dec_row + jnp.broadcast_to(ksum, (SUBLANES, MIX_W))))
            outs.append(outs_g)
        return outs

    def commit(gi, outs):
        for bl, (qc_b, num_b, den_b, c_b, n_b) in enumerate(outs):
            b = gi * g + bl
            qc_s[b] = qc_b
            num_s[b] = num_b
            den_s[b] = den_b
            cst_ref[b] = c_b
            nst_ref[b] = n_b

    _paired_loop(ngroups, per_groups, commit, width=8)

    num = num_s[...].reshape(n, MIX_W) + inter * qc_s[...].reshape(n, MIX_W)
    den = den_s[...].reshape(n, MIX_W) + inter * qn
    h = num / jnp.maximum(jnp.abs(den), e_negm)
    mu = _head_sum(h, ones_bd) * (1.0 / HEAD_DIM)
    hc = h - mu
    var = _head_sum(hc * hc, ones_bd) * (1.0 / HEAD_DIM)
    y = hc * lax.rsqrt(var + NORM_EPS) * ng_ref[...] * _sigmoid(d_o)
    y_ref[...] = y.reshape(bb, c, MIX_W)
    mst_ref[...] = m_t.reshape(bb, c, LANES)[:, c - SUBLANES:c, :]

    @pl.when(kk == pl.num_programs(1) - 1)
    def _():
        conv1_ref[...] = prev_ref[...]
        c1_ref[...] = cst_ref[...]
        n1_ref[...] = nst_ref[...]
        m1_ref[...] = mst_ref[...]


def _mlstm_call(x3, w, wlo, cw, cb, wq, wk, bi, bf, ng, ones_bd, expand, conv0, c0, n0, m0, bb, c, g, cs):
    b, l, _ = x3.shape
    x_spec, y_spec = _seq_specs(bb, c)
    st_conv = _state_spec(bb, SUBLANES, MIX_W)
    st_c = _state_spec(bb, MIX_W, MIX_W)
    st_n = _state_spec(bb, SUBLANES, MIX_W)
    st_m = _state_spec(bb, SUBLANES, LANES)
    consts = (w, wlo, cw, cb, wq, wk, bi, bf, ng, ones_bd, expand)
    rows = pltpu.VMEM((bb, c, MIX_W), F32)
    cols = pltpu.VMEM((bb, c, LANES), F32)
    return pl.pallas_call(
        functools.partial(_mlstm_kernel, g, cs),
        out_shape=(jax.ShapeDtypeStruct((b, l, MIX_W), F32), jax.ShapeDtypeStruct((b, SUBLANES, MIX_W), F32),
                   jax.ShapeDtypeStruct((b, MIX_W, MIX_W), F32), jax.ShapeDtypeStruct((b, SUBLANES, MIX_W), F32),
                   jax.ShapeDtypeStruct((b, SUBLANES, LANES), F32)),
        grid=(b // bb, l // c),
        in_specs=[x_spec] + [_full(a) for a in consts] + [st_conv, st_c, st_n, st_m],
        out_specs=(y_spec, st_conv, st_c, st_n, st_m),
        scratch_shapes=[pltpu.VMEM((bb, c + SUBLANES, MIX_W), F32), pltpu.VMEM((bb, SUBLANES, MIX_W), F32),
                        pltpu.VMEM((bb, MIX_W, MIX_W), F32), pltpu.VMEM((bb, SUBLANES, MIX_W), F32),
                        pltpu.VMEM((bb, SUBLANES, LANES), F32),
                        rows, rows, rows, rows, rows, cols, cols, rows, rows, rows],
        compiler_params=_cparams("parallel", "arbitrary"),
        name="mixer_mlstm",
    )(x3, *consts, conv0, c0, n0, m0)


def _gdn_kernel(g, cs, x_ref, w_ref, wlo_ref, cw_ref, alog_ref, dtb_ref, ng_ref, ones_ref, exp_ref, conv0_ref, s0_ref,
                y_ref, conv1_ref, s1_ref,
                ext_ref, prev_ref, sst_ref, q_s, k_s, vb_s, kb_s, qe_s, kd_s, el_s, gc_s, be_s, o_s):
    kk = pl.program_id(1)
    bb, c, _ = x_ref.shape
    n = bb * c
    ngroups = bb // g
    qkv_w = 3 * MIX_W

    @pl.when(kk == 0)
    def _():
        prev_ref[...] = conv0_ref[...]
        sst_ref[...] = s0_ref[...]

    xb, xlo = _split_bf16(x_ref[...].reshape(n, D_MODEL))
    proj = _mm(xb, w_ref[...])
    gates = proj[:, 4 * MIX_W:] + _mm(xlo, w_ref[:, 4 * MIX_W:]) + _mm(xb, wlo_ref[...])
    qkv = _causal_conv(proj[:, 0:qkv_w].reshape(bb, c, qkv_w), ext_ref, prev_ref, cw_ref, c)
    qkv = _silu(qkv).reshape(n, qkv_w)
    ones_bd = ones_ref[...]
    expand = exp_ref[...]
    q = qkv[:, 0:MIX_W]
    k = qkv[:, MIX_W:2 * MIX_W]
    v = qkv[:, 2 * MIX_W:3 * MIX_W]
    q = q * lax.rsqrt(_head_sum(q * q, ones_bd) + NORM_EPS) * HEAD_DIM ** -0.5
    k = k * lax.rsqrt(_head_sum(k * k, ones_bd) + NORM_EPS)
    z = proj[:, qkv_w:qkv_w + MIX_W]
    a_in = gates[:, 0:LANES]
    b_in = gates[:, LANES:2 * LANES]
    gdec = -jnp.exp(alog_ref[...]) * _softplus(a_in + dtb_ref[...])
    beta = _sigmoid(b_in)
    pos = _chunk_pos(bb, c, LANES)
    gcum = _seg_scan(gdec, pos, c, jnp.add, 0.0)
    g_last = _chunk_last(gcum, pos, bb, c)
    beta_hb = _to_hb(beta, expand)
    eg_hb = _to_hb(jnp.exp(gcum), expand)

    q_s[...] = q.reshape(bb, c, MIX_W)
    k_s[...] = k.reshape(bb, c, MIX_W)
    vb_s[...] = (v * beta_hb).reshape(bb, c, MIX_W)
    kb_s[...] = (k * beta_hb * eg_hb).reshape(bb, c, MIX_W)
    qe_s[...] = (q * eg_hb).reshape(bb, c, MIX_W)
    kd_s[...] = (k * _to_hb(jnp.exp(g_last - gcum), expand)).reshape(bb, c, MIX_W)
    el_s[...] = _to_hb(jnp.exp(g_last), expand).reshape(bb, c, MIX_W)
    gc_s[...] = gcum.reshape(bb, c, LANES)
    be_s[...] = beta.reshape(bb, c, LANES)

    hmasks = _head_masks()
    causal, strict = _bd_masks(cs)
    rowb = lax.broadcasted_iota(jnp.int32, (BD, MIX_W), 0) // (N_HEADS * cs)
    rows_b = N_HEADS * cs
    sel = _stack(jnp.ones((g * c, MIX_W), F32), hmasks, g, c, cs) > 0.0

    def per_groups(gis):
        ng = range(len(gis))
        qg = [_group_rows(qe_s, gi, g, c) for gi in gis]
        qst = [_stack(_group_rows(q_s, gi, g, c), hmasks, g, c, cs).astype(BF16) for gi in gis]
        kst = [_stack(_group_rows(k_s, gi, g, c), hmasks, g, c, cs).astype(BF16) for gi in gis]
        kdst = [_stack(_group_rows(kd_s, gi, g, c), hmasks, g, c, cs) for gi in gis]
        x = [_stack(_group_rows(vb_s, gi, g, c), hmasks, g, c, cs)
             + pltpu.roll(_stack(_group_rows(kb_s, gi, g, c), hmasks, g, c, cs), HEAD_DIM, 1) for gi in gis]
        gcr = [_stack_cols(_group_rows(gc_s, gi, g, c), g, c, cs) for gi in gis]
        bcol = [_stack_cols(_group_rows(be_s, gi, g, c), g, c, cs)[0] for gi in gis]
        decay = [jnp.exp(jnp.minimum(gcr[j][0] - gcr[j][1], 0.0)) for j in ng]
        kk = [_mm(kst[j], kst[j], _NT) for j in ng]
        p = [jnp.where(strict, bcol[j] * kk[j] * decay[j], 0.0) for j in ng]
        ax = [_bmm(p[j], x[j]) for j in ng]
        x = [x[j] - ax[j] for j in ng]
        span = 2
        while span < cs:
            p = [_bmm(p[j], p[j]) for j in ng]
            px = [_bmm(p[j], x[j]) for j in ng]
            x = [x[j] + px[j] for j in ng]
            span *= 2
        qk = [_mm(qst[j], kst[j], _NT) for j in ng]
        smats = [[sst_ref[gi * g + bl] for bl in range(g)] for gi in gis]
        w_v = [jnp.where(sel, x[j], 0.0) for j in ng]
        w_k = [jnp.where(sel, pltpu.roll(x[j], MIX_W - HEAD_DIM, 1), 0.0) for j in ng]
        ws = [[_bmm(w_k[j][bl * rows_b:(bl + 1) * rows_b], smats[j][bl]) for bl in range(g)] for j in ng]
        ub = []
        for j in ng:
            us = [w_v[j][bl * rows_b:(bl + 1) * rows_b] - ws[j][bl] for bl in range(g)]
            ub.append((us[0] if g == 1 else jnp.concatenate(us, axis=0)).astype(BF16))
        qkd = [jnp.where(causal, qk[j] * decay[j], 0.0).astype(BF16) for j in ng]
        o_intra = [_unstack(_mm(qkd[j], ub[j]), g, c, cs) for j in ng]
        outs = []
        for j, gi in enumerate(gis):
            outs_g = []
            for bl in range(g):
                b = gi * g + bl
                o_b = o_intra[j][bl * c:(bl + 1) * c] + _bmm(qg[j][bl * c:(bl + 1) * c], smats[j][bl])
                kdb = kdst[j] if g == 1 else jnp.where(rowb == bl, kdst[j], 0.0)
                outs_g.append((o_b, smats[j][bl] * el_s[b, 0:1, :] + _mm(kdb.astype(BF16), ub[j], _TN)))
            outs.append(outs_g)
        return outs

    def commit(gi, outs):
        for bl, (o_b, s_b) in enumerate(outs):
            o_s[gi * g + bl] = o_b
            sst_ref[gi * g + bl] = s_b

    _paired_loop(ngroups, per_groups, commit, width=8)

    o = o_s[...].reshape(n, MIX_W)
    ms = _head_sum(o * o, ones_bd) * (1.0 / HEAD_DIM)
    y = o * lax.rsqrt(ms + NORM_EPS) * ng_ref[...] * _silu(z)
    y_ref[...] = y.reshape(bb, c, MIX_W)

    @pl.when(kk == pl.num_programs(1) - 1)
    def _():
        conv1_ref[...] = prev_ref[...]
        s1_ref[...] = sst_ref[...]


def _gdn_call(x3, w, wlo, cw, alog, dtb, ng, ones_bd, expand, conv0, s0, bb, c, g, cs):
    b, l, _ = x3.shape
    x_spec, y_spec = _seq_specs(bb, c)
    st_conv = _state_spec(bb, SUBLANES, 3 * MIX_W)
    st_s = _state_spec(bb, MIX_W, MIX_W)
    consts = (w, wlo, cw, alog, dtb, ng, ones_bd, expand)
    rows = pltpu.VMEM((bb, c, MIX_W), F32)
    cols = pltpu.VMEM((bb, c, LANES), F32)
    return pl.pallas_call(
        functools.partial(_gdn_kernel, g, cs),
        out_shape=(jax.ShapeDtypeStruct((b, l, MIX_W), F32), jax.ShapeDtypeStruct((b, SUBLANES, 3 * MIX_W), F32),
                   jax.ShapeDtypeStruct((b, MIX_W, MIX_W), F32)),
        grid=(b // bb, l // c),
        in_specs=[x_spec] + [_full(a) for a in consts] + [st_conv, st_s],
        out_specs=(y_spec, st_conv, st_s),
        scratch_shapes=[pltpu.VMEM((bb, c + SUBLANES, 3 * MIX_W), F32), pltpu.VMEM((bb, SUBLANES, 3 * MIX_W), F32),
                        pltpu.VMEM((bb, MIX_W, MIX_W), F32),
                        rows, rows, rows, rows, rows, rows, rows, cols, cols, rows],
        compiler_params=_cparams("parallel", "arbitrary"),
        name="mixer_gdn",
    )(x3, *consts, conv0, s0)


def _pad_cols(a, width):
    return jnp.pad(a, ((0, 0), (0, width - a.shape[1])))


def _row(a, width=None):
    a = a.reshape(1, -1).astype(F32)
    return a if width is None else _pad_cols(a, width)


def _block_diag(w4):
    h, d, _ = w4.shape
    eye = jnp.eye(h, dtype=w4.dtype)
    return jnp.einsum("hij,hg->higj", w4, eye).reshape(h * d, h * d)


def _bd_state(s):
    b, h, d, _ = s.shape
    eye = jnp.eye(h, dtype=s.dtype)
    return jnp.einsum("bhij,hg->bhigj", s, eye).reshape(b, h * d, h * d)


def _bd_blocks(s):
    b = s.shape[0]
    s5 = s.reshape(b, N_HEADS, HEAD_DIM, N_HEADS, HEAD_DIM)
    return jnp.stack([s5[:, h, :, h, :] for h in range(N_HEADS)], axis=1)


def _tail8(buf):
    return jnp.pad(buf, ((0, 0), (SUBLANES - (CONV_W - 1), 0), (0, 0)))


def _bcast8(a):
    return jnp.broadcast_to(a[:, None, :], (a.shape[0], SUBLANES, a.shape[1]))


def _layer_params(P, lb, l):
    w_f32 = P["w_in"][l].astype(F32)
    w_in = w_f32.astype(BF16)
    w_lo = (w_f32 - w_in.astype(F32)).astype(BF16)

    def gate_tiles(w, lo, hi):
        return [_pad_cols(w[:, lo:lo + N_HEADS], LANES), _pad_cols(w[:, lo + N_HEADS:hi], LANES)]

    p = {}
    p["w_a"] = w_in[:, _A0:_B0]
    p["w_b"] = jnp.concatenate([w_in[:, _B0:_B0 + 4 * MIX_W]] + gate_tiles(w_in, _B0 + 4 * MIX_W, _C0), axis=1)
    p["w_b_lo"] = jnp.concatenate(gate_tiles(w_lo, _B0 + 4 * MIX_W, _C0), axis=1)
    p["w_c"] = w_in[:, _C0:_D0]
    p["w_c_lo"] = w_lo[:, _C0 + MIX_W:_C0 + 2 * MIX_W]
    p["w_d"] = jnp.concatenate([w_in[:, _D0:_D0 + 3 * MIX_W]] + gate_tiles(w_in, _D0 + 3 * MIX_W, _G0), axis=1)
    p["w_d_lo"] = jnp.concatenate(gate_tiles(w_lo, _D0 + 3 * MIX_W, _G0), axis=1)
    p["w_g"] = w_in[:, _G0:]
    p["lru_cw"] = P["w_lru_conv"][l]
    p["lru_cb"] = _row(P["b_lru_conv"][l])
    p["lru_wra"] = _block_diag(P["w_lru_ra"][l]).astype(BF16)
    p["lru_bra"] = _row(P["b_lru_ra"][l])
    p["lru_wix"] = _block_diag(P["w_lru_ix"][l]).astype(BF16)
    p["lru_bix"] = _row(P["b_lru_ix"][l])
    p["lru_lam"] = _row(P["lru_lambda"][l])
    p["gdn_cw"] = P["w_gdn_conv"][l]
    p["gdn_alog"] = _row(P["gdn_a_log"][l], LANES)
    p["gdn_dtb"] = _row(P["gdn_dt_bias"][l], LANES)
    p["gdn_ng"] = _row(jnp.tile(P["gdn_norm_g"][l], N_HEADS))
    p["hg_loglb"] = _row(jnp.log(lb[l]))
    p["hg_log1mlb"] = _row(jnp.log1p(-lb[l]))
    p["hg_onemlb"] = _row(1.0 - lb[l])
    p["hg_ng"] = _row(jnp.tile(P["hg_norm_g"][l], N_HEADS))
    p["ml_cw"] = P["w_ml_conv"][l]
    p["ml_cb"] = _row(P["b_ml_conv"][l])
    p["ml_wq"] = _block_diag(P["w_ml_q"][l]).astype(BF16)
    p["ml_wk"] = _block_diag(P["w_ml_k"][l]).astype(BF16)
    p["ml_bi"] = _row(P["b_ml_i"][l], LANES)
    p["ml_bf"] = _row(P["b_ml_f"][l], LANES)
    p["ml_ng"] = _row(P["ml_norm_g"][l])
    p["w_br"] = P["w_branch"][l].astype(BF16)
    p["w_out"] = P["w_out"][l].astype(BF16)
    p["ln1_g"] = _row(P["ln1_g"][l])
    p["ln1_b"] = _row(P["ln1_b"][l])
    p["ln2_g"] = _row(P["ln2_g"][l])
    p["ln2_b"] = _row(P["ln2_b"][l])
    p["w_r"] = _pad_cols(P["w_router"][l], LANES).astype(BF16)
    p["b_r"] = jnp.concatenate([P["b_router"][l].astype(F32), jnp.full((LANES - N_EXPERTS,), -1e30, F32)]).reshape(1, LANES)
    return p


def _mixers(x3, p, st, consts, bb, c, g, cs):
    ones_bd, expand = consts
    lru_conv, lru_h, gdn_conv, gdn_s, hg_st, ml_conv, ml_c, ml_n, ml_m = st
    y_a, lru_conv1, lru_h1 = _lru_call(x3, p["w_a"], p["lru_cw"], p["lru_cb"], p["lru_wra"], p["lru_bra"],
                                       p["lru_wix"], p["lru_bix"], p["lru_lam"], lru_conv, lru_h, bb, c)
    y_b, gdn_conv1, gdn_s1 = _gdn_call(x3, p["w_b"], p["w_b_lo"], p["gdn_cw"], p["gdn_alog"], p["gdn_dtb"], p["gdn_ng"],
                                       ones_bd, expand, gdn_conv, gdn_s, bb, c, g, cs)
    y_c, hg_st1 = _hgrn_call(x3, p["w_c"], p["w_c_lo"], p["hg_loglb"], p["hg_log1mlb"], p["hg_onemlb"], p["hg_ng"], ones_bd,
                             hg_st, bb, c)
    y_d, ml_conv1, ml_c1, ml_n1, ml_m1 = _mlstm_call(x3, p["w_d"], p["w_d_lo"], p["ml_cw"], p["ml_cb"], p["ml_wq"], p["ml_wk"],
                                                     p["ml_bi"], p["ml_bf"], p["ml_ng"], ones_bd, expand,
                                                     ml_conv, ml_c, ml_n, ml_m, bb, c, g, cs)
    return (y_a, y_b, y_c, y_d), (lru_conv1, lru_h1, gdn_conv1, gdn_s1, hg_st1, ml_conv1, ml_c1, ml_n1, ml_m1)


def _states_to_kernel(st):
    lru_conv, lru_h, gdn_conv, gdn_s, hg_s, ml_conv, ml_c, ml_n, ml_m = [s.astype(F32) for s in st]
    b = lru_h.shape[0]
    return (_tail8(lru_conv), _bcast8(lru_h), _tail8(gdn_conv), _bd_state(gdn_s),
            jnp.swapaxes(_bd_state(hg_s), 1, 2), _tail8(ml_conv), _bd_state(ml_c),
            _bcast8(ml_n.reshape(b, MIX_W)), _bcast8(_pad_cols(ml_m, LANES)))


def _states_from_kernel(st):
    lru_conv, lru_h, gdn_conv, gdn_s, hg_st, ml_conv, ml_c, ml_n, ml_m = st
    b = lru_h.shape[0]
    tail = SUBLANES - (CONV_W - 1)
    return (lru_conv[:, tail:], lru_h[:, SUBLANES - 1], gdn_conv[:, tail:], _bd_blocks(gdn_s),
            _bd_blocks(jnp.swapaxes(hg_st, 1, 2)), ml_conv[:, tail:], _bd_blocks(ml_c),
            ml_n[:, 0].reshape(b, N_HEADS, HEAD_DIM), ml_m[:, SUBLANES - 1, :N_HEADS])


def _route(top_e):
    t = top_e.shape[0]
    m = t * TOP_K
    flat_e = top_e.reshape(m)
    onehot = (flat_e[:, None] == jnp.arange(N_EXPERTS, dtype=jnp.int32)[None, :]).astype(jnp.int32)
    csum = jnp.cumsum(onehot, axis=0)
    rank = jnp.sum(onehot * csum, axis=1) - 1
    counts = csum[-1]
    padded = (counts + MOE_BLOCK_ROWS - 1) // MOE_BLOCK_ROWS * MOE_BLOCK_ROWS
    pad_ends = jnp.cumsum(padded)
    dest = (pad_ends - padded)[flat_e] + rank
    n_blocks = m // MOE_BLOCK_ROWS + N_EXPERTS
    mp = n_blocks * MOE_BLOCK_ROWS
    block_start = jnp.arange(n_blocks, dtype=jnp.int32) * MOE_BLOCK_ROWS
    block_e = jnp.minimum(jnp.sum((pad_ends[None, :] <= block_start[:, None]).astype(jnp.int32), axis=1),
                          N_EXPERTS - 1)
    n_used = (pad_ends[-1] // MOE_BLOCK_ROWS).astype(jnp.int32).reshape(1)
    n_pad = padded - counts
    j = jnp.arange(MOE_BLOCK_ROWS, dtype=jnp.int32)[None, :]
    is_pad = (j < n_pad[:, None]).reshape(-1)
    pad_row = ((pad_ends - n_pad)[:, None] + j).reshape(-1)
    tail_rank = jnp.cumsum(jnp.logical_not(is_pad).astype(jnp.int32)) - 1
    zero_rows = jnp.where(is_pad, pad_row, pad_ends[-1] + tail_rank).astype(jnp.int32)
    return dest.reshape(t, TOP_K), zero_rows, block_e, n_used, mp


def _dest_tiles(dest):
    t = dest.shape[0]
    n = t // COMBINE_ROWS
    return jnp.swapaxes(dest.reshape(n, COMBINE_ROWS, TOP_K), 1, 2).reshape(n, 1, TOP_K * COMBINE_ROWS)


def _trunk_layer(layer, xs, states, p, moe_w, consts, cfgs):
    x1s, tes, tgs, new_states = [], [], [], []
    for x3, st, (bb, c, g, cs, tm) in zip(xs, states, cfgs):
        b, l, _ = x3.shape
        ys, st1 = _mixers(x3, p, st, consts, bb, c, g, cs)
        x1, te, tg = _merge_call(x3.reshape(b * l, D_MODEL), [y.reshape(b * l, MIX_W) for y in ys],
                                 p["w_g"], p["w_br"], p["w_out"], p["ln1_g"], p["ln1_b"], p["w_r"], p["b_r"], tm)
        x1s.append(x1)
        tes.append(te[:, :TOP_K])
        tgs.append(tg)
        new_states.append(st1)
    dest, zero_rows, block_e, n_used, mp = _route(jnp.concatenate(tes, axis=0))
    d_tiles = []
    off = 0
    for x1 in x1s:
        d_tiles.append(_dest_tiles(lax.slice_in_dim(dest, off, off + x1.shape[0], axis=0)))
        off += x1.shape[0]
    z_tiles = zero_rows.reshape(-1, 1, TOP_K * COMBINE_ROWS)
    rows = _dispatch_call(x1s[0], x1s[1], jnp.concatenate(d_tiles + [z_tiles], axis=0), mp)
    ys = _moe_call(layer, block_e, n_used, rows, *moe_w)
    outs = []
    for x3, x1, tg, dt in zip(xs, x1s, tgs, d_tiles):
        outs.append(_combine_ln2_call(x1, tg, dt, ys, p["ln2_g"], p["ln2_b"]).reshape(x3.shape))
    return outs, new_states


def _row_tile(t):
    for tm in (512, 384, 256, 128, 64, 32, 16):
        if t % tm == 0:
            return tm
    raise ValueError(f"no row tile for {t} rows")


def _group_cfg(b, l):
    if l % PROMPT_CHUNK == 0:
        return (min(b, 8), PROMPT_CHUNK, 1, BD // N_HEADS, _row_tile(b * l))
    assert l == SUBLANES, "sequence length must be a multiple of the prompt chunk or one sublane tile"
    g = BD // (N_HEADS * SUBLANES)
    bb = min(b, 16)
    assert bb % g == 0 and b % bb == 0
    return (bb, SUBLANES, g, SUBLANES, _row_tile(b * l))


def _zero_states(b):
    return (jnp.zeros((b, CONV_W - 1, MIX_W), F32), jnp.zeros((b, MIX_W), F32),
            jnp.zeros((b, CONV_W - 1, 3 * MIX_W), F32), jnp.zeros((b, N_HEADS, HEAD_DIM, HEAD_DIM), F32),
            jnp.zeros((b, N_HEADS, HEAD_DIM, HEAD_DIM), F32), jnp.zeros((b, CONV_W - 1, MIX_W), F32),
            jnp.zeros((b, N_HEADS, HEAD_DIM, HEAD_DIM), F32), jnp.zeros((b, N_HEADS, HEAD_DIM), F32),
            jnp.zeros((b, N_HEADS), F32))


def kernel(x_prompt, x_sample, state_lru_conv, state_lru_h, state_gdn_conv, state_gdn, state_hgrn, state_mlstm_conv, state_mlstm_c, state_mlstm_n, state_mlstm_m, meta_tokens, ln_emb_g, ln_emb_b, hg_lb_logits, w_in, w_lru_conv, b_lru_conv, w_lru_ra, b_lru_ra, w_lru_ix, b_lru_ix, lru_lambda, w_gdn_conv, gdn_a_log, gdn_dt_bias, gdn_norm_g, hg_norm_g, w_ml_conv, b_ml_conv, w_ml_q, w_ml_k, b_ml_i, b_ml_f, ml_norm_g, w_branch, w_out, ln1_g, ln1_b, ln2_g, ln2_b, w_router, b_router, w_up, b_up, w_down, b_down):
    P = dict(w_in=w_in, w_lru_conv=w_lru_conv, b_lru_conv=b_lru_conv, w_lru_ra=w_lru_ra, b_lru_ra=b_lru_ra,
             w_lru_ix=w_lru_ix, b_lru_ix=b_lru_ix, lru_lambda=lru_lambda, w_gdn_conv=w_gdn_conv,
             gdn_a_log=gdn_a_log, gdn_dt_bias=gdn_dt_bias, gdn_norm_g=gdn_norm_g, hg_norm_g=hg_norm_g,
             w_ml_conv=w_ml_conv, b_ml_conv=b_ml_conv, w_ml_q=w_ml_q, w_ml_k=w_ml_k, b_ml_i=b_ml_i,
             b_ml_f=b_ml_f, ml_norm_g=ml_norm_g, w_branch=w_branch, w_out=w_out, ln1_g=ln1_g, ln1_b=ln1_b,
             ln2_g=ln2_g, ln2_b=ln2_b, w_router=w_router, b_router=b_router, w_up=w_up, b_up=b_up,
             w_down=w_down, b_down=b_down)
    depth = w_in.shape[0]
    lb_cum = jnp.cumsum(jax.nn.softmax(hg_lb_logits.astype(F32), axis=0), axis=0)
    lb = lb_cum - lb_cum[0:1]

    lane = jnp.arange(MIX_W) // HEAD_DIM
    ones_bd = (lane[:, None] == lane[None, :]).astype(BF16)
    expand = (jnp.arange(LANES)[:, None] == lane[None, :]).astype(BF16)
    consts = (ones_bd, expand)
    moe_w = (w_up.astype(F32), b_up.reshape(depth, N_EXPERTS, 1, 2 * D_FF).astype(F32),
             w_down.astype(F32), b_down.reshape(depth, N_EXPERTS, 1, D_MODEL).astype(F32))

    bp, lp0, _ = x_prompt.shape
    bs, ls, _ = x_sample.shape
    meta = jnp.broadcast_to(meta_tokens.astype(F32)[None], (bp, N_META, D_MODEL))
    xp_in = jnp.concatenate([meta, x_prompt], axis=1)
    lp = lp0 + N_META
    cfgs = [_group_cfg(bp, lp), _group_cfg(bs, ls)]
    g_emb, b_emb = _row(ln_emb_g), _row(ln_emb_b)
    xp = _ln_call(xp_in.reshape(bp * lp, D_MODEL), g_emb, b_emb, cfgs[0][4]).reshape(bp, lp, D_MODEL)
    xs_ = _ln_call(x_sample.reshape(bs * ls, D_MODEL), g_emb, b_emb, cfgs[1][4]).reshape(bs, ls, D_MODEL)

    sample_states = (state_lru_conv, state_lru_h, state_gdn_conv, state_gdn, state_hgrn,
                     state_mlstm_conv, state_mlstm_c, state_mlstm_n, state_mlstm_m)
    zero_p = _zero_states(bp)
    xs = [xp, xs_]
    collected = ([], [])
    for l in range(depth):
        p = _layer_params(P, lb, l)
        states = [_states_to_kernel(zero_p), _states_to_kernel(tuple(s[l] for s in sample_states))]
        xs, new_states = _trunk_layer(l, xs, states, p, moe_w, consts, cfgs)
        for grp in range(2):
            collected[grp].append(_states_from_kernel(new_states[grp]))
    new_p = tuple(jnp.stack([layer[i] for layer in collected[0]]) for i in range(9))
    new_s = tuple(jnp.stack([layer[i] for layer in collected[1]]) for i in range(9))
    y_prompt = xs[0][:, N_META:]
    y_sample = xs[1]
    out = [y_prompt, y_sample]
    for i in range(9):
        out.append(new_p[i])
        out.append(new_s[i])
    return tuple(out)
```
